```python
import math
import jax
import jax.numpy as jnp
from jax import lax
import numpy as np

D_MODEL = 2048
BATCH = 4
SEQ = 2048
DEPTH = 2
DEC_BATCH = 128
DEC_SEQ = 8
PAST_LEN = 16384
PAGE_SIZE = 128

EPS = 1e-6
GLA_WIDTH = D_MODEL // 2
GLA_HEADS = 4
GLA_DV = GLA_WIDTH // GLA_HEADS
GLA_DK = GLA_DV // 2
GLA_LOWRANK = 16
GK_NORM = 16.0
GLA_CHUNK = 64
POOL_WIDTH = D_MODEL - GLA_WIDTH
POOL_WINDOWS = (2, 4, 8, 16)
POOL_GROUPS = 4
POOL_GC = POOL_WIDTH // POOL_GROUPS
POOL_BUF = 15
D_FF = ((8 * D_MODEL // 3 + 255) // 256) * 256
Q_END = GLA_HEADS * GLA_DK
K_END = Q_END + GLA_HEADS * GLA_DK
V_END = K_END + GLA_WIDTH
G_END = V_END + GLA_WIDTH
GK_END = G_END + GLA_LOWRANK
IN_WIDTH = GK_END + POOL_WIDTH

kernel_name = "hymba_gla_pool_decoder_step"


def rmsnorm(x, g):
    xf = x.astype(jnp.float32)
    r = lax.rsqrt(jnp.mean(xf * xf, axis=-1, keepdims=True) + EPS)
    return (xf * r).astype(x.dtype) * g


def gla(q, k, v, gk, s0):
    B, T = q.shape[0], q.shape[1]
    c = math.gcd(T, GLA_CHUNK)
    n = T // c

    def split(a):
        return a.reshape(B, n, c, a.shape[2], a.shape[3]).transpose(1, 0, 3, 2, 4)

    qc, kc, vc, gc = split(q), split(k), split(v), split(gk)
    causal = jnp.tril(jnp.ones((c, c), dtype=bool))[:, :, None]

    def step(S, inp):
        qi, ki, vi, gi = inp
        b = jnp.cumsum(gi.astype(jnp.float32), axis=2)
        diff = b[:, :, :, None, :] - b[:, :, None, :, :]
        decay = jnp.exp(jnp.where(causal, diff, -jnp.inf))
        scores = jnp.einsum('bhid,bhjd,bhijd->bhij', qi.astype(jnp.float32), ki.astype(jnp.float32), decay)
        o_intra = jnp.einsum('bhij,bhjv->bhiv', scores, vi.astype(jnp.float32))
        o_inter = jnp.einsum('bhid,bhdv->bhiv', qi.astype(jnp.float32) * jnp.exp(b), S)
        b_last = b[:, :, -1:, :]
        k_dec = ki.astype(jnp.float32) * jnp.exp(b_last - b)
        S_new = jnp.exp(b_last[:, :, 0, :])[..., None] * S + jnp.einsum('bhjd,bhjv->bhdv', k_dec, vi.astype(jnp.float32))
        return S_new, o_intra + o_inter

    S_fin, o = lax.scan(step, s0.astype(jnp.float32), (qc, kc, vc, gc))
    o = o.transpose(1, 0, 3, 2, 4).reshape(B, T, GLA_HEADS, GLA_DV)
    return o.astype(q.dtype), S_fin.astype(s0.dtype)


def pool_mixer(u, buf, pos0, w_pool, pool_scale):
    B, T = u.shape[0], u.shape[1]
    ext = jnp.concatenate([buf.astype(u.dtype), u], axis=1)
    cs = jnp.cumsum(ext.astype(jnp.float32), axis=1)
    cs = jnp.pad(cs, ((0, 0), (1, 0), (0, 0)))
    pos = pos0 + jnp.arange(T)
    outs = []
    for i, w in enumerate(POOL_WINDOWS):
        lo, hi = i * POOL_GC, (i + 1) * POOL_GC
        s = cs[:, POOL_BUF + 1:POOL_BUF + 1 + T, lo:hi] - cs[:, POOL_BUF + 1 - w:POOL_BUF + 1 - w + T, lo:hi]
        cnt = jnp.minimum(w, pos + 1).astype(jnp.float32)[None, :, None]
        outs.append(s / cnt - u[..., lo:hi].astype(jnp.float32))
    d = jnp.stack(outs, axis=2).astype(u.dtype)
    y = jnp.einsum('btgc,gcd->btgd', d, w_pool).reshape(B, T, POOL_WIDTH) * pool_scale
    return y, ext[:, -POOL_BUF:]


def mixer(h, s_gla, buf, pos0, w_in, w_gk_up, b_gk, gla_norm, w_pool, pool_scale, w_out):
    B, T = h.shape[0], h.shape[1]
    proj = h @ w_in
    q = proj[..., :Q_END].reshape(B, T, GLA_HEADS, GLA_DK) * (GLA_DK ** -0.5)
    k = proj[..., Q_END:K_END].reshape(B, T, GLA_HEADS, GLA_DK)
    v = proj[..., K_END:V_END].reshape(B, T, GLA_HEADS, GLA_DV)
    g = proj[..., V_END:G_END]
    gk_lr = proj[..., G_END:GK_END]
    u = proj[..., GK_END:]
    gk = jax.nn.log_sigmoid((gk_lr @ w_gk_up + b_gk).astype(jnp.float32)) / GK_NORM
    gk = gk.reshape(B, T, GLA_HEADS, GLA_DK)
    o, s_new = gla(q, k, v, gk, s_gla)
    o = rmsnorm(o, gla_norm).reshape(B, T, GLA_WIDTH) * jax.nn.silu(g)
    p, buf_new = pool_mixer(u, buf, pos0, w_pool, pool_scale)
    out = jnp.concatenate([o, p.astype(o.dtype)], axis=-1) @ w_out
    return out, s_new, buf_new


def trunk(x, s_gla_all, buf_all, pos0, norm_mix, w_in, w_gk_up, b_gk, gla_norm, w_pool,
          pool_scale, w_out, norm_ffn, w_gate, w_up, w_down, norm_final):
    new_s, new_b = [], []
    for l in range(DEPTH):
        h = rmsnorm(x, norm_mix[l])
        m, s, b = mixer(h, s_gla_all[l], buf_all[l], pos0, w_in[l], w_gk_up[l], b_gk[l],
                        gla_norm[l], w_pool[l], pool_scale[l], w_out[l])
        x = x + m
        h = rmsnorm(x, norm_ffn[l])
        x = x + (jax.nn.silu(h @ w_gate[l]) * (h @ w_up[l])) @ w_down[l]
        new_s.append(s)
        new_b.append(b)
    return rmsnorm(x, norm_final), jnp.stack(new_s), jnp.stack(new_b)


def setup_inputs(seed: int = 0) -> dict:
    key = jax.random.key(seed)
    ks = jax.random.split(key, 20)
    f = jnp.float32
    nrm = lambda k, shape, s: jax.random.normal(k, shape, f) * s
    return {
        "x_prompt": nrm(ks[0], (BATCH, SEQ, D_MODEL), 1.0),
        "x_sample": nrm(ks[1], (DEC_BATCH, DEC_SEQ, D_MODEL), 1.0),
        "state_gla": nrm(ks[2], (DEPTH, DEC_BATCH, GLA_HEADS, GLA_DK, GLA_DV), 0.5),
        "state_pool": nrm(ks[3], (DEPTH, DEC_BATCH, POOL_BUF, POOL_WIDTH), 1.0),
        "norm_mix": 1.0 + nrm(ks[4], (DEPTH, D_MODEL), 0.01),
        "w_in": nrm(ks[5], (DEPTH, D_MODEL, IN_WIDTH), D_MODEL ** -0.5),
        "w_gk_up": nrm(ks[6], (DEPTH, GLA_LOWRANK, GLA_HEADS * GLA_DK), GLA_LOWRANK ** -0.5),
        "b_gk": nrm(ks[7], (DEPTH, GLA_HEADS * GLA_DK), 0.01),
        "gla_norm": 1.0 + nrm(ks[8], (DEPTH, GLA_DV), 0.01),
        "w_pool": nrm(ks[9], (DEPTH, POOL_GROUPS, POOL_GC, POOL_GC), POOL_GC ** -0.5),
        "pool_scale": 1.0 + nrm(ks[10], (DEPTH, POOL_WIDTH), 0.01),
        "w_out": nrm(ks[11], (DEPTH, D_MODEL, D_MODEL), D_MODEL ** -0.5),
        "norm_ffn": 1.0 + nrm(ks[12], (DEPTH, D_MODEL), 0.01),
        "w_gate": nrm(ks[13], (DEPTH, D_MODEL, D_FF), D_MODEL ** -0.5),
        "w_up": nrm(ks[14], (DEPTH, D_MODEL, D_FF), D_MODEL ** -0.5),
        "w_down": nrm(ks[15], (DEPTH, D_FF, D_MODEL), D_FF ** -0.5),
        "norm_final": 1.0 + nrm(ks[16], (D_MODEL,), 0.01),
    }


def reference(x_prompt, x_sample, state_gla, state_pool, norm_mix, w_in, w_gk_up, b_gk,
              gla_norm, w_pool, pool_scale, w_out, norm_ffn, w_gate, w_up, w_down, norm_final):
    s0_prompt = jnp.zeros((DEPTH, BATCH, GLA_HEADS, GLA_DK, GLA_DV), x_prompt.dtype)
    b0_prompt = jnp.zeros((DEPTH, BATCH, POOL_BUF, POOL_WIDTH), x_prompt.dtype)
    y_prompt, state_gla_prompt, state_pool_prompt = trunk(
        x_prompt, s0_prompt, b0_prompt, 0, norm_mix, w_in, w_gk_up, b_gk, gla_norm, w_pool,
        pool_scale, w_out, norm_ffn, w_gate, w_up, w_down, norm_final)
    y_sample, state_gla_sample, state_pool_sample = trunk(
        x_sample, state_gla, state_pool, PAST_LEN, norm_mix, w_in, w_gk_up, b_gk, gla_norm, w_pool,
        pool_scale, w_out, norm_ffn, w_gate, w_up, w_down, norm_final)
    return (y_prompt, y_sample, state_gla_prompt, state_pool_prompt, state_gla_sample, state_pool_sample)
```

```python
import functools

import jax
import jax.numpy as jnp
from jax import lax
from jax.experimental import pallas as pl
from jax.experimental.pallas import tpu as pltpu

F32 = jnp.float32
BF16 = jnp.bfloat16

D_MODEL = 2048
DEPTH = 2
EPS = 1e-6
GLA_WIDTH = D_MODEL // 2
GLA_HEADS = 4
GLA_DV = GLA_WIDTH // GLA_HEADS
GLA_DK = GLA_DV // 2
GLA_LOWRANK = 16
GK_NORM = 16.0
GLA_CHUNK = 64
POOL_WIDTH = D_MODEL - GLA_WIDTH
POOL_WINDOWS = (2, 4, 8, 16)
POOL_GC = POOL_WIDTH // len(POOL_WINDOWS)
POOL_BUF = 15
D_FF = ((8 * D_MODEL // 3 + 255) // 256) * 256
QK_WIDTH = GLA_HEADS * GLA_DK
MAIN_WIDTH = 2 * QK_WIDTH + 2 * GLA_WIDTH + POOL_WIDTH
K_OFF = QK_WIDTH
V_OFF = 2 * QK_WIDTH
G_OFF = V_OFF + GLA_WIDTH
U_OFF = G_OFF + GLA_WIDTH
Q_SCALE = GLA_DK ** -0.5
PAST_LEN = 16384

LANES = 128
SUBLANES = 8
VMEM_LIMIT_BYTES = 56 * 1024 * 1024

TM_PROJ = 1024
TN_PROJ = 512
TM_FFN = 512
TF_FFN = 512
TT_PROMPT = 256
SUB = 16
ROW_CHUNK = 128
SEQ_PER_STEP = 8
BUF_ROWS = POOL_BUF + 1


def _rmsnorm(x, g):
    r = lax.rsqrt(jnp.mean(x * x, axis=-1, keepdims=True) + EPS)
    return (x * r) * g


def _dot(a, b):
    return jnp.dot(a, b, preferred_element_type=F32)


def _dot_nt(a, b):
    return lax.dot_general(a, b, (((1,), (1,)), ((), ())), preferred_element_type=F32)


def _dot_tn(a, b):
    return lax.dot_general(a, b, (((0,), (0,)), ((), ())), preferred_element_type=F32)


def _split3(x):
    hi = x.astype(BF16)
    r1 = x - hi.astype(F32)
    mid = r1.astype(BF16)
    lo = (r1 - mid.astype(F32)).astype(BF16)
    return hi, mid, lo


def _cumsum_rows(tri, g):
    hi, mid, lo = _split3(g)
    b3 = _dot(tri, jnp.concatenate([hi, mid, lo], axis=1))
    n = g.shape[1]
    return b3[:, :n] + b3[:, n:2 * n] + b3[:, 2 * n:]


def _inproj_kernel(x_ref, nrm_ref, w_ref, wlr_ref, wup_ref, bgk_ref, main_ref, gk_ref, h_ref):
    j = pl.program_id(1)

    @pl.when(j == 0)
    def _():
        def body(r, carry):
            rows = pl.ds(pl.multiple_of(r * ROW_CHUNK, ROW_CHUNK), ROW_CHUNK)
            h_ref[rows, :] = _rmsnorm(x_ref[rows, :], nrm_ref[...]).astype(BF16)
            return carry

        lax.fori_loop(0, TM_PROJ // ROW_CHUNK, body, 0)
        lr = _dot(h_ref[...], wlr_ref[...])
        z = _dot(lr.astype(BF16), wup_ref[...]) + bgk_ref[...]
        gk_ref[...] = jax.nn.log_sigmoid(z) / GK_NORM

    main_ref[...] = _dot(h_ref[...], w_ref[...])


def _inproj(x, nrm, w_main, w_lr, w_up, b_gk):
    m = x.shape[0]
    grid = (m // TM_PROJ, MAIN_WIDTH // TN_PROJ)
    return pl.pallas_call(
        _inproj_kernel,
        grid=grid,
        in_specs=[
            pl.BlockSpec((TM_PROJ, D_MODEL), lambda i, j: (i, 0)),
            pl.BlockSpec((1, D_MODEL), lambda i, j: (0, 0)),
            pl.BlockSpec((D_MODEL, TN_PROJ), lambda i, j: (0, j)),
            pl.BlockSpec((D_MODEL, LANES), lambda i, j: (0, 0)),
            pl.BlockSpec((LANES, QK_WIDTH), lambda i, j: (0, 0)),
            pl.BlockSpec((1, QK_WIDTH), lambda i, j: (0, 0)),
        ],
        out_specs=[
            pl.BlockSpec((TM_PROJ, TN_PROJ), lambda i, j: (i, j)),
            pl.BlockSpec((TM_PROJ, QK_WIDTH), lambda i, j: (i, 0)),
        ],
        out_shape=[
            jax.ShapeDtypeStruct((m, MAIN_WIDTH), F32),
            jax.ShapeDtypeStruct((m, QK_WIDTH), F32),
        ],
        scratch_shapes=[pltpu.VMEM((TM_PROJ, D_MODEL), BF16)],
        compiler_params=pltpu.CompilerParams(
            dimension_semantics=("arbitrary", "arbitrary"), vmem_limit_bytes=VMEM_LIMIT_BYTES),
        name="inproj",
    )(x, nrm, w_main, w_lr, w_up, b_gk)


def _pool_group(ext_load, u_cols, pos, gi, wpool_ref, ps_ref):
    w = POOL_WINDOWS[gi]
    s = u_cols
    for sft in range(1, w):
        s = s + ext_load(sft)
    cnt = jnp.minimum(w, pos + 1).astype(F32)
    d = s / cnt - u_cols
    cols = slice(gi * POOL_GC, (gi + 1) * POOL_GC)
    return _dot(d.astype(BF16), wpool_ref[gi]) * ps_ref[:, cols]


def _gla_chunk(qh, kh, vh, gh, st, masks):
    tri, lane4, col_of_blk, below_blk, causal = masks
    c = GLA_CHUNK
    nb = c // SUB
    b = _cumsum_rows(tri, gh)

    off_rows = [jnp.zeros((SUB, c), F32)]
    for blk in range(1, nb):
        r0 = blk * SUB
        bs = b[r0:r0 + 1, :]
        qi = qh[r0:r0 + SUB, :] * jnp.exp(b[r0:r0 + SUB, :] - bs)
        kj = kh * jnp.exp(jnp.minimum(bs - b, 0.0))
        off_rows.append(_dot_nt(qi.astype(BF16), kj.astype(BF16)))
    s_off = jnp.concatenate(off_rows, axis=0)

    q4 = qh.reshape(nb, SUB, GLA_DK)
    k4 = kh.reshape(nb, SUB, GLA_DK)
    b4 = b.reshape(nb, SUB, GLA_DK)
    a4 = jnp.zeros((nb, SUB, c), F32)
    for jj in range(SUB):
        kj = k4[:, jj:jj + 1, :]
        bj = b4[:, jj:jj + 1, :]
        p = q4 * (kj * jnp.exp(jnp.minimum(b4 - bj, 0.0)))
        col = jnp.sum(p, axis=-1, keepdims=True)
        a4 = jnp.where(lane4 == col_of_blk + jj, col, a4)
    a = jnp.where(below_blk, s_off, a4.reshape(c, c))
    a = jnp.where(causal, a, 0.0)

    vb = vh.astype(BF16)
    o = _dot(a.astype(BF16), vb) + _dot_nt((qh * jnp.exp(b)).astype(BF16), st.astype(BF16))
    b_last = b[c - 1:c, :]
    k_dec = kh * jnp.exp(b_last - b)
    st_new = st * jnp.exp(b_last) + _dot_tn(vb, k_dec.astype(BF16))
    return o, st_new


def _gla_masks():
    c = GLA_CHUNK
    nb = c // SUB
    row = lax.broadcasted_iota(jnp.int32, (c, c), 0)
    col = lax.broadcasted_iota(jnp.int32, (c, c), 1)
    tri = (row >= col).astype(BF16)
    lane4 = lax.broadcasted_iota(jnp.int32, (nb, SUB, c), 2)
    col_of_blk = lax.broadcasted_iota(jnp.int32, (nb, SUB, c), 0) * SUB
    below_blk = col < (row - jnp.bitwise_and(row, SUB - 1))
    causal = col <= row
    return tri, lane4, col_of_blk, below_blk, causal


def _gated_head_out(o, gt, gn):
    return _rmsnorm(o, gn) * (gt * jax.nn.sigmoid(gt))


def _mixer_prompt_kernel(main_ref, gk_ref, s0_ref, buf0_ref, gn_ref, wpool_ref, ps_ref,
                         cat_ref, sout_ref, bufout_ref, st_ref, ext_ref, *, pos0):
    t = pl.program_id(1)
    nt = pl.num_programs(1)
    tt = TT_PROMPT

    @pl.when(t == 0)
    def _():
        for h in range(GLA_HEADS):
            st_ref[h] = s0_ref[0, h].T
        ext_ref[0:BUF_ROWS, :] = buf0_ref[0]

    ext_ref[BUF_ROWS:BUF_ROWS + tt, :] = main_ref[:, U_OFF:U_OFF + POOL_WIDTH]
    pos = pos0 + t * tt + lax.broadcasted_iota(jnp.int32, (tt, POOL_GC), 0)
    for gi in range(len(POOL_WINDOWS)):
        cols = slice(gi * POOL_GC, (gi + 1) * POOL_GC)
        ucols = slice(U_OFF + gi * POOL_GC, U_OFF + (gi + 1) * POOL_GC)
        y = _pool_group(lambda sft: ext_ref[BUF_ROWS - sft:BUF_ROWS - sft + tt, cols],
                        main_ref[:, ucols], pos, gi, wpool_ref, ps_ref)
        cat_ref[:, GLA_WIDTH + gi * POOL_GC:GLA_WIDTH + (gi + 1) * POOL_GC] = y.astype(BF16)
    carry = ext_ref[tt:tt + BUF_ROWS, :]
    ext_ref[0:BUF_ROWS, :] = carry
    bufout_ref[0] = carry

    masks = _gla_masks()

    def chunk_body(ci, carry_):
        rows = pl.ds(pl.multiple_of(ci * GLA_CHUNK, GLA_CHUNK), GLA_CHUNK)
        for h in range(GLA_HEADS):
            kcols = slice(h * GLA_DK, (h + 1) * GLA_DK)
            vcols = slice(h * GLA_DV, (h + 1) * GLA_DV)
            o, st_new = _gla_chunk(
                main_ref[rows, h * GLA_DK:(h + 1) * GLA_DK] * Q_SCALE,
                main_ref[rows, K_OFF + h * GLA_DK:K_OFF + (h + 1) * GLA_DK],
                main_ref[rows, V_OFF + h * GLA_DV:V_OFF + (h + 1) * GLA_DV],
                gk_ref[rows, kcols], st_ref[h], masks)
            st_ref[h] = st_new
            gt = main_ref[rows, G_OFF + h * GLA_DV:G_OFF + (h + 1) * GLA_DV]
            cat_ref[rows, vcols] = _gated_head_out(o, gt, gn_ref[...]).astype(BF16)
        return carry_

    lax.fori_loop(0, tt // GLA_CHUNK, chunk_body, 0)

    @pl.when(t == nt - 1)
    def _():
        for h in range(GLA_HEADS):
            sout_ref[0, h] = st_ref[h].T


def _mixer_prompt(main, gk, s0, buf0, gla_norm, w_pool, pool_scale, *, n_seq, seq_len, pos0, m_total):
    tt = TT_PROMPT
    nt = seq_len // tt
    row = lambda b, t: b * nt + t
    return pl.pallas_call(
        functools.partial(_mixer_prompt_kernel, pos0=pos0),
        grid=(n_seq, nt),
        in_specs=[
            pl.BlockSpec((tt, MAIN_WIDTH), lambda b, t: (row(b, t), 0)),
            pl.BlockSpec((tt, QK_WIDTH), lambda b, t: (row(b, t), 0)),
            pl.BlockSpec((1, GLA_HEADS, GLA_DK, GLA_DV), lambda b, t: (b, 0, 0, 0)),
            pl.BlockSpec((1, BUF_ROWS, POOL_WIDTH), lambda b, t: (b, 0, 0)),
            pl.BlockSpec((1, GLA_DV), lambda b, t: (0, 0)),
            pl.BlockSpec((len(POOL_WINDOWS), POOL_GC, POOL_GC), lambda b, t: (0, 0, 0)),
            pl.BlockSpec((1, POOL_WIDTH), lambda b, t: (0, 0)),
        ],
        out_specs=[
            pl.BlockSpec((tt, D_MODEL), lambda b, t: (row(b, t), 0)),
            pl.BlockSpec((1, GLA_HEADS, GLA_DK, GLA_DV), lambda b, t: (b, 0, 0, 0)),
            pl.BlockSpec((1, BUF_ROWS, POOL_WIDTH), lambda b, t: (b, 0, 0)),
        ],
        out_shape=[
            jax.ShapeDtypeStruct((m_total, D_MODEL), BF16),
            jax.ShapeDtypeStruct((n_seq, GLA_HEADS, GLA_DK, GLA_DV), F32),
            jax.ShapeDtypeStruct((n_seq, BUF_ROWS, POOL_WIDTH), F32),
        ],
        scratch_shapes=[
            pltpu.VMEM((GLA_HEADS, GLA_DV, GLA_DK), F32),
            pltpu.VMEM((BUF_ROWS + tt, POOL_WIDTH), F32),
        ],
        compiler_params=pltpu.CompilerParams(
            dimension_semantics=("arbitrary", "arbitrary"), vmem_limit_bytes=VMEM_LIMIT_BYTES),
        name="mixer_prompt",
    )(main, gk, s0, buf0, gla_norm, w_pool, pool_scale)


def _mixer_sample_kernel(main_ref, gk_ref, s0_ref, buf0_ref, gn_ref, wpool_ref, ps_ref, catin_ref,
                         cat_ref, sout_ref, bufout_ref, ext_ref, *, pos0, t_len):
    del catin_ref
    ns = SEQ_PER_STEP
    rows = ns * t_len

    ext_ref[:, 0:BUF_ROWS, :] = buf0_ref[...]
    ext_ref[:, BUF_ROWS:BUF_ROWS + t_len, :] = (
        main_ref[:, U_OFF:U_OFF + POOL_WIDTH].reshape(ns, t_len, POOL_WIDTH))
    pos = pos0 + lax.broadcasted_iota(jnp.int32, (ns, t_len, POOL_GC), 1)
    for gi in range(len(POOL_WINDOWS)):
        cols = slice(gi * POOL_GC, (gi + 1) * POOL_GC)
        u_cols = main_ref[:, U_OFF + gi * POOL_GC:U_OFF + (gi + 1) * POOL_GC].reshape(
            ns, t_len, POOL_GC)
        w = POOL_WINDOWS[gi]
        s = u_cols
        for sft in range(1, w):
            s = s + ext_ref[:, BUF_ROWS - sft:BUF_ROWS - sft + t_len, cols]
        cnt = jnp.minimum(w, pos + 1).astype(F32)
        d = (s / cnt - u_cols).reshape(rows, POOL_GC)
        y = _dot(d.astype(BF16), wpool_ref[gi]) * ps_ref[:, cols]
        cat_ref[:, GLA_WIDTH + gi * POOL_GC:GLA_WIDTH + (gi + 1) * POOL_GC] = y.astype(BF16)
    bufout_ref[...] = ext_ref[:, t_len:t_len + BUF_ROWS, :]

    r_i = lax.broadcasted_iota(jnp.int32, (rows, rows), 0)
    c_i = lax.broadcasted_iota(jnp.int32, (rows, rows), 1)
    same_seq = (r_i - jnp.bitwise_and(r_i, t_len - 1)) == (c_i - jnp.bitwise_and(c_i, t_len - 1))
    tri = jnp.logical_and(r_i >= c_i, same_seq).astype(BF16)
    row_in_seq = lax.broadcasted_iota(jnp.int32, (ns, t_len, 1), 1)
    seq_of_row = lax.broadcasted_iota(jnp.int32, (rows, 1), 0) // t_len
    zero_pad = jnp.zeros((LANES - rows, GLA_DK), F32)

    for h in range(GLA_HEADS):
        kcols = slice(h * GLA_DK, (h + 1) * GLA_DK)
        vcols = slice(h * GLA_DV, (h + 1) * GLA_DV)
        qh = main_ref[:, h * GLA_DK:(h + 1) * GLA_DK] * Q_SCALE
        kh = main_ref[:, K_OFF + h * GLA_DK:K_OFF + (h + 1) * GLA_DK]
        vh = main_ref[:, V_OFF + h * GLA_DV:V_OFF + (h + 1) * GLA_DV]
        gt = main_ref[:, G_OFF + h * GLA_DV:G_OFF + (h + 1) * GLA_DV]
        b = _cumsum_rows(tri, gk_ref[:, kcols])
        q3 = qh.reshape(ns, t_len, GLA_DK)
        k3 = kh.reshape(ns, t_len, GLA_DK)
        b3 = b.reshape(ns, t_len, GLA_DK)
        v3 = vh.reshape(ns, t_len, GLA_DV)
        o3 = jnp.zeros((ns, t_len, GLA_DV), F32)
        for jj in range(t_len):
            p = q3 * (k3[:, jj:jj + 1, :] * jnp.exp(jnp.minimum(b3 - b3[:, jj:jj + 1, :], 0.0)))
            col = jnp.sum(p, axis=-1, keepdims=True)
            col = jnp.where(row_in_seq >= jj, col, 0.0)
            o3 = o3 + col * v3[:, jj:jj + 1, :]
        o = o3.reshape(rows, GLA_DV)

        qe = qh * jnp.exp(b)
        k_dec = (k3 * jnp.exp(b3[:, t_len - 1:t_len, :] - b3)).reshape(rows, GLA_DK)
        b_t = jnp.concatenate([b, zero_pad], axis=0).T
        vb = vh.astype(BF16)
        for s in range(ns):
            mine = seq_of_row == s
            s0 = s0_ref[s, h]
            o = o + _dot(jnp.where(mine, qe, 0.0).astype(BF16), s0.astype(BF16))
            last = s * t_len + t_len - 1
            a_col = jnp.exp(b_t[:, last:last + 1])
            upd = _dot_tn(jnp.where(mine, k_dec, 0.0).astype(BF16), vb)
            sout_ref[s, h] = a_col * s0 + upd
        cat_ref[:, vcols] = _gated_head_out(o, gt, gn_ref[...]).astype(BF16)


def _mixer_sample(main, gk, s0, buf0, gla_norm, w_pool, pool_scale, cat, *, n_seq, t_len, pos0,
                  row0):
    ns = SEQ_PER_STEP
    rows = ns * t_len
    rb0 = row0 // rows
    return pl.pallas_call(
        functools.partial(_mixer_sample_kernel, pos0=pos0, t_len=t_len),
        grid=(n_seq // ns,),
        in_specs=[
            pl.BlockSpec((rows, MAIN_WIDTH), lambda i: (rb0 + i, 0)),
            pl.BlockSpec((rows, QK_WIDTH), lambda i: (rb0 + i, 0)),
            pl.BlockSpec((ns, GLA_HEADS, GLA_DK, GLA_DV), lambda i: (i, 0, 0, 0)),
            pl.BlockSpec((ns, BUF_ROWS, POOL_WIDTH), lambda i: (i, 0, 0)),
            pl.BlockSpec((1, GLA_DV), lambda i: (0, 0)),
            pl.BlockSpec((len(POOL_WINDOWS), POOL_GC, POOL_GC), lambda i: (0, 0, 0)),
            pl.BlockSpec((1, POOL_WIDTH), lambda i: (0, 0)),
            pl.BlockSpec(memory_space=pl.ANY),
        ],
        out_specs=[
            pl.BlockSpec((rows, D_MODEL), lambda i: (rb0 + i, 0)),
            pl.BlockSpec((ns, GLA_HEADS, GLA_DK, GLA_DV), lambda i: (i, 0, 0, 0)),
            pl.BlockSpec((ns, BUF_ROWS, POOL_WIDTH), lambda i: (i, 0, 0)),
        ],
        out_shape=[
            jax.ShapeDtypeStruct(cat.shape, BF16),
            jax.ShapeDtypeStruct((n_seq, GLA_HEADS, GLA_DK, GLA_DV), F32),
            jax.ShapeDtypeStruct((n_seq, BUF_ROWS, POOL_WIDTH), F32),
        ],
        scratch_shapes=[pltpu.VMEM((ns, BUF_ROWS + t_len, POOL_WIDTH), F32)],
        input_output_aliases={7: 0},
        compiler_params=pltpu.CompilerParams(
            dimension_semantics=("arbitrary",), vmem_limit_bytes=VMEM_LIMIT_BYTES),
        name="mixer_sample",
    )(main, gk, s0, buf0, gla_norm, w_pool, pool_scale, cat)


def _ffn_kernel(x_ref, cat_ref, wout_ref, nrm_ref, wg_ref, wu_ref, wd_ref, nf_ref, o_ref, h_ref,
                *, final_norm):
    j = pl.program_id(1)
    nj = pl.num_programs(1)

    @pl.when(j == 0)
    def _():
        x2 = x_ref[...] + _dot(cat_ref[...], wout_ref[...])
        o_ref[...] = x2
        h_ref[...] = _rmsnorm(x2, nrm_ref[...]).astype(BF16)

    h = h_ref[...]
    gate = _dot(h, wg_ref[...])
    up = _dot(h, wu_ref[...])
    act = (gate * jax.nn.sigmoid(gate)) * up
    o_ref[...] += _dot(act.astype(BF16), wd_ref[...])

    if final_norm:
        @pl.when(j == nj - 1)
        def _():
            o_ref[...] = _rmsnorm(o_ref[...], nf_ref[...])


def _ffn(x, cat, w_out, nrm, w_gate, w_up, w_down, norm_final, *, final_norm):
    m = x.shape[0]
    grid = (m // TM_FFN, D_FF // TF_FFN)
    return pl.pallas_call(
        functools.partial(_ffn_kernel, final_norm=final_norm),
        grid=grid,
        in_specs=[
            pl.BlockSpec((TM_FFN, D_MODEL), lambda i, j: (i, 0)),
            pl.BlockSpec((TM_FFN, D_MODEL), lambda i, j: (i, 0)),
            pl.BlockSpec((D_MODEL, D_MODEL), lambda i, j: (0, 0), pipeline_mode=pl.Buffered(1)),
            pl.BlockSpec((1, D_MODEL), lambda i, j: (0, 0)),
            pl.BlockSpec((D_MODEL, TF_FFN), lambda i, j: (0, j)),
            pl.BlockSpec((D_MODEL, TF_FFN), lambda i, j: (0, j)),
            pl.BlockSpec((TF_FFN, D_MODEL), lambda i, j: (j, 0)),
            pl.BlockSpec((1, D_MODEL), lambda i, j: (0, 0)),
        ],
        out_specs=pl.BlockSpec((TM_FFN, D_MODEL), lambda i, j: (i, 0)),
        out_shape=jax.ShapeDtypeStruct((m, D_MODEL), F32),
        scratch_shapes=[pltpu.VMEM((TM_FFN, D_MODEL), BF16)],
        compiler_params=pltpu.CompilerParams(
            dimension_semantics=("arbitrary", "arbitrary"), vmem_limit_bytes=VMEM_LIMIT_BYTES),
        name="ffn",
    )(x, cat, w_out, nrm, w_gate, w_up, w_down, norm_final)


def _pad_buf(buf):
    return jnp.pad(buf, ((0, 0), (1, 0), (0, 0)))


def kernel(x_prompt, x_sample, state_gla, state_pool, norm_mix, w_in, w_gk_up, b_gk, gla_norm,
           w_pool, pool_scale, w_out, norm_ffn, w_gate, w_up, w_down, norm_final):
    n_p, t_p, _ = x_prompt.shape
    n_s, t_s, _ = x_sample.shape
    m_p = n_p * t_p
    m_s = n_s * t_s
    m = m_p + m_s

    x = jnp.concatenate([x_prompt.reshape(m_p, D_MODEL), x_sample.reshape(m_s, D_MODEL)], axis=0)

    main_end = 2 * QK_WIDTH + 2 * GLA_WIDTH
    lr_end = main_end + GLA_LOWRANK
    w_main = jnp.concatenate([w_in[:, :, :main_end], w_in[:, :, lr_end:]], axis=2).astype(BF16)
    w_lr = jnp.pad(w_in[:, :, main_end:lr_end], ((0, 0), (0, 0), (0, LANES - GLA_LOWRANK))).astype(BF16)
    w_upp = jnp.pad(w_gk_up, ((0, 0), (0, LANES - GLA_LOWRANK), (0, 0))).astype(BF16)
    w_pool_b = w_pool.astype(BF16)
    w_out_b = w_out.astype(BF16)
    w_gate_b = w_gate.astype(BF16)
    w_up_b = w_up.astype(BF16)
    w_down_b = w_down.astype(BF16)

    s0_p = jnp.zeros((n_p, GLA_HEADS, GLA_DK, GLA_DV), F32)
    buf0_p = jnp.zeros((n_p, BUF_ROWS, POOL_WIDTH), F32)

    s_p, b_p, s_s, b_s = [], [], [], []
    for l in range(DEPTH):
        main, gk = _inproj(x, norm_mix[l][None], w_main[l], w_lr[l], w_upp[l], b_gk[l][None])
        cat, sp, bp = _mixer_prompt(main, gk, s0_p, buf0_p, gla_norm[l][None], w_pool_b[l],
                                    pool_scale[l][None], n_seq=n_p, seq_len=t_p, pos0=0, m_total=m)
        cat, ss, bs = _mixer_sample(main, gk, state_gla[l], _pad_buf(state_pool[l]),
                                    gla_norm[l][None], w_pool_b[l], pool_scale[l][None], cat,
                                    n_seq=n_s, t_len=t_s, pos0=PAST_LEN, row0=m_p)
        x = _ffn(x, cat, w_out_b[l], norm_ffn[l][None], w_gate_b[l], w_up_b[l], w_down_b[l],
                 norm_final[None], final_norm=(l == DEPTH - 1))
        s_p.append(sp)
        b_p.append(bp[:, 1:])
        s_s.append(ss)
        b_s.append(bs[:, 1:])

    y_prompt = x[:m_p].reshape(n_p, t_p, D_MODEL)
    y_sample = x[m_p:].reshape(n_s, t_s, D_MODEL)
    return (y_prompt, y_sample, jnp.stack(s_p), jnp.stack(b_p), jnp.stack(s_s), jnp.stack(b_s))
```

```python
import functools

import jax
import jax.numpy as jnp
from jax import lax
from jax.experimental import pallas as pl
from jax.experimental.pallas import tpu as pltpu

F32 = jnp.float32
BF16 = jnp.bfloat16

D_MODEL = 2048
DEPTH = 2
EPS = 1e-6
GLA_WIDTH = D_MODEL // 2
GLA_HEADS = 4
GLA_DV = GLA_WIDTH // GLA_HEADS
GLA_DK = GLA_DV // 2
GLA_LOWRANK = 16
GK_NORM = 16.0
GLA_CHUNK = 64
POOL_WIDTH = D_MODEL - GLA_WIDTH
POOL_WINDOWS = (2, 4, 8, 16)
POOL_GC = POOL_WIDTH // len(POOL_WINDOWS)
POOL_BUF = 15
D_FF = ((8 * D_MODEL // 3 + 255) // 256) * 256
QK_WIDTH = GLA_HEADS * GLA_DK
MAIN_WIDTH = 2 * QK_WIDTH + 2 * GLA_WIDTH + POOL_WIDTH
K_OFF = QK_WIDTH
V_OFF = 2 * QK_WIDTH
G_OFF = V_OFF + GLA_WIDTH
U_OFF = G_OFF + GLA_WIDTH
Q_SCALE = GLA_DK ** -0.5
PAST_LEN = 16384

LANES = 128
SUBLANES = 8
VMEM_LIMIT_BYTES = 56 * 1024 * 1024

TM_PROJ = 1024
TN_PROJ = 512
TM_FFN = 512
TF_FFN = 512
TT_PROMPT = 256
SUB = 16
ROW_CHUNK = 128
SEQ_PER_STEP = 8
BUF_ROWS = POOL_BUF + 1


def _rmsnorm(x, g):
    r = lax.rsqrt(jnp.mean(x * x, axis=-1, keepdims=True) + EPS)
    return (x * r) * g


def _dot(a, b):
    return jnp.dot(a, b, preferred_element_type=F32)


def _dot_nt(a, b):
    return lax.dot_general(a, b, (((1,), (1,)), ((), ())), preferred_element_type=F32)


def _dot_tn(a, b):
    return lax.dot_general(a, b, (((0,), (0,)), ((), ())), preferred_element_type=F32)


def _split3(x):
    hi = x.astype(BF16)
    r1 = x - hi.astype(F32)
    mid = r1.astype(BF16)
    lo = (r1 - mid.astype(F32)).astype(BF16)
    return hi, mid, lo


def _cumsum_rows(tri, g):
    hi, mid, lo = _split3(g)
    b3 = _dot(tri, jnp.concatenate([hi, mid, lo], axis=1))
    n = g.shape[1]
    return b3[:, :n] + b3[:, n:2 * n] + b3[:, 2 * n:]


def _layer_spec(block, index_map, **kw):
    return pl.BlockSpec((None,) + tuple(block), index_map, **kw)


def _split_rows_specs(tm, n_first_tiles, two_inputs):
    if not two_inputs:
        return [pl.BlockSpec((tm, D_MODEL), lambda i, j: (i, 0))]
    last = n_first_tiles - 1
    return [
        pl.BlockSpec((tm, D_MODEL), lambda i, j: (jnp.minimum(i, last), 0)),
        pl.BlockSpec((tm, D_MODEL), lambda i, j: (jnp.maximum(i - n_first_tiles, 0), 0),
                     pipeline_mode=pl.Buffered(1)),
    ]


def _for_row_source(x_refs, n_first_tiles, fn):
    if len(x_refs) == 1:
        fn(x_refs[0])
        return
    i = pl.program_id(0)
    pl.when(i < n_first_tiles)(lambda: fn(x_refs[0]))
    pl.when(i >= n_first_tiles)(lambda: fn(x_refs[1]))


def _inproj_kernel(*refs, n_x, n_first_tiles):
    x_refs = refs[:n_x]
    nrm_ref, w_ref, wlr_ref, wup_ref, bgk_ref, main_ref, gk_ref, h_ref = refs[n_x:]
    j = pl.program_id(1)

    @pl.when(j == 0)
    def _():
        def normalize(x_ref):
            def body(r, carry):
                rows = pl.ds(pl.multiple_of(r * ROW_CHUNK, ROW_CHUNK), ROW_CHUNK)
                h_ref[rows, :] = _rmsnorm(x_ref[rows, :], nrm_ref[...]).astype(BF16)
                return carry

            lax.fori_loop(0, TM_PROJ // ROW_CHUNK, body, 0)

        _for_row_source(x_refs, n_first_tiles, normalize)
        lr = _dot(h_ref[...], wlr_ref[...])
        z = _dot(lr.astype(BF16), wup_ref[...]) + bgk_ref[...]
        gk_ref[...] = jax.nn.log_sigmoid(z) / GK_NORM

    main_ref[...] = _dot(h_ref[...], w_ref[...])


def _inproj(xs, layer, nrm, w_main, w_lr, w_up, b_gk):
    m = sum(x.shape[0] for x in xs)
    n_first_tiles = xs[0].shape[0] // TM_PROJ
    grid = (m // TM_PROJ, MAIN_WIDTH // TN_PROJ)
    return pl.pallas_call(
        functools.partial(_inproj_kernel, n_x=len(xs), n_first_tiles=n_first_tiles),
        grid=grid,
        in_specs=_split_rows_specs(TM_PROJ, n_first_tiles, len(xs) == 2) + [
            _layer_spec((1, D_MODEL), lambda i, j: (layer, 0, 0)),
            _layer_spec((D_MODEL, TN_PROJ), lambda i, j: (layer, 0, j)),
            _layer_spec((D_MODEL, LANES), lambda i, j: (layer, 0, 0)),
            _layer_spec((LANES, QK_WIDTH), lambda i, j: (layer, 0, 0)),
            _layer_spec((1, QK_WIDTH), lambda i, j: (layer, 0, 0)),
        ],
        out_specs=[
            pl.BlockSpec((TM_PROJ, TN_PROJ), lambda i, j: (i, j)),
            pl.BlockSpec((TM_PROJ, QK_WIDTH), lambda i, j: (i, 0)),
        ],
        out_shape=[
            jax.ShapeDtypeStruct((m, MAIN_WIDTH), F32),
            jax.ShapeDtypeStruct((m, QK_WIDTH), F32),
        ],
        scratch_shapes=[pltpu.VMEM((TM_PROJ, D_MODEL), BF16)],
        compiler_params=pltpu.CompilerParams(
            dimension_semantics=("arbitrary", "arbitrary"), vmem_limit_bytes=VMEM_LIMIT_BYTES),
        name="inproj",
    )(*xs, nrm, w_main, w_lr, w_up, b_gk)


def _pool_group(ext_load, u_cols, pos, gi, wpool_ref, ps_ref):
    w = POOL_WINDOWS[gi]
    s = u_cols
    for sft in range(1, w):
        s = s + ext_load(sft)
    cnt = jnp.minimum(w, pos + 1).astype(F32)
    d = s / cnt - u_cols
    cols = slice(gi * POOL_GC, (gi + 1) * POOL_GC)
    return _dot(d.astype(BF16), wpool_ref[gi]) * ps_ref[:, cols]


def _gla_chunk(qh, kh, vh, gh, st, masks):
    tri, lane4, col_of_blk, below_blk, causal = masks
    c = GLA_CHUNK
    nb = c // SUB
    b = _cumsum_rows(tri, gh)

    off_rows = [jnp.zeros((SUB, c), F32)]
    for blk in range(1, nb):
        r0 = blk * SUB
        bs = b[r0:r0 + 1, :]
        qi = qh[r0:r0 + SUB, :] * jnp.exp(b[r0:r0 + SUB, :] - bs)
        kj = kh * jnp.exp(jnp.minimum(bs - b, 0.0))
        off_rows.append(_dot_nt(qi.astype(BF16), kj.astype(BF16)))
    s_off = jnp.concatenate(off_rows, axis=0)

    q4 = qh.reshape(nb, SUB, GLA_DK)
    k4 = kh.reshape(nb, SUB, GLA_DK)
    b4 = b.reshape(nb, SUB, GLA_DK)
    a4 = jnp.zeros((nb, SUB, c), F32)
    for jj in range(SUB):
        kj = k4[:, jj:jj + 1, :]
        bj = b4[:, jj:jj + 1, :]
        p = q4 * (kj * jnp.exp(jnp.minimum(b4 - bj, 0.0)))
        col = jnp.sum(p, axis=-1, keepdims=True)
        a4 = jnp.where(lane4 == col_of_blk + jj, col, a4)
    a = jnp.where(below_blk, s_off, a4.reshape(c, c))
    a = jnp.where(causal, a, 0.0)

    vb = vh.astype(BF16)
    o = _dot(a.astype(BF16), vb) + _dot_nt((qh * jnp.exp(b)).astype(BF16), st.astype(BF16))
    b_last = b[c - 1:c, :]
    k_dec = kh * jnp.exp(b_last - b)
    st_new = st * jnp.exp(b_last) + _dot_tn(vb, k_dec.astype(BF16))
    return o, st_new


def _gla_masks():
    c = GLA_CHUNK
    nb = c // SUB
    row = lax.broadcasted_iota(jnp.int32, (c, c), 0)
    col = lax.broadcasted_iota(jnp.int32, (c, c), 1)
    tri = (row >= col).astype(BF16)
    lane4 = lax.broadcasted_iota(jnp.int32, (nb, SUB, c), 2)
    col_of_blk = lax.broadcasted_iota(jnp.int32, (nb, SUB, c), 0) * SUB
    below_blk = col < (row - jnp.bitwise_and(row, SUB - 1))
    causal = col <= row
    return tri, lane4, col_of_blk, below_blk, causal


def _gated_head_out(o, gt, gn):
    return _rmsnorm(o, gn) * (gt * jax.nn.sigmoid(gt))


def _mixer_prompt_kernel(main_ref, gk_ref, s0_ref, buf0_ref, gn_ref, wpool_ref, ps_ref, *rest,
                         pos0, n_alias):
    cat_ref, sout_ref, bufout_ref, st_ref, ext_ref = rest[n_alias:]
    t = pl.program_id(1)
    nt = pl.num_programs(1)
    tt = TT_PROMPT

    @pl.when(t == 0)
    def _():
        for h in range(GLA_HEADS):
            st_ref[h] = s0_ref[0, h].T
        ext_ref[0:BUF_ROWS, :] = buf0_ref[0]

    ext_ref[BUF_ROWS:BUF_ROWS + tt, :] = main_ref[:, U_OFF:U_OFF + POOL_WIDTH]
    pos = pos0 + t * tt + lax.broadcasted_iota(jnp.int32, (tt, POOL_GC), 0)
    for gi in range(len(POOL_WINDOWS)):
        cols = slice(gi * POOL_GC, (gi + 1) * POOL_GC)
        ucols = slice(U_OFF + gi * POOL_GC, U_OFF + (gi + 1) * POOL_GC)
        y = _pool_group(lambda sft: ext_ref[BUF_ROWS - sft:BUF_ROWS - sft + tt, cols],
                        main_ref[:, ucols], pos, gi, wpool_ref, ps_ref)
        cat_ref[:, GLA_WIDTH + gi * POOL_GC:GLA_WIDTH + (gi + 1) * POOL_GC] = y.astype(BF16)
    ext_ref[0:BUF_ROWS, :] = ext_ref[tt:tt + BUF_ROWS, :]

    masks = _gla_masks()

    def chunk_body(ci, carry_):
        rows = pl.ds(pl.multiple_of(ci * GLA_CHUNK, GLA_CHUNK), GLA_CHUNK)
        for h in range(GLA_HEADS):
            kcols = slice(h * GLA_DK, (h + 1) * GLA_DK)
            vcols = slice(h * GLA_DV, (h + 1) * GLA_DV)
            o, st_new = _gla_chunk(
                main_ref[rows, h * GLA_DK:(h + 1) * GLA_DK] * Q_SCALE,
                main_ref[rows, K_OFF + h * GLA_DK:K_OFF + (h + 1) * GLA_DK],
                main_ref[rows, V_OFF + h * GLA_DV:V_OFF + (h + 1) * GLA_DV],
                gk_ref[rows, kcols], st_ref[h], masks)
            st_ref[h] = st_new
            gt = main_ref[rows, G_OFF + h * GLA_DV:G_OFF + (h + 1) * GLA_DV]
            cat_ref[rows, vcols] = _gated_head_out(o, gt, gn_ref[...]).astype(BF16)
        return carry_

    lax.fori_loop(0, tt // GLA_CHUNK, chunk_body, 0)

    @pl.when(t == nt - 1)
    def _():
        for h in range(GLA_HEADS):
            sout_ref[0, h] = st_ref[h].T
        bufout_ref[0] = ext_ref[1:BUF_ROWS, :]


def _mixer_prompt(main, gk, s0, buf0, layer, gla_norm, w_pool, pool_scale, stacked, *,
                  n_seq, seq_len, pos0):
    tt = TT_PROMPT
    nt = seq_len // tt
    row = lambda b, t: b * nt + t
    n_alias = len(stacked)
    n_in = 7
    return pl.pallas_call(
        functools.partial(_mixer_prompt_kernel, pos0=pos0, n_alias=n_alias),
        grid=(n_seq, nt),
        in_specs=[
            pl.BlockSpec((tt, MAIN_WIDTH), lambda b, t: (row(b, t), 0)),
            pl.BlockSpec((tt, QK_WIDTH), lambda b, t: (row(b, t), 0)),
            pl.BlockSpec((1, GLA_HEADS, GLA_DK, GLA_DV), lambda b, t: (b, 0, 0, 0)),
            pl.BlockSpec((1, BUF_ROWS, POOL_WIDTH), lambda b, t: (b, 0, 0)),
            _layer_spec((1, GLA_DV), lambda b, t: (layer, 0, 0)),
            _layer_spec((len(POOL_WINDOWS), POOL_GC, POOL_GC), lambda b, t: (layer, 0, 0, 0)),
            _layer_spec((1, POOL_WIDTH), lambda b, t: (layer, 0, 0)),
        ] + [pl.BlockSpec(memory_space=pl.ANY)] * n_alias,
        out_specs=[
            pl.BlockSpec((tt, D_MODEL), lambda b, t: (row(b, t), 0)),
            _layer_spec((1, GLA_HEADS, GLA_DK, GLA_DV), lambda b, t: (layer, b, 0, 0, 0)),
            _layer_spec((1, POOL_BUF, POOL_WIDTH), lambda b, t: (layer, b, 0, 0)),
        ],
        out_shape=[
            jax.ShapeDtypeStruct((main.shape[0], D_MODEL), BF16),
            jax.ShapeDtypeStruct((DEPTH, n_seq, GLA_HEADS, GLA_DK, GLA_DV), F32),
            jax.ShapeDtypeStruct((DEPTH, n_seq, POOL_BUF, POOL_WIDTH), F32),
        ],
        scratch_shapes=[
            pltpu.VMEM((GLA_HEADS, GLA_DV, GLA_DK), F32),
            pltpu.VMEM((BUF_ROWS + tt, POOL_WIDTH), F32),
        ],
        input_output_aliases={n_in + k: 1 + k for k in range(n_alias)},
        compiler_params=pltpu.CompilerParams(
            dimension_semantics=("arbitrary", "arbitrary"), vmem_limit_bytes=VMEM_LIMIT_BYTES),
        name="mixer_prompt",
    )(main, gk, s0, buf0, gla_norm, w_pool, pool_scale, *stacked)


def _mixer_sample_kernel(main_ref, gk_ref, s0_ref, buf0_ref, gn_ref, wpool_ref, ps_ref, *rest,
                         pos0, t_len, n_alias):
    cat_ref, sout_ref, bufout_ref, ext_ref = rest[n_alias:]
    ns = SEQ_PER_STEP
    rows = ns * t_len

    ext_ref[:, 1:BUF_ROWS, :] = buf0_ref[...]
    ext_ref[:, BUF_ROWS:BUF_ROWS + t_len, :] = (
        main_ref[:, U_OFF:U_OFF + POOL_WIDTH].reshape(ns, t_len, POOL_WIDTH))
    pos = pos0 + lax.broadcasted_iota(jnp.int32, (ns, t_len, POOL_GC), 1)
    for gi in range(len(POOL_WINDOWS)):
        cols = slice(gi * POOL_GC, (gi + 1) * POOL_GC)
        u_cols = main_ref[:, U_OFF + gi * POOL_GC:U_OFF + (gi + 1) * POOL_GC].reshape(
            ns, t_len, POOL_GC)
        w = POOL_WINDOWS[gi]
        s = u_cols
        for sft in range(1, w):
            s = s + ext_ref[:, BUF_ROWS - sft:BUF_ROWS - sft + t_len, cols]
        cnt = jnp.minimum(w, pos + 1).astype(F32)
        d = (s / cnt - u_cols).reshape(rows, POOL_GC)
        y = _dot(d.astype(BF16), wpool_ref[gi]) * ps_ref[:, cols]
        cat_ref[:, GLA_WIDTH + gi * POOL_GC:GLA_WIDTH + (gi + 1) * POOL_GC] = y.astype(BF16)
    bufout_ref[...] = ext_ref[:, t_len + 1:t_len + BUF_ROWS, :]

    r_i = lax.broadcasted_iota(jnp.int32, (rows, rows), 0)
    c_i = lax.broadcasted_iota(jnp.int32, (rows, rows), 1)
    same_seq = (r_i - jnp.bitwise_and(r_i, t_len - 1)) == (c_i - jnp.bitwise_and(c_i, t_len - 1))
    tri = jnp.logical_and(r_i >= c_i, same_seq).astype(BF16)
    row_in_seq = lax.broadcasted_iota(jnp.int32, (ns, t_len, 1), 1)
    seq_of_row = lax.broadcasted_iota(jnp.int32, (rows, 1), 0) // t_len
    zero_pad = jnp.zeros((LANES - rows, GLA_DK), F32)

    for h in range(GLA_HEADS):
        kcols = slice(h * GLA_DK, (h + 1) * GLA_DK)
        vcols = slice(h * GLA_DV, (h + 1) * GLA_DV)
        qh = main_ref[:, h * GLA_DK:(h + 1) * GLA_DK] * Q_SCALE
        kh = main_ref[:, K_OFF + h * GLA_DK:K_OFF + (h + 1) * GLA_DK]
        vh = main_ref[:, V_OFF + h * GLA_DV:V_OFF + (h + 1) * GLA_DV]
        gt = main_ref[:, G_OFF + h * GLA_DV:G_OFF + (h + 1) * GLA_DV]
        b = _cumsum_rows(tri, gk_ref[:, kcols])
        q3 = qh.reshape(ns, t_len, GLA_DK)
        k3 = kh.reshape(ns, t_len, GLA_DK)
        b3 = b.reshape(ns, t_len, GLA_DK)
        v3 = vh.reshape(ns, t_len, GLA_DV)
        o3 = jnp.zeros((ns, t_len, GLA_DV), F32)
        for jj in range(t_len):
            p = q3 * (k3[:, jj:jj + 1, :] * jnp.exp(jnp.minimum(b3 - b3[:, jj:jj + 1, :], 0.0)))
            col = jnp.sum(p, axis=-1, keepdims=True)
            col = jnp.where(row_in_seq >= jj, col, 0.0)
            o3 = o3 + col * v3[:, jj:jj + 1, :]
        o = o3.reshape(rows, GLA_DV)

        qe = qh * jnp.exp(b)
        k_dec = (k3 * jnp.exp(b3[:, t_len - 1:t_len, :] - b3)).reshape(rows, GLA_DK)
        b_t = jnp.concatenate([b, zero_pad], axis=0).T
        vb = vh.astype(BF16)
        for s in range(ns):
            mine = seq_of_row == s
            s0 = s0_ref[s, h]
            o = o + _dot(jnp.where(mine, qe, 0.0).astype(BF16), s0.astype(BF16))
            last = s * t_len + t_len - 1
            a_col = jnp.exp(b_t[:, last:last + 1])
            upd = _dot_tn(jnp.where(mine, k_dec, 0.0).astype(BF16), vb)
            sout_ref[s, h] = a_col * s0 + upd
        cat_ref[:, vcols] = _gated_head_out(o, gt, gn_ref[...]).astype(BF16)


def _mixer_sample(main, gk, s_in, buf_in, layer, gla_norm, w_pool, pool_scale, cat, stacked, *,
                  n_seq, t_len, pos0, row0):
    ns = SEQ_PER_STEP
    rows = ns * t_len
    rb0 = row0 // rows
    n_alias = 1 + len(stacked)
    n_in = 7
    return pl.pallas_call(
        functools.partial(_mixer_sample_kernel, pos0=pos0, t_len=t_len, n_alias=n_alias),
        grid=(n_seq // ns,),
        in_specs=[
            pl.BlockSpec((rows, MAIN_WIDTH), lambda i: (rb0 + i, 0)),
            pl.BlockSpec((rows, QK_WIDTH), lambda i: (rb0 + i, 0)),
            _layer_spec((ns, GLA_HEADS, GLA_DK, GLA_DV), lambda i: (layer, i, 0, 0, 0)),
            _layer_spec((ns, POOL_BUF, POOL_WIDTH), lambda i: (layer, i, 0, 0)),
            _layer_spec((1, GLA_DV), lambda i: (layer, 0, 0)),
            _layer_spec((len(POOL_WINDOWS), POOL_GC, POOL_GC), lambda i: (layer, 0, 0, 0)),
            _layer_spec((1, POOL_WIDTH), lambda i: (layer, 0, 0)),
        ] + [pl.BlockSpec(memory_space=pl.ANY)] * n_alias,
        out_specs=[
            pl.BlockSpec((rows, D_MODEL), lambda i: (rb0 + i, 0)),
            _layer_spec((ns, GLA_HEADS, GLA_DK, GLA_DV), lambda i: (layer, i, 0, 0, 0)),
            _layer_spec((ns, POOL_BUF, POOL_WIDTH), lambda i: (layer, i, 0, 0)),
        ],
        out_shape=[
            jax.ShapeDtypeStruct(cat.shape, BF16),
            jax.ShapeDtypeStruct((DEPTH, n_seq, GLA_HEADS, GLA_DK, GLA_DV), F32),
            jax.ShapeDtypeStruct((DEPTH, n_seq, POOL_BUF, POOL_WIDTH), F32),
        ],
        scratch_shapes=[pltpu.VMEM((ns, BUF_ROWS + t_len, POOL_WIDTH), F32)],
        input_output_aliases={n_in + k: k for k in range(n_alias)},
        compiler_params=pltpu.CompilerParams(
            dimension_semantics=("arbitrary",), vmem_limit_bytes=VMEM_LIMIT_BYTES),
        name="mixer_sample",
    )(main, gk, s_in, buf_in, gla_norm, w_pool, pool_scale, cat, *stacked)


def _ffn_kernel(*refs, n_x, n_first_tiles, final_norm):
    x_refs = refs[:n_x]
    cat_ref, wout_ref, nrm_ref, wg_ref, wu_ref, wd_ref, nf_ref, o_ref, h_ref = refs[n_x:]
    j = pl.program_id(1)
    nj = pl.num_programs(1)

    @pl.when(j == 0)
    def _():
        def start(x_ref):
            x2 = x_ref[...] + _dot(cat_ref[...], wout_ref[...])
            o_ref[...] = x2
            h_ref[...] = _rmsnorm(x2, nrm_ref[...]).astype(BF16)

        _for_row_source(x_refs, n_first_tiles, start)

    h = h_ref[...]
    gate = _dot(h, wg_ref[...])
    up = _dot(h, wu_ref[...])
    act = (gate * jax.nn.sigmoid(gate)) * up
    o_ref[...] += _dot(act.astype(BF16), wd_ref[...])

    if final_norm:
        @pl.when(j == nj - 1)
        def _():
            o_ref[...] = _rmsnorm(o_ref[...], nf_ref[...])


def _ffn(xs, cat, layer, w_out, nrm, w_gate, w_up, w_down, norm_final, *, final_norm):
    m = sum(x.shape[0] for x in xs)
    n_first_tiles = xs[0].shape[0] // TM_FFN
    grid = (m // TM_FFN, D_FF // TF_FFN)
    return pl.pallas_call(
        functools.partial(_ffn_kernel, n_x=len(xs), n_first_tiles=n_first_tiles,
                          final_norm=final_norm),
        grid=grid,
        in_specs=_split_rows_specs(TM_FFN, n_first_tiles, len(xs) == 2) + [
            pl.BlockSpec((TM_FFN, D_MODEL), lambda i, j: (i, 0)),
            _layer_spec((D_MODEL, D_MODEL), lambda i, j: (layer, 0, 0),
                        pipeline_mode=pl.Buffered(1)),
            _layer_spec((1, D_MODEL), lambda i, j: (layer, 0, 0)),
            _layer_spec((D_MODEL, TF_FFN), lambda i, j: (layer, 0, j)),
            _layer_spec((D_MODEL, TF_FFN), lambda i, j: (layer, 0, j)),
            _layer_spec((TF_FFN, D_MODEL), lambda i, j: (layer, j, 0)),
            pl.BlockSpec((1, D_MODEL), lambda i, j: (0, 0)),
        ],
        out_specs=pl.BlockSpec((TM_FFN, D_MODEL), lambda i, j: (i, 0)),
        out_shape=jax.ShapeDtypeStruct((m, D_MODEL), F32),
        scratch_shapes=[pltpu.VMEM((TM_FFN, D_MODEL), BF16)],
        compiler_params=pltpu.CompilerParams(
            dimension_semantics=("arbitrary", "arbitrary"), vmem_limit_bytes=VMEM_LIMIT_BYTES),
        name="ffn",
    )(*xs, cat, w_out, nrm, w_gate, w_up, w_down, norm_final)


def kernel(x_prompt, x_sample, state_gla, state_pool, norm_mix, w_in, w_gk_up, b_gk, gla_norm,
           w_pool, pool_scale, w_out, norm_ffn, w_gate, w_up, w_down, norm_final):
    n_p, t_p, _ = x_prompt.shape
    n_s, t_s, _ = x_sample.shape
    m_p = n_p * t_p
    m_s = n_s * t_s

    main_end = 2 * QK_WIDTH + 2 * GLA_WIDTH
    lr_end = main_end + GLA_LOWRANK
    w_main = jnp.concatenate([w_in[:, :, :main_end], w_in[:, :, lr_end:]], axis=2).astype(BF16)
    w_lr = jnp.pad(w_in[:, :, main_end:lr_end], ((0, 0), (0, 0), (0, LANES - GLA_LOWRANK))).astype(BF16)
    w_upp = jnp.pad(w_gk_up, ((0, 0), (0, LANES - GLA_LOWRANK), (0, 0))).astype(BF16)
    w_pool_b = w_pool.astype(BF16)
    w_out_b = w_out.astype(BF16)
    w_gate_b = w_gate.astype(BF16)
    w_up_b = w_up.astype(BF16)
    w_down_b = w_down.astype(BF16)
    row3 = lambda a: a[:, None, :]
    norm_mix3, norm_ffn3, b_gk3 = row3(norm_mix), row3(norm_ffn), row3(b_gk)
    gla_norm3, pool_scale3 = row3(gla_norm), row3(pool_scale)

    s0_p = jnp.zeros((n_p, GLA_HEADS, GLA_DK, GLA_DV), F32)
    buf0_p = jnp.zeros((n_p, BUF_ROWS, POOL_WIDTH), F32)

    xs = [x_prompt.reshape(m_p, D_MODEL), x_sample.reshape(m_s, D_MODEL)]
    stacked_p, stacked_s = (), ()
    for l in range(DEPTH):
        main, gk = _inproj(xs, l, norm_mix3, w_main, w_lr, w_upp, b_gk3)
        cat, *stacked_p = _mixer_prompt(main, gk, s0_p, buf0_p, l, gla_norm3, w_pool_b, pool_scale3,
                                        stacked_p, n_seq=n_p, seq_len=t_p, pos0=0)
        cat, *stacked_s = _mixer_sample(main, gk, state_gla, state_pool, l, gla_norm3, w_pool_b,
                                        pool_scale3, cat, stacked_s, n_seq=n_s, t_len=t_s,
                                        pos0=PAST_LEN, row0=m_p)
        x = _ffn(xs, cat, l, w_out_b, norm_ffn3, w_gate_b, w_up_b, w_down_b, norm_final[None],
                 final_norm=(l == DEPTH - 1))
        xs = [x]

    y_prompt = x[:m_p].reshape(n_p, t_p, D_MODEL)
    y_sample = x[m_p:].reshape(n_s, t_s, D_MODEL)
    return (y_prompt, y_sample, stacked_p[0], stacked_p[1], stacked_s[0], stacked_s[1])
```

```python
import functools

import jax
import jax.numpy as jnp
from jax import lax
from jax.experimental import pallas as pl
from jax.experimental.pallas import tpu as pltpu

F32 = jnp.float32
BF16 = jnp.bfloat16

D_MODEL = 2048
DEPTH = 2
EPS = 1e-6
GLA_WIDTH = D_MODEL // 2
GLA_HEADS = 4
GLA_DV = GLA_WIDTH // GLA_HEADS
GLA_DK = GLA_DV // 2
GLA_LOWRANK = 16
GK_NORM = 16.0
GLA_CHUNK = 64
POOL_WIDTH = D_MODEL - GLA_WIDTH
POOL_WINDOWS = (2, 4, 8, 16)
POOL_GC = POOL_WIDTH // len(POOL_WINDOWS)
POOL_BUF = 15
D_FF = ((8 * D_MODEL // 3 + 255) // 256) * 256
QK_WIDTH = GLA_HEADS * GLA_DK
MAIN_WIDTH = 2 * QK_WIDTH + 2 * GLA_WIDTH + POOL_WIDTH
K_OFF = QK_WIDTH
V_OFF = 2 * QK_WIDTH
G_OFF = V_OFF + GLA_WIDTH
U_OFF = G_OFF + GLA_WIDTH
Q_SCALE = GLA_DK ** -0.5
PAST_LEN = 16384

LANES = 128
SUBLANES = 8
VMEM_LIMIT_BYTES = 58 * 1024 * 1024

TM_PROJ = 1024
TN_PROJ = 512
N_ALIGNED_TILES = (2 * QK_WIDTH + 2 * GLA_WIDTH) // TN_PROJ
PROJ_K_CHUNK = 512
TM_FFN = 512
TF_FFN = 512
TT_PROMPT = 256
SUB = 16
ROW_CHUNK = 128
SEQ_PER_STEP = 8
BUF_ROWS = POOL_BUF + 1
N_SIDE_WEIGHTS = 4


def _rmsnorm(x, g):
    r = lax.rsqrt(jnp.mean(x * x, axis=-1, keepdims=True) + EPS)
    return (x * r) * g


def _dot(a, b):
    return jnp.dot(a, b, preferred_element_type=F32)


def _dot_nt(a, b):
    return lax.dot_general(a, b, (((1,), (1,)), ((), ())), preferred_element_type=F32)


def _dot_tn(a, b):
    return lax.dot_general(a, b, (((0,), (0,)), ((), ())), preferred_element_type=F32)


def _split3(x):
    hi = x.astype(BF16)
    r1 = x - hi.astype(F32)
    mid = r1.astype(BF16)
    lo = (r1 - mid.astype(F32)).astype(BF16)
    return hi, mid, lo


def _cumsum_rows(tri, g):
    hi, mid, lo = _split3(g)
    b3 = _dot(tri, jnp.concatenate([hi, mid, lo], axis=1))
    n = g.shape[1]
    return b3[:, :n] + b3[:, n:2 * n] + b3[:, 2 * n:]


def _layer_spec(block, index_map, **kw):
    return pl.BlockSpec((None,) + tuple(block), index_map, **kw)


def _split_rows_specs(tm, n_first_tiles, two_inputs):
    if not two_inputs:
        return [pl.BlockSpec((tm, D_MODEL), lambda i, j: (i, 0))]
    last = n_first_tiles - 1
    return [
        pl.BlockSpec((tm, D_MODEL), lambda i, j: (jnp.minimum(i, last), 0)),
        pl.BlockSpec((tm, D_MODEL), lambda i, j: (jnp.maximum(i - n_first_tiles, 0), 0),
                     pipeline_mode=pl.Buffered(1)),
    ]


def _for_row_source(x_refs, n_first_tiles, fn):
    if len(x_refs) == 1:
        fn(x_refs[0])
        return
    i = pl.program_id(0)
    pl.when(i < n_first_tiles)(lambda: fn(x_refs[0]))
    pl.when(i >= n_first_tiles)(lambda: fn(x_refs[1]))


def _inproj_kernel(*refs, n_x, n_first_tiles):
    x_refs = refs[:n_x]
    nrm_ref, w_ref, wlr_ref, wup_ref, bgk_ref, main_ref, gk_ref, h_ref = refs[n_x:]
    j = pl.program_id(1)

    @pl.when(j == 0)
    def _():
        def normalize(x_ref):
            def body(r, carry):
                rows = pl.ds(pl.multiple_of(r * ROW_CHUNK, ROW_CHUNK), ROW_CHUNK)
                h_ref[rows, :] = _rmsnorm(x_ref[rows, :], nrm_ref[...]).astype(BF16)
                return carry

            lax.fori_loop(0, TM_PROJ // ROW_CHUNK, body, 0)

        _for_row_source(x_refs, n_first_tiles, normalize)

        lr = _dot_nt(h_ref[...], wlr_ref[...])
        z = _dot(lr.astype(BF16), wup_ref[...]) + bgk_ref[...]
        gk_ref[...] = jax.nn.log_sigmoid(z) / GK_NORM

    kc = PROJ_K_CHUNK
    t = _dot_nt(h_ref[:, 0:kc], w_ref[0, :, 0:kc])
    for k0 in range(kc, D_MODEL, kc):
        t = t + _dot_nt(h_ref[:, k0:k0 + kc], w_ref[0, :, k0:k0 + kc])
    main_ref[...] = t


def _inproj(xs, layer, nrm, w_in_t, w_lr_t, w_up, b_gk):
    m = sum(x.shape[0] for x in xs)
    n_first_tiles = xs[0].shape[0] // TM_PROJ
    grid = (m // TM_PROJ, MAIN_WIDTH // TN_PROJ)

    def w_rows(j):
        start = jnp.where(j < N_ALIGNED_TILES, j * TN_PROJ, j * TN_PROJ + GLA_LOWRANK)
        return pl.multiple_of(start, GLA_LOWRANK)

    return pl.pallas_call(
        functools.partial(_inproj_kernel, n_x=len(xs), n_first_tiles=n_first_tiles),
        grid=grid,
        in_specs=_split_rows_specs(TM_PROJ, n_first_tiles, len(xs) == 2) + [
            _layer_spec((1, D_MODEL), lambda i, j: (layer, 0, 0)),
            pl.BlockSpec((pl.Element(1), pl.Element(TN_PROJ), pl.Element(D_MODEL)),
                         lambda i, j: (layer, w_rows(j), 0)),
            _layer_spec((LANES, D_MODEL), lambda i, j: (layer, 0, 0)),
            _layer_spec((LANES, QK_WIDTH), lambda i, j: (layer, 0, 0)),
            _layer_spec((1, QK_WIDTH), lambda i, j: (layer, 0, 0)),
        ],
        out_specs=[
            pl.BlockSpec((TM_PROJ, TN_PROJ), lambda i, j: (i, j)),
            pl.BlockSpec((TM_PROJ, QK_WIDTH), lambda i, j: (i, 0)),
        ],
        out_shape=[
            jax.ShapeDtypeStruct((m, MAIN_WIDTH), F32),
            jax.ShapeDtypeStruct((m, QK_WIDTH), F32),
        ],
        scratch_shapes=[pltpu.VMEM((TM_PROJ, D_MODEL), BF16)],
        compiler_params=pltpu.CompilerParams(
            dimension_semantics=("arbitrary", "arbitrary"), vmem_limit_bytes=VMEM_LIMIT_BYTES),
        name="inproj",
    )(*xs, nrm, w_in_t, w_lr_t, w_up, b_gk)


def _pool_group(ext_load, u_cols, pos, gi, wpool_ref, ps_ref):
    w = POOL_WINDOWS[gi]
    s = u_cols
    for sft in range(1, w):
        s = s + ext_load(sft)
    cnt = jnp.minimum(w, pos + 1).astype(F32)
    d = s / cnt - u_cols
    cols = slice(gi * POOL_GC, (gi + 1) * POOL_GC)
    return _dot(d.astype(BF16), wpool_ref[gi]) * ps_ref[:, cols]


def _gla_chunk(qh, kh, vh, gh, st, masks):
    tri, lane4, col_of_blk, below_blk, causal = masks
    c = GLA_CHUNK
    nb = c // SUB
    b = _cumsum_rows(tri, gh)

    off_rows = [jnp.zeros((SUB, c), F32)]
    for blk in range(1, nb):
        r0 = blk * SUB
        bs = b[r0:r0 + 1, :]
        qi = qh[r0:r0 + SUB, :] * jnp.exp(b[r0:r0 + SUB, :] - bs)
        kj = kh * jnp.exp(jnp.minimum(bs - b, 0.0))
        off_rows.append(_dot_nt(qi.astype(BF16), kj.astype(BF16)))
    s_off = jnp.concatenate(off_rows, axis=0)

    q4 = qh.reshape(nb, SUB, GLA_DK)
    k4 = kh.reshape(nb, SUB, GLA_DK)
    b4 = b.reshape(nb, SUB, GLA_DK)
    a4 = jnp.zeros((nb, SUB, c), F32)
    for jj in range(SUB):
        kj = k4[:, jj:jj + 1, :]
        bj = b4[:, jj:jj + 1, :]
        p = q4 * (kj * jnp.exp(jnp.minimum(b4 - bj, 0.0)))
        col = jnp.sum(p, axis=-1, keepdims=True)
        a4 = jnp.where(lane4 == col_of_blk + jj, col, a4)
    a = jnp.where(below_blk, s_off, a4.reshape(c, c))
    a = jnp.where(causal, a, 0.0)

    vb = vh.astype(BF16)
    o = _dot(a.astype(BF16), vb) + _dot_nt((qh * jnp.exp(b)).astype(BF16), st.astype(BF16))
    b_last = b[c - 1:c, :]
    k_dec = kh * jnp.exp(b_last - b)
    st_new = st * jnp.exp(b_last) + _dot_tn(vb, k_dec.astype(BF16))
    return o, st_new


def _gla_masks():
    c = GLA_CHUNK
    nb = c // SUB
    row = lax.broadcasted_iota(jnp.int32, (c, c), 0)
    col = lax.broadcasted_iota(jnp.int32, (c, c), 1)
    tri = (row >= col).astype(BF16)
    lane4 = lax.broadcasted_iota(jnp.int32, (nb, SUB, c), 2)
    col_of_blk = lax.broadcasted_iota(jnp.int32, (nb, SUB, c), 0) * SUB
    below_blk = col < (row - jnp.bitwise_and(row, SUB - 1))
    causal = col <= row
    return tri, lane4, col_of_blk, below_blk, causal


def _gated_head_out(o, gt, gn):
    return _rmsnorm(o, gn) * (gt * jax.nn.sigmoid(gt))


def _mixer_prompt_kernel(main_ref, gk_ref, s0_ref, buf0_ref, gn_ref, wpool_ref, ps_ref, *rest,
                         pos0, n_alias):
    n_w = N_SIDE_WEIGHTS
    w32_refs = rest[:n_w]
    rest = rest[n_w + n_alias:]
    cat_ref, sout_ref, bufout_ref = rest[:3]
    w16_refs = rest[3:3 + n_w]
    st_ref, ext_ref = rest[3 + n_w:]
    t = pl.program_id(1)
    nt = pl.num_programs(1)
    tt = TT_PROMPT

    for w32_ref, w16_ref in zip(w32_refs, w16_refs):
        w16_ref[...] = w32_ref[...].astype(BF16)

    @pl.when(t == 0)
    def _():
        for h in range(GLA_HEADS):
            st_ref[h] = s0_ref[0, h].T
        ext_ref[0:BUF_ROWS, :] = buf0_ref[0]

    ext_ref[BUF_ROWS:BUF_ROWS + tt, :] = main_ref[:, U_OFF:U_OFF + POOL_WIDTH]
    pos = pos0 + t * tt + lax.broadcasted_iota(jnp.int32, (tt, POOL_GC), 0)
    for gi in range(len(POOL_WINDOWS)):
        cols = slice(gi * POOL_GC, (gi + 1) * POOL_GC)
        ucols = slice(U_OFF + gi * POOL_GC, U_OFF + (gi + 1) * POOL_GC)
        y = _pool_group(lambda sft: ext_ref[BUF_ROWS - sft:BUF_ROWS - sft + tt, cols],
                        main_ref[:, ucols], pos, gi, wpool_ref, ps_ref)
        cat_ref[:, GLA_WIDTH + gi * POOL_GC:GLA_WIDTH + (gi + 1) * POOL_GC] = y.astype(BF16)
    ext_ref[0:BUF_ROWS, :] = ext_ref[tt:tt + BUF_ROWS, :]

    masks = _gla_masks()

    def chunk_body(ci, carry_):
        rows = pl.ds(pl.multiple_of(ci * GLA_CHUNK, GLA_CHUNK), GLA_CHUNK)
        for h in range(GLA_HEADS):
            kcols = slice(h * GLA_DK, (h + 1) * GLA_DK)
            vcols = slice(h * GLA_DV, (h + 1) * GLA_DV)
            o, st_new = _gla_chunk(
                main_ref[rows, h * GLA_DK:(h + 1) * GLA_DK] * Q_SCALE,
                main_ref[rows, K_OFF + h * GLA_DK:K_OFF + (h + 1) * GLA_DK],
                main_ref[rows, V_OFF + h * GLA_DV:V_OFF + (h + 1) * GLA_DV],
                gk_ref[rows, kcols], st_ref[h], masks)
            st_ref[h] = st_new
            gt = main_ref[rows, G_OFF + h * GLA_DV:G_OFF + (h + 1) * GLA_DV]
            cat_ref[rows, vcols] = _gated_head_out(o, gt, gn_ref[...]).astype(BF16)
        return carry_

    lax.fori_loop(0, tt // GLA_CHUNK, chunk_body, 0)

    @pl.when(t == nt - 1)
    def _():
        for h in range(GLA_HEADS):
            sout_ref[0, h] = st_ref[h].T
        bufout_ref[0] = ext_ref[1:BUF_ROWS, :]


def _mixer_prompt(main, gk, s0, buf0, layer, gla_norm, w_pool, pool_scale, ffn_weights, stacked, *,
                  n_seq, seq_len, pos0):
    tt = TT_PROMPT
    nt = seq_len // tt
    n_steps = n_seq * nt
    row = lambda b, t: b * nt + t
    n_alias = len(stacked)
    n_in = 7 + len(ffn_weights)
    w_blocks = [(w.shape[1] // n_steps, w.shape[2]) for w in ffn_weights]
    return pl.pallas_call(
        functools.partial(_mixer_prompt_kernel, pos0=pos0, n_alias=n_alias),
        grid=(n_seq, nt),
        in_specs=[
            pl.BlockSpec((tt, MAIN_WIDTH), lambda b, t: (row(b, t), 0)),
            pl.BlockSpec((tt, QK_WIDTH), lambda b, t: (row(b, t), 0)),
            pl.BlockSpec((1, GLA_HEADS, GLA_DK, GLA_DV), lambda b, t: (b, 0, 0, 0)),
            pl.BlockSpec((1, BUF_ROWS, POOL_WIDTH), lambda b, t: (b, 0, 0)),
            _layer_spec((1, GLA_DV), lambda b, t: (layer, 0, 0)),
            _layer_spec((len(POOL_WINDOWS), POOL_GC, POOL_GC), lambda b, t: (layer, 0, 0, 0)),
            _layer_spec((1, POOL_WIDTH), lambda b, t: (layer, 0, 0)),
        ] + [_layer_spec(blk, lambda b, t: (layer, row(b, t), 0)) for blk in w_blocks]
        + [pl.BlockSpec(memory_space=pl.ANY)] * n_alias,
        out_specs=[
            pl.BlockSpec((tt, D_MODEL), lambda b, t: (row(b, t), 0)),
            _layer_spec((1, GLA_HEADS, GLA_DK, GLA_DV), lambda b, t: (layer, b, 0, 0, 0)),
            _layer_spec((1, POOL_BUF, POOL_WIDTH), lambda b, t: (layer, b, 0, 0)),
        ] + [pl.BlockSpec(blk, lambda b, t: (row(b, t), 0)) for blk in w_blocks],
        out_shape=[
            jax.ShapeDtypeStruct((main.shape[0], D_MODEL), BF16),
            jax.ShapeDtypeStruct((DEPTH, n_seq, GLA_HEADS, GLA_DK, GLA_DV), F32),
            jax.ShapeDtypeStruct((DEPTH, n_seq, POOL_BUF, POOL_WIDTH), F32),
        ] + [jax.ShapeDtypeStruct(w.shape[1:], BF16) for w in ffn_weights],
        scratch_shapes=[
            pltpu.VMEM((GLA_HEADS, GLA_DV, GLA_DK), F32),
            pltpu.VMEM((BUF_ROWS + tt, POOL_WIDTH), F32),
        ],
        input_output_aliases={n_in + k: 1 + k for k in range(n_alias)},
        compiler_params=pltpu.CompilerParams(
            dimension_semantics=("arbitrary", "arbitrary"), vmem_limit_bytes=VMEM_LIMIT_BYTES),
        name="mixer_prompt",
    )(main, gk, s0, buf0, gla_norm, w_pool, pool_scale, *ffn_weights, *stacked)


def _mixer_sample_kernel(main_ref, gk_ref, s0_ref, buf0_ref, gn_ref, wpool_ref, ps_ref, *rest,
                         pos0, t_len, n_alias):
    cat_ref, sout_ref, bufout_ref, ext_ref = rest[n_alias:]
    ns = SEQ_PER_STEP
    rows = ns * t_len

    ext_ref[:, 1:BUF_ROWS, :] = buf0_ref[...]
    ext_ref[:, BUF_ROWS:BUF_ROWS + t_len, :] = (
        main_ref[:, U_OFF:U_OFF + POOL_WIDTH].reshape(ns, t_len, POOL_WIDTH))
    pos = pos0 + lax.broadcasted_iota(jnp.int32, (ns, t_len, POOL_GC), 1)
    for gi in range(len(POOL_WINDOWS)):
        cols = slice(gi * POOL_GC, (gi + 1) * POOL_GC)
        u_cols = main_ref[:, U_OFF + gi * POOL_GC:U_OFF + (gi + 1) * POOL_GC].reshape(
            ns, t_len, POOL_GC)
        w = POOL_WINDOWS[gi]
        s = u_cols
        for sft in range(1, w):
            s = s + ext_ref[:, BUF_ROWS - sft:BUF_ROWS - sft + t_len, cols]
        cnt = jnp.minimum(w, pos + 1).astype(F32)
        d = (s / cnt - u_cols).reshape(rows, POOL_GC)
        y = _dot(d.astype(BF16), wpool_ref[gi]) * ps_ref[:, cols]
        cat_ref[:, GLA_WIDTH + gi * POOL_GC:GLA_WIDTH + (gi + 1) * POOL_GC] = y.astype(BF16)
    bufout_ref[...] = ext_ref[:, t_len + 1:t_len + BUF_ROWS, :]

    r_i = lax.broadcasted_iota(jnp.int32, (rows, rows), 0)
    c_i = lax.broadcasted_iota(jnp.int32, (rows, rows), 1)
    same_seq = (r_i - jnp.bitwise_and(r_i, t_len - 1)) == (c_i - jnp.bitwise_and(c_i, t_len - 1))
    tri = jnp.logical_and(r_i >= c_i, same_seq).astype(BF16)
    row_in_seq = lax.broadcasted_iota(jnp.int32, (ns, t_len, 1), 1)
    seq_of_row = lax.broadcasted_iota(jnp.int32, (rows, 1), 0) // t_len
    zero_pad = jnp.zeros((LANES - rows, GLA_DK), F32)

    for h in range(GLA_HEADS):
        kcols = slice(h * GLA_DK, (h + 1) * GLA_DK)
        vcols = slice(h * GLA_DV, (h + 1) * GLA_DV)
        qh = main_ref[:, h * GLA_DK:(h + 1) * GLA_DK] * Q_SCALE
        kh = main_ref[:, K_OFF + h * GLA_DK:K_OFF + (h + 1) * GLA_DK]
        vh = main_ref[:, V_OFF + h * GLA_DV:V_OFF + (h + 1) * GLA_DV]
        gt = main_ref[:, G_OFF + h * GLA_DV:G_OFF + (h + 1) * GLA_DV]
        b = _cumsum_rows(tri, gk_ref[:, kcols])
        q3 = qh.reshape(ns, t_len, GLA_DK)
        k3 = kh.reshape(ns, t_len, GLA_DK)
        b3 = b.reshape(ns, t_len, GLA_DK)
        v3 = vh.reshape(ns, t_len, GLA_DV)
        o3 = jnp.zeros((ns, t_len, GLA_DV), F32)
        for jj in range(t_len):
            p = q3 * (k3[:, jj:jj + 1, :] * jnp.exp(jnp.minimum(b3 - b3[:, jj:jj + 1, :], 0.0)))
            col = jnp.sum(p, axis=-1, keepdims=True)
            col = jnp.where(row_in_seq >= jj, col, 0.0)
            o3 = o3 + col * v3[:, jj:jj + 1, :]
        o = o3.reshape(rows, GLA_DV)

        qe = qh * jnp.exp(b)
        k_dec = (k3 * jnp.exp(b3[:, t_len - 1:t_len, :] - b3)).reshape(rows, GLA_DK)
        b_t = jnp.concatenate([b, zero_pad], axis=0).T
        vb = vh.astype(BF16)
        for s in range(ns):
            mine = seq_of_row == s
            s0 = s0_ref[s, h]
            o = o + _dot(jnp.where(mine, qe, 0.0).astype(BF16), s0.astype(BF16))
            last = s * t_len + t_len - 1
            a_col = jnp.exp(b_t[:, last:last + 1])
            upd = _dot_tn(jnp.where(mine, k_dec, 0.0).astype(BF16), vb)
            sout_ref[s, h] = a_col * s0 + upd
        cat_ref[:, vcols] = _gated_head_out(o, gt, gn_ref[...]).astype(BF16)


def _mixer_sample(main, gk, s_in, buf_in, layer, gla_norm, w_pool, pool_scale, cat, stacked, *,
                  n_seq, t_len, pos0, row0):
    ns = SEQ_PER_STEP
    rows = ns * t_len
    rb0 = row0 // rows
    n_alias = 1 + len(stacked)
    n_in = 7
    return pl.pallas_call(
        functools.partial(_mixer_sample_kernel, pos0=pos0, t_len=t_len, n_alias=n_alias),
        grid=(n_seq // ns,),
        in_specs=[
            pl.BlockSpec((rows, MAIN_WIDTH), lambda i: (rb0 + i, 0)),
            pl.BlockSpec((rows, QK_WIDTH), lambda i: (rb0 + i, 0)),
            _layer_spec((ns, GLA_HEADS, GLA_DK, GLA_DV), lambda i: (layer, i, 0, 0, 0)),
            _layer_spec((ns, POOL_BUF, POOL_WIDTH), lambda i: (layer, i, 0, 0)),
            _layer_spec((1, GLA_DV), lambda i: (layer, 0, 0)),
            _layer_spec((len(POOL_WINDOWS), POOL_GC, POOL_GC), lambda i: (layer, 0, 0, 0)),
            _layer_spec((1, POOL_WIDTH), lambda i: (layer, 0, 0)),
        ] + [pl.BlockSpec(memory_space=pl.ANY)] * n_alias,
        out_specs=[
            pl.BlockSpec((rows, D_MODEL), lambda i: (rb0 + i, 0)),
            _layer_spec((ns, GLA_HEADS, GLA_DK, GLA_DV), lambda i: (layer, i, 0, 0, 0)),
            _layer_spec((ns, POOL_BUF, POOL_WIDTH), lambda i: (layer, i, 0, 0)),
        ],
        out_shape=[
            jax.ShapeDtypeStruct(cat.shape, BF16),
            jax.ShapeDtypeStruct((DEPTH, n_seq, GLA_HEADS, GLA_DK, GLA_DV), F32),
            jax.ShapeDtypeStruct((DEPTH, n_seq, POOL_BUF, POOL_WIDTH), F32),
        ],
        scratch_shapes=[pltpu.VMEM((ns, BUF_ROWS + t_len, POOL_WIDTH), F32)],
        input_output_aliases={n_in + k: k for k in range(n_alias)},
        compiler_params=pltpu.CompilerParams(
            dimension_semantics=("arbitrary",), vmem_limit_bytes=VMEM_LIMIT_BYTES),
        name="mixer_sample",
    )(main, gk, s_in, buf_in, gla_norm, w_pool, pool_scale, cat, *stacked)


def _ffn_kernel(*refs, n_x, n_o, n_first_tiles, final_norm):
    x_refs = refs[:n_x]
    cat_ref, wout_ref, nrm_ref, wg_ref, wu_ref, wd_ref, nf_ref = refs[n_x:n_x + 7]
    o_refs = refs[n_x + 7:n_x + 7 + n_o]
    h_ref = refs[-1]
    j = pl.program_id(1)
    nj = pl.num_programs(1)

    @pl.when(j == 0)
    def _():
        if n_o == 2:
            @pl.when(pl.program_id(0) == 0)
            def _():
                o_refs[1][...] = jnp.zeros(o_refs[1].shape, F32)

        def start(x_ref):
            x2 = x_ref[...] + _dot(cat_ref[...], wout_ref[...])
            h_ref[...] = _rmsnorm(x2, nrm_ref[...]).astype(BF16)

            def init(o_ref):
                o_ref[...] = x2

            _for_row_source(o_refs, n_first_tiles, init)

        _for_row_source(x_refs, n_first_tiles, start)

    h = h_ref[...]
    gate = _dot(h, wg_ref[...])
    up = _dot(h, wu_ref[...])
    act = (gate * jax.nn.sigmoid(gate)) * up
    delta = _dot(act.astype(BF16), wd_ref[...])
    if n_o == 1:
        o_refs[0][...] += delta
    else:
        first = pl.program_id(0) < n_first_tiles
        o_refs[0][...] += jnp.where(first, delta, 0.0)
        o_refs[1][...] += jnp.where(first, 0.0, delta)

    if final_norm:
        @pl.when(j == nj - 1)
        def _():
            def finish(o_ref):
                o_ref[...] = _rmsnorm(o_ref[...], nf_ref[...])

            _for_row_source(o_refs, n_first_tiles, finish)


def _ffn(xs, cat, layer, w_out, nrm, w_gate, w_up, w_down, norm_final, *, final_norm, out_rows):
    m = sum(x.shape[0] for x in xs)
    n_first_tiles = (xs[0].shape[0] if len(xs) == 2 else out_rows[0]) // TM_FFN
    grid = (m // TM_FFN, D_FF // TF_FFN)
    if len(out_rows) == 1:
        out_specs = [pl.BlockSpec((TM_FFN, D_MODEL), lambda i, j: (i, 0))]
    else:
        last = n_first_tiles - 1
        out_specs = [
            pl.BlockSpec((TM_FFN, D_MODEL), lambda i, j: (jnp.minimum(i, last), 0)),
            pl.BlockSpec((TM_FFN, D_MODEL), lambda i, j: (jnp.maximum(i - n_first_tiles, 0), 0)),
        ]
    return pl.pallas_call(
        functools.partial(_ffn_kernel, n_x=len(xs), n_o=len(out_rows),
                          n_first_tiles=n_first_tiles, final_norm=final_norm),
        grid=grid,
        in_specs=_split_rows_specs(TM_FFN, n_first_tiles, len(xs) == 2) + [
            pl.BlockSpec((TM_FFN, D_MODEL), lambda i, j: (i, 0)),
            pl.BlockSpec((D_MODEL, D_MODEL), lambda i, j: (0, 0), pipeline_mode=pl.Buffered(1)),
            _layer_spec((1, D_MODEL), lambda i, j: (layer, 0, 0)),
            pl.BlockSpec((D_MODEL, TF_FFN), lambda i, j: (0, j)),
            pl.BlockSpec((D_MODEL, TF_FFN), lambda i, j: (0, j)),
            pl.BlockSpec((TF_FFN, D_MODEL), lambda i, j: (j, 0)),
            pl.BlockSpec((1, D_MODEL), lambda i, j: (0, 0)),
        ],
        out_specs=out_specs,
        out_shape=[jax.ShapeDtypeStruct((r, D_MODEL), F32) for r in out_rows],
        scratch_shapes=[pltpu.VMEM((TM_FFN, D_MODEL), BF16)],
        compiler_params=pltpu.CompilerParams(
            dimension_semantics=("arbitrary", "arbitrary"), vmem_limit_bytes=VMEM_LIMIT_BYTES),
        name="ffn",
    )(*xs, cat, w_out, nrm, w_gate, w_up, w_down, norm_final)


def kernel(x_prompt, x_sample, state_gla, state_pool, norm_mix, w_in, w_gk_up, b_gk, gla_norm,
           w_pool, pool_scale, w_out, norm_ffn, w_gate, w_up, w_down, norm_final):
    n_p, t_p, _ = x_prompt.shape
    n_s, t_s, _ = x_sample.shape
    m_p = n_p * t_p
    m_s = n_s * t_s

    w_in_t = jnp.swapaxes(w_in, 1, 2).astype(BF16)
    gate0 = G_OFF + GLA_WIDTH
    w_lr_t = jnp.pad(w_in_t[:, gate0:gate0 + GLA_LOWRANK],
                     ((0, 0), (0, LANES - GLA_LOWRANK), (0, 0)))
    w_upp = jnp.pad(w_gk_up, ((0, 0), (0, LANES - GLA_LOWRANK), (0, 0))).astype(BF16)
    w_pool_b = w_pool.astype(BF16)
    ffn_weights = (w_out, w_gate, w_up, w_down)
    row3 = lambda a: a[:, None, :]
    norm_mix3, norm_ffn3, b_gk3 = row3(norm_mix), row3(norm_ffn), row3(b_gk)
    gla_norm3, pool_scale3 = row3(gla_norm), row3(pool_scale)

    s0_p = jnp.zeros((n_p, GLA_HEADS, GLA_DK, GLA_DV), F32)
    buf0_p = jnp.zeros((n_p, BUF_ROWS, POOL_WIDTH), F32)

    xs = [x_prompt.reshape(m_p, D_MODEL), x_sample.reshape(m_s, D_MODEL)]
    stacked_p, stacked_s = (), ()
    for l in range(DEPTH):
        last = l == DEPTH - 1
        main, gk = _inproj(xs, l, norm_mix3, w_in_t, w_lr_t, w_upp, b_gk3)
        cat, *rest = _mixer_prompt(main, gk, s0_p, buf0_p, l, gla_norm3, w_pool_b, pool_scale3,
                                   ffn_weights, stacked_p, n_seq=n_p, seq_len=t_p, pos0=0)
        stacked_p, (w_out_b, w_gate_b, w_up_b, w_down_b) = rest[:2], rest[2:]
        cat, *stacked_s = _mixer_sample(main, gk, state_gla, state_pool, l, gla_norm3, w_pool_b,
                                        pool_scale3, cat, stacked_s, n_seq=n_s, t_len=t_s,
                                        pos0=PAST_LEN, row0=m_p)
        xs = _ffn(xs, cat, l, w_out_b, norm_ffn3, w_gate_b, w_up_b, w_down_b, norm_final[None],
                  final_norm=last, out_rows=(m_p, m_s) if last else (m_p + m_s,))

    y_prompt = xs[0].reshape(n_p, t_p, D_MODEL)
    y_sample = xs[1].reshape(n_s, t_s, D_MODEL)
    return (y_prompt, y_sample, stacked_p[0], stacked_p[1], stacked_s[0], stacked_s[1])
```

```python
import functools

import jax
import jax.numpy as jnp
from jax import lax
from jax.experimental import pallas as pl
from jax.experimental.pallas import tpu as pltpu

F32 = jnp.float32
BF16 = jnp.bfloat16

D_MODEL = 2048
DEPTH = 2
EPS = 1e-6
GLA_WIDTH = D_MODEL // 2
GLA_HEADS = 4
GLA_DV = GLA_WIDTH // GLA_HEADS
GLA_DK = GLA_DV // 2
GLA_LOWRANK = 16
GK_NORM = 16.0
GLA_CHUNK = 64
POOL_WIDTH = D_MODEL - GLA_WIDTH
POOL_WINDOWS = (2, 4, 8, 16)
POOL_GC = POOL_WIDTH // len(POOL_WINDOWS)
POOL_BUF = 15
D_FF = ((8 * D_MODEL // 3 + 255) // 256) * 256
QK_WIDTH = GLA_HEADS * GLA_DK
MAIN_WIDTH = 2 * QK_WIDTH + 2 * GLA_WIDTH + POOL_WIDTH
K_OFF = QK_WIDTH
V_OFF = 2 * QK_WIDTH
G_OFF = V_OFF + GLA_WIDTH
U_OFF = G_OFF + GLA_WIDTH
Q_SCALE = GLA_DK ** -0.5
PAST_LEN = 16384

LANES = 128
SUBLANES = 8
VMEM_LIMIT_BYTES = 58 * 1024 * 1024

TM_PROJ = 1024
TN_PROJ = 512
N_ALIGNED_TILES = (2 * QK_WIDTH + 2 * GLA_WIDTH) // TN_PROJ
PROJ_K_CHUNK = 512
TM_FFN = 512
TF_FFN = 512
TT_PROMPT = 256
SUB = 16
ROW_CHUNK = 128
SEQ_PER_STEP = 8
BUF_ROWS = POOL_BUF + 1
EXT0 = BUF_ROWS + SUBLANES
N_SIDE_WEIGHTS = 4


def _rmsnorm(x, g):
    r = lax.rsqrt(jnp.mean(x * x, axis=-1, keepdims=True) + EPS)
    return (x * r) * g


def _dot(a, b):
    return jnp.dot(a, b, preferred_element_type=F32)


def _dot_nt(a, b):
    return lax.dot_general(a, b, (((1,), (1,)), ((), ())), preferred_element_type=F32)


def _dot_tn(a, b):
    return lax.dot_general(a, b, (((0,), (0,)), ((), ())), preferred_element_type=F32)


def _split3(x):
    hi = x.astype(BF16)
    r1 = x - hi.astype(F32)
    mid = r1.astype(BF16)
    lo = (r1 - mid.astype(F32)).astype(BF16)
    return hi, mid, lo


def _cumsum_rows(tri, g):
    hi, mid, lo = _split3(g)
    b3 = _dot(tri, jnp.concatenate([hi, mid, lo], axis=1))
    n = g.shape[1]
    return b3[:, :n] + b3[:, n:2 * n] + b3[:, 2 * n:]


def _layer_spec(block, index_map, **kw):
    return pl.BlockSpec((None,) + tuple(block), index_map, **kw)


def _split_rows_specs(tm, n_first_tiles, two_inputs):
    if not two_inputs:
        return [pl.BlockSpec((tm, D_MODEL), lambda i, j: (i, 0))]
    last = n_first_tiles - 1
    return [
        pl.BlockSpec((tm, D_MODEL), lambda i, j: (jnp.minimum(i, last), 0)),
        pl.BlockSpec((tm, D_MODEL), lambda i, j: (jnp.maximum(i - n_first_tiles, 0), 0),
                     pipeline_mode=pl.Buffered(1)),
    ]


def _for_row_source(x_refs, n_first_tiles, fn):
    if len(x_refs) == 1:
        fn(x_refs[0])
        return
    i = pl.program_id(0)
    pl.when(i < n_first_tiles)(lambda: fn(x_refs[0]))
    pl.when(i >= n_first_tiles)(lambda: fn(x_refs[1]))


def _inproj_kernel(*refs, n_x, n_first_tiles):
    x_refs = refs[:n_x]
    nrm_ref, w_ref, wlr_ref, wup_ref, bgk_ref, main_ref, gk_ref, h_ref = refs[n_x:]
    j = pl.program_id(1)

    @pl.when(j == 0)
    def _():
        def normalize(x_ref):
            def body(r, carry):
                rows = pl.ds(pl.multiple_of(r * ROW_CHUNK, ROW_CHUNK), ROW_CHUNK)
                h_ref[rows, :] = _rmsnorm(x_ref[rows, :], nrm_ref[...]).astype(BF16)
                return carry

            lax.fori_loop(0, TM_PROJ // ROW_CHUNK, body, 0)

        _for_row_source(x_refs, n_first_tiles, normalize)

        lr = _dot_nt(h_ref[...], wlr_ref[...])
        z = _dot(lr.astype(BF16), wup_ref[...]) + bgk_ref[...]
        gk_ref[...] = jax.nn.log_sigmoid(z) / GK_NORM

    kc = PROJ_K_CHUNK
    t = _dot_nt(h_ref[:, 0:kc], w_ref[0, :, 0:kc])
    for k0 in range(kc, D_MODEL, kc):
        t = t + _dot_nt(h_ref[:, k0:k0 + kc], w_ref[0, :, k0:k0 + kc])
    main_ref[...] = t


def _inproj(xs, layer, nrm, w_in_t, w_lr_t, w_up, b_gk):
    m = sum(x.shape[0] for x in xs)
    n_first_tiles = xs[0].shape[0] // TM_PROJ
    grid = (m // TM_PROJ, MAIN_WIDTH // TN_PROJ)

    def w_rows(j):
        start = jnp.where(j < N_ALIGNED_TILES, j * TN_PROJ, j * TN_PROJ + GLA_LOWRANK)
        return pl.multiple_of(start, GLA_LOWRANK)

    return pl.pallas_call(
        functools.partial(_inproj_kernel, n_x=len(xs), n_first_tiles=n_first_tiles),
        grid=grid,
        in_specs=_split_rows_specs(TM_PROJ, n_first_tiles, len(xs) == 2) + [
            _layer_spec((1, D_MODEL), lambda i, j: (layer, 0, 0)),
            pl.BlockSpec((pl.Element(1), pl.Element(TN_PROJ), pl.Element(D_MODEL)),
                         lambda i, j: (layer, w_rows(j), 0)),
            _layer_spec((LANES, D_MODEL), lambda i, j: (layer, 0, 0)),
            _layer_spec((LANES, QK_WIDTH), lambda i, j: (layer, 0, 0)),
            _layer_spec((1, QK_WIDTH), lambda i, j: (layer, 0, 0)),
        ],
        out_specs=[
            pl.BlockSpec((TM_PROJ, TN_PROJ), lambda i, j: (i, j)),
            pl.BlockSpec((TM_PROJ, QK_WIDTH), lambda i, j: (i, 0)),
        ],
        out_shape=[
            jax.ShapeDtypeStruct((m, MAIN_WIDTH), F32),
            jax.ShapeDtypeStruct((m, QK_WIDTH), F32),
        ],
        scratch_shapes=[pltpu.VMEM((TM_PROJ, D_MODEL), BF16)],
        compiler_params=pltpu.CompilerParams(
            dimension_semantics=("arbitrary", "arbitrary"), vmem_limit_bytes=VMEM_LIMIT_BYTES),
        name="inproj",
    )(*xs, nrm, w_in_t, w_lr_t, w_up, b_gk)


def _pool_output(s, u_cols, pos, gi, wpool_ref, ps_ref):
    cnt = jnp.minimum(POOL_WINDOWS[gi], pos + 1).astype(F32)
    d = s / cnt - u_cols
    cols = slice(gi * POOL_GC, (gi + 1) * POOL_GC)
    return _dot(d.astype(BF16), wpool_ref[gi]) * ps_ref[:, cols]


def _window_sums_doubling(ext_ref, lvl_ref, cols, w, n_rows):
    end = EXT0 + n_rows
    n_lvl = w.bit_length() - 1

    def read(lo, hi):
        return ext_ref[lo:hi, cols]

    for lvl in range(n_lvl):
        sh = 1 << lvl
        lo = EXT0 if lvl == n_lvl - 1 else SUBLANES
        val = read(lo, end) + read(lo - sh, end - sh)
        if lvl == n_lvl - 1:
            return val
        buf = lvl_ref.at[lvl % 2]
        buf[lo:end, :] = val

        def read(lo_, hi_, buf=buf):
            return buf[lo_:hi_, :]


def _gla_chunk(q, k, v, g, st_ref, masks):
    tri, row_id, pair_level, on_diag = masks
    c = GLA_CHUNK
    b = _cumsum_rows(tri, g)

    a = jnp.where(on_diag, jnp.sum(q * k, axis=-1, keepdims=True), 0.0)
    for lvl in range(c.bit_length() - 1):
        s = 1 << lvl
        f = jnp.exp(-jnp.abs(b - _segment_mid_rows(b, row_id, s)))
        upper = jnp.bitwise_and(row_id, s) != 0
        x = f * jnp.where(upper, q, k)
        qt = jnp.where(upper, x, 0.0).astype(BF16)
        kt = jnp.where(upper, 0.0, x).astype(BF16)
        a = jnp.where(pair_level == lvl, _dot_nt(qt, kt), a)

    vb = v.astype(BF16)
    o_intra = _dot(a.astype(BF16), vb)
    qe = (q * jnp.exp(b)).astype(BF16)
    outs = []
    for h in range(GLA_HEADS):
        sl = slice(h * c, (h + 1) * c)
        st = st_ref[h]
        outs.append(o_intra[sl] + _dot_nt(qe[sl], st.astype(BF16)))
        b_last = b[(h + 1) * c - 1:(h + 1) * c, :]
        k_dec = k[sl] * jnp.exp(b_last - b[sl])
        st_ref[h] = st * jnp.exp(b_last) + _dot_tn(vb[sl], k_dec.astype(BF16))
    return outs


def _segment_mid_rows(b, row_id, s):
    c, dk = b.shape
    seg = 2 * s
    if seg >= SUBLANES:
        b3 = b.reshape(c // seg, seg, dk)
        return jnp.broadcast_to(b3[:, s:s + 1, :], b3.shape).reshape(c, dk)
    b3 = b.reshape(c // SUBLANES, SUBLANES, dk)
    place = jnp.bitwise_and(row_id, seg - 1)
    out = b
    for p in range(seg):
        if p != s:
            below = pltpu.roll(b3, (p - s) % SUBLANES, axis=1).reshape(c, dk)
            out = jnp.where(place == p, below, out)
    return out


def _gla_masks():
    n = GLA_HEADS * GLA_CHUNK
    row = lax.broadcasted_iota(jnp.int32, (n, n), 0)
    col = lax.broadcasted_iota(jnp.int32, (n, n), 1)
    top_differing_bit = 31 - lax.clz(jnp.bitwise_xor(row, col))
    same_head = top_differing_bit < GLA_CHUNK.bit_length() - 1
    tri = jnp.logical_and(row >= col, same_head).astype(BF16)
    row_id = lax.broadcasted_iota(jnp.int32, (n, GLA_DK), 0)
    pair_level = jnp.where(col < row, top_differing_bit, -1)
    return tri, row_id, pair_level, row == col


def _gated_head_out(o, gt, gn):
    return _rmsnorm(o, gn) * (gt * jax.nn.sigmoid(gt))


def _mixer_prompt_kernel(main_ref, gk_ref, s0_ref, buf0_ref, gn_ref, wpool_ref, ps_ref, *rest,
                         pos0, n_alias):
    n_w = N_SIDE_WEIGHTS
    w32_refs = rest[:n_w]
    rest = rest[n_w + n_alias:]
    cat_ref, sout_ref, bufout_ref = rest[:3]
    w16_refs = rest[3:3 + n_w]
    st_ref, ext_ref, lvl_ref = rest[3 + n_w:]
    t = pl.program_id(1)
    nt = pl.num_programs(1)
    tt = TT_PROMPT

    for w32_ref, w16_ref in zip(w32_refs, w16_refs):
        w16_ref[...] = w32_ref[...].astype(BF16)

    @pl.when(t == 0)
    def _():
        for h in range(GLA_HEADS):
            st_ref[h] = s0_ref[0, h].T
        ext_ref[0:SUBLANES, :] = jnp.zeros((SUBLANES, POOL_WIDTH), F32)
        lvl_ref[:, 0:SUBLANES, :] = jnp.zeros((2, SUBLANES, POOL_GC), F32)
        ext_ref[SUBLANES:EXT0, :] = buf0_ref[0]

    ext_ref[EXT0:EXT0 + tt, :] = main_ref[:, U_OFF:U_OFF + POOL_WIDTH]
    pos = pos0 + t * tt + lax.broadcasted_iota(jnp.int32, (tt, POOL_GC), 0)
    for gi, w in enumerate(POOL_WINDOWS):
        cols = slice(gi * POOL_GC, (gi + 1) * POOL_GC)
        ucols = slice(U_OFF + gi * POOL_GC, U_OFF + (gi + 1) * POOL_GC)
        s = _window_sums_doubling(ext_ref, lvl_ref, cols, w, tt)
        y = _pool_output(s, main_ref[:, ucols], pos, gi, wpool_ref, ps_ref)
        cat_ref[:, GLA_WIDTH + gi * POOL_GC:GLA_WIDTH + (gi + 1) * POOL_GC] = y.astype(BF16)
    ext_ref[SUBLANES:EXT0, :] = ext_ref[tt + SUBLANES:tt + EXT0, :]

    masks = _gla_masks()

    def chunk_body(ci, carry_):
        rows = pl.ds(pl.multiple_of(ci * GLA_CHUNK, GLA_CHUNK), GLA_CHUNK)

        def heads_on_rows(ref, off, width):
            return jnp.concatenate(
                [ref[rows, off + h * width:off + (h + 1) * width] for h in range(GLA_HEADS)], axis=0)

        outs = _gla_chunk(heads_on_rows(main_ref, 0, GLA_DK) * Q_SCALE,
                          heads_on_rows(main_ref, K_OFF, GLA_DK),
                          heads_on_rows(main_ref, V_OFF, GLA_DV),
                          heads_on_rows(gk_ref, 0, GLA_DK), st_ref, masks)
        for h in range(GLA_HEADS):
            vcols = slice(h * GLA_DV, (h + 1) * GLA_DV)
            gt = main_ref[rows, G_OFF + h * GLA_DV:G_OFF + (h + 1) * GLA_DV]
            cat_ref[rows, vcols] = _gated_head_out(outs[h], gt, gn_ref[...]).astype(BF16)
        return carry_

    lax.fori_loop(0, tt // GLA_CHUNK, chunk_body, 0)

    @pl.when(t == nt - 1)
    def _():
        for h in range(GLA_HEADS):
            sout_ref[0, h] = st_ref[h].T
        bufout_ref[0] = ext_ref[EXT0 - POOL_BUF:EXT0, :]


def _mixer_prompt(main, gk, s0, buf0, layer, gla_norm, w_pool, pool_scale, ffn_weights, stacked, *,
                  n_seq, seq_len, pos0):
    tt = TT_PROMPT
    nt = seq_len // tt
    n_steps = n_seq * nt
    row = lambda b, t: b * nt + t
    n_alias = len(stacked)
    n_in = 7 + len(ffn_weights)
    w_blocks = [(w.shape[1] // n_steps, w.shape[2]) for w in ffn_weights]
    return pl.pallas_call(
        functools.partial(_mixer_prompt_kernel, pos0=pos0, n_alias=n_alias),
        grid=(n_seq, nt),
        in_specs=[
            pl.BlockSpec((tt, MAIN_WIDTH), lambda b, t: (row(b, t), 0)),
            pl.BlockSpec((tt, QK_WIDTH), lambda b, t: (row(b, t), 0)),
            pl.BlockSpec((1, GLA_HEADS, GLA_DK, GLA_DV), lambda b, t: (b, 0, 0, 0)),
            pl.BlockSpec((1, BUF_ROWS, POOL_WIDTH), lambda b, t: (b, 0, 0)),
            _layer_spec((1, GLA_DV), lambda b, t: (layer, 0, 0)),
            _layer_spec((len(POOL_WINDOWS), POOL_GC, POOL_GC), lambda b, t: (layer, 0, 0, 0)),
            _layer_spec((1, POOL_WIDTH), lambda b, t: (layer, 0, 0)),
        ] + [_layer_spec(blk, lambda b, t: (layer, row(b, t), 0)) for blk in w_blocks]
        + [pl.BlockSpec(memory_space=pl.ANY)] * n_alias,
        out_specs=[
            pl.BlockSpec((tt, D_MODEL), lambda b, t: (row(b, t), 0)),
            _layer_spec((1, GLA_HEADS, GLA_DK, GLA_DV), lambda b, t: (layer, b, 0, 0, 0)),
            _layer_spec((1, POOL_BUF, POOL_WIDTH), lambda b, t: (layer, b, 0, 0)),
        ] + [pl.BlockSpec(blk, lambda b, t: (row(b, t), 0)) for blk in w_blocks],
        out_shape=[
            jax.ShapeDtypeStruct((main.shape[0], D_MODEL), BF16),
            jax.ShapeDtypeStruct((DEPTH, n_seq, GLA_HEADS, GLA_DK, GLA_DV), F32),
            jax.ShapeDtypeStruct((DEPTH, n_seq, POOL_BUF, POOL_WIDTH), F32),
        ] + [jax.ShapeDtypeStruct(w.shape[1:], BF16) for w in ffn_weights],
        scratch_shapes=[
            pltpu.VMEM((GLA_HEADS, GLA_DV, GLA_DK), F32),
            pltpu.VMEM((EXT0 + tt, POOL_WIDTH), F32),
            pltpu.VMEM((2, EXT0 + tt, POOL_GC), F32),
        ],
        input_output_aliases={n_in + k: 1 + k for k in range(n_alias)},
        compiler_params=pltpu.CompilerParams(
            dimension_semantics=("arbitrary", "arbitrary"), vmem_limit_bytes=VMEM_LIMIT_BYTES),
        name="mixer_prompt",
    )(main, gk, s0, buf0, gla_norm, w_pool, pool_scale, *ffn_weights, *stacked)


def _mixer_sample_kernel(main_ref, gk_ref, s0_ref, buf0_ref, gn_ref, wpool_ref, ps_ref, *rest,
                         pos0, t_len, n_alias):
    cat_ref, sout_ref, bufout_ref, ext_ref = rest[n_alias:]
    ns = SEQ_PER_STEP
    rows = ns * t_len

    ext_ref[:, 1:BUF_ROWS, :] = buf0_ref[...]
    ext_ref[:, BUF_ROWS:BUF_ROWS + t_len, :] = (
        main_ref[:, U_OFF:U_OFF + POOL_WIDTH].reshape(ns, t_len, POOL_WIDTH))
    pos = pos0 + lax.broadcasted_iota(jnp.int32, (ns, t_len, POOL_GC), 1)
    for gi in range(len(POOL_WINDOWS)):
        cols = slice(gi * POOL_GC, (gi + 1) * POOL_GC)
        u_cols = main_ref[:, U_OFF + gi * POOL_GC:U_OFF + (gi + 1) * POOL_GC].reshape(
            ns, t_len, POOL_GC)
        w = POOL_WINDOWS[gi]
        s = u_cols
        for sft in range(1, w):
            s = s + ext_ref[:, BUF_ROWS - sft:BUF_ROWS - sft + t_len, cols]
        cnt = jnp.minimum(w, pos + 1).astype(F32)
        d = (s / cnt - u_cols).reshape(rows, POOL_GC)
        y = _dot(d.astype(BF16), wpool_ref[gi]) * ps_ref[:, cols]
        cat_ref[:, GLA_WIDTH + gi * POOL_GC:GLA_WIDTH + (gi + 1) * POOL_GC] = y.astype(BF16)
    bufout_ref[...] = ext_ref[:, t_len + 1:t_len + BUF_ROWS, :]

    r_i = lax.broadcasted_iota(jnp.int32, (rows, rows), 0)
    c_i = lax.broadcasted_iota(jnp.int32, (rows, rows), 1)
    same_seq = (r_i - jnp.bitwise_and(r_i, t_len - 1)) == (c_i - jnp.bitwise_and(c_i, t_len - 1))
    tri = jnp.logical_and(r_i >= c_i, same_seq).astype(BF16)
    row_in_seq = lax.broadcasted_iota(jnp.int32, (ns, t_len, 1), 1)
    seq_of_row = lax.broadcasted_iota(jnp.int32, (rows, 1), 0) // t_len
    zero_pad = jnp.zeros((LANES - rows, GLA_DK), F32)

    for h in range(GLA_HEADS):
        kcols = slice(h * GLA_DK, (h + 1) * GLA_DK)
        vcols = slice(h * GLA_DV, (h + 1) * GLA_DV)
        qh = main_ref[:, h * GLA_DK:(h + 1) * GLA_DK] * Q_SCALE
        kh = main_ref[:, K_OFF + h * GLA_DK:K_OFF + (h + 1) * GLA_DK]
        vh = main_ref[:, V_OFF + h * GLA_DV:V_OFF + (h + 1) * GLA_DV]
        gt = main_ref[:, G_OFF + h * GLA_DV:G_OFF + (h + 1) * GLA_DV]
        b = _cumsum_rows(tri, gk_ref[:, kcols])
        q3 = qh.reshape(ns, t_len, GLA_DK)
        k3 = kh.reshape(ns, t_len, GLA_DK)
        b3 = b.reshape(ns, t_len, GLA_DK)
        v3 = vh.reshape(ns, t_len, GLA_DV)
        o3 = jnp.zeros((ns, t_len, GLA_DV), F32)
        for jj in range(t_len):
            p = q3 * (k3[:, jj:jj + 1, :] * jnp.exp(jnp.minimum(b3 - b3[:, jj:jj + 1, :], 0.0)))
            col = jnp.sum(p, axis=-1, keepdims=True)
            col = jnp.where(row_in_seq >= jj, col, 0.0)
            o3 = o3 + col * v3[:, jj:jj + 1, :]
        o = o3.reshape(rows, GLA_DV)

        qe = qh * jnp.exp(b)
        k_dec = (k3 * jnp.exp(b3[:, t_len - 1:t_len, :] - b3)).reshape(rows, GLA_DK)
        b_t = jnp.concatenate([b, zero_pad], axis=0).T
        vb = vh.astype(BF16)
        for s in range(ns):
            mine = seq_of_row == s
            s0 = s0_ref[s, h]
            o = o + _dot(jnp.where(mine, qe, 0.0).astype(BF16), s0.astype(BF16))
            last = s * t_len + t_len - 1
            a_col = jnp.exp(b_t[:, last:last + 1])
            upd = _dot_tn(jnp.where(mine, k_dec, 0.0).astype(BF16), vb)
            sout_ref[s, h] = a_col * s0 + upd
        cat_ref[:, vcols] = _gated_head_out(o, gt, gn_ref[...]).astype(BF16)


def _mixer_sample(main, gk, s_in, buf_in, layer, gla_norm, w_pool, pool_scale, cat, stacked, *,
                  n_seq, t_len, pos0, row0):
    ns = SEQ_PER_STEP
    rows = ns * t_len
    rb0 = row0 // rows
    n_alias = 1 + len(stacked)
    n_in = 7
    return pl.pallas_call(
        functools.partial(_mixer_sample_kernel, pos0=pos0, t_len=t_len, n_alias=n_alias),
        grid=(n_seq // ns,),
        in_specs=[
            pl.BlockSpec((rows, MAIN_WIDTH), lambda i: (rb0 + i, 0)),
            pl.BlockSpec((rows, QK_WIDTH), lambda i: (rb0 + i, 0)),
            _layer_spec((ns, GLA_HEADS, GLA_DK, GLA_DV), lambda i: (layer, i, 0, 0, 0)),
            _layer_spec((ns, POOL_BUF, POOL_WIDTH), lambda i: (layer, i, 0, 0)),
            _layer_spec((1, GLA_DV), lambda i: (layer, 0, 0)),
            _layer_spec((len(POOL_WINDOWS), POOL_GC, POOL_GC), lambda i: (layer, 0, 0, 0)),
            _layer_spec((1, POOL_WIDTH), lambda i: (layer, 0, 0)),
        ] + [pl.BlockSpec(memory_space=pl.ANY)] * n_alias,
        out_specs=[
            pl.BlockSpec((rows, D_MODEL), lambda i: (rb0 + i, 0)),
            _layer_spec((ns, GLA_HEADS, GLA_DK, GLA_DV), lambda i: (layer, i, 0, 0, 0)),
            _layer_spec((ns, POOL_BUF, POOL_WIDTH), lambda i: (layer, i, 0, 0)),
        ],
        out_shape=[
            jax.ShapeDtypeStruct(cat.shape, BF16),
            jax.ShapeDtypeStruct((DEPTH, n_seq, GLA_HEADS, GLA_DK, GLA_DV), F32),
            jax.ShapeDtypeStruct((DEPTH, n_seq, POOL_BUF, POOL_WIDTH), F32),
        ],
        scratch_shapes=[pltpu.VMEM((ns, BUF_ROWS + t_len, POOL_WIDTH), F32)],
        input_output_aliases={n_in + k: k for k in range(n_alias)},
        compiler_params=pltpu.CompilerParams(
            dimension_semantics=("arbitrary",), vmem_limit_bytes=VMEM_LIMIT_BYTES),
        name="mixer_sample",
    )(main, gk, s_in, buf_in, gla_norm, w_pool, pool_scale, cat, *stacked)


def _ffn_kernel(*refs, n_x, n_o, n_first_tiles, final_norm):
    x_refs = refs[:n_x]
    cat_ref, wout_ref, nrm_ref, wg_ref, wu_ref, wd_ref, nf_ref = refs[n_x:n_x + 7]
    o_refs = refs[n_x + 7:n_x + 7 + n_o]
    h_ref = refs[-1]
    j = pl.program_id(1)
    nj = pl.num_programs(1)

    @pl.when(j == 0)
    def _():
        if n_o == 2:
            @pl.when(pl.program_id(0) == 0)
            def _():
                o_refs[1][...] = jnp.zeros(o_refs[1].shape, F32)

        def start(x_ref):
            x2 = x_ref[...] + _dot(cat_ref[...], wout_ref[...])
            h_ref[...] = _rmsnorm(x2, nrm_ref[...]).astype(BF16)

            def init(o_ref):
                o_ref[...] = x2

            _for_row_source(o_refs, n_first_tiles, init)

        _for_row_source(x_refs, n_first_tiles, start)

    h = h_ref[...]
    gate = _dot(h, wg_ref[...])
    up = _dot(h, wu_ref[...])
    act = (gate * jax.nn.sigmoid(gate)) * up
    delta = _dot(act.astype(BF16), wd_ref[...])
    if n_o == 1:
        o_refs[0][...] += delta
    else:
        first = pl.program_id(0) < n_first_tiles
        o_refs[0][...] += jnp.where(first, delta, 0.0)
        o_refs[1][...] += jnp.where(first, 0.0, delta)

    if final_norm:
        @pl.when(j == nj - 1)
        def _():
            def finish(o_ref):
                o_ref[...] = _rmsnorm(o_ref[...], nf_ref[...])

            _for_row_source(o_refs, n_first_tiles, finish)


def _ffn(xs, cat, layer, w_out, nrm, w_gate, w_up, w_down, norm_final, *, final_norm, out_rows):
    m = sum(x.shape[0] for x in xs)
    n_first_tiles = (xs[0].shape[0] if len(xs) == 2 else out_rows[0]) // TM_FFN
    grid = (m // TM_FFN, D_FF // TF_FFN)
    if len(out_rows) == 1:
        out_specs = [pl.BlockSpec((TM_FFN, D_MODEL), lambda i, j: (i, 0))]
    else:
        last = n_first_tiles - 1
        out_specs = [
            pl.BlockSpec((TM_FFN, D_MODEL), lambda i, j: (jnp.minimum(i, last), 0)),
            pl.BlockSpec((TM_FFN, D_MODEL), lambda i, j: (jnp.maximum(i - n_first_tiles, 0), 0)),
        ]
    return pl.pallas_call(
        functools.partial(_ffn_kernel, n_x=len(xs), n_o=len(out_rows),
                          n_first_tiles=n_first_tiles, final_norm=final_norm),
        grid=grid,
        in_specs=_split_rows_specs(TM_FFN, n_first_tiles, len(xs) == 2) + [
            pl.BlockSpec((TM_FFN, D_MODEL), lambda i, j: (i, 0)),
            pl.BlockSpec((D_MODEL, D_MODEL), lambda i, j: (0, 0), pipeline_mode=pl.Buffered(1)),
            _layer_spec((1, D_MODEL), lambda i, j: (layer, 0, 0)),
            pl.BlockSpec((D_MODEL, TF_FFN), lambda i, j: (0, j)),
            pl.BlockSpec((D_MODEL, TF_FFN), lambda i, j: (0, j)),
            pl.BlockSpec((TF_FFN, D_MODEL), lambda i, j: (j, 0)),
            pl.BlockSpec((1, D_MODEL), lambda i, j: (0, 0)),
        ],
        out_specs=out_specs,
        out_shape=[jax.ShapeDtypeStruct((r, D_MODEL), F32) for r in out_rows],
        scratch_shapes=[pltpu.VMEM((TM_FFN, D_MODEL), BF16)],
        compiler_params=pltpu.CompilerParams(
            dimension_semantics=("arbitrary", "arbitrary"), vmem_limit_bytes=VMEM_LIMIT_BYTES),
        name="ffn",
    )(*xs, cat, w_out, nrm, w_gate, w_up, w_down, norm_final)


def kernel(x_prompt, x_sample, state_gla, state_pool, norm_mix, w_in, w_gk_up, b_gk, gla_norm,
           w_pool, pool_scale, w_out, norm_ffn, w_gate, w_up, w_down, norm_final):
    n_p, t_p, _ = x_prompt.shape
    n_s, t_s, _ = x_sample.shape
    m_p = n_p * t_p
    m_s = n_s * t_s

    w_in_t = jnp.swapaxes(w_in, 1, 2).astype(BF16)
    gate0 = G_OFF + GLA_WIDTH
    w_lr_t = jnp.pad(w_in_t[:, gate0:gate0 + GLA_LOWRANK],
                     ((0, 0), (0, LANES - GLA_LOWRANK), (0, 0)))
    w_upp = jnp.pad(w_gk_up, ((0, 0), (0, LANES - GLA_LOWRANK), (0, 0))).astype(BF16)
    w_pool_b = w_pool.astype(BF16)
    ffn_weights = (w_out, w_gate, w_up, w_down)
    row3 = lambda a: a[:, None, :]
    norm_mix3, norm_ffn3, b_gk3 = row3(norm_mix), row3(norm_ffn), row3(b_gk)
    gla_norm3, pool_scale3 = row3(gla_norm), row3(pool_scale)

    s0_p = jnp.zeros((n_p, GLA_HEADS, GLA_DK, GLA_DV), F32)
    buf0_p = jnp.zeros((n_p, BUF_ROWS, POOL_WIDTH), F32)

    xs = [x_prompt.reshape(m_p, D_MODEL), x_sample.reshape(m_s, D_MODEL)]
    stacked_p, stacked_s = (), ()
    for l in range(DEPTH):
        last = l == DEPTH - 1
        main, gk = _inproj(xs, l, norm_mix3, w_in_t, w_lr_t, w_upp, b_gk3)
        cat, *rest = _mixer_prompt(main, gk, s0_p, buf0_p, l, gla_norm3, w_pool_b, pool_scale3,
                                   ffn_weights, stacked_p, n_seq=n_p, seq_len=t_p, pos0=0)
        stacked_p, (w_out_b, w_gate_b, w_up_b, w_down_b) = rest[:2], rest[2:]
        cat, *stacked_s = _mixer_sample(main, gk, state_gla, state_pool, l, gla_norm3, w_pool_b,
                                        pool_scale3, cat, stacked_s, n_seq=n_s, t_len=t_s,
                                        pos0=PAST_LEN, row0=m_p)
        xs = _ffn(xs, cat, l, w_out_b, norm_ffn3, w_gate_b, w_up_b, w_down_b, norm_final[None],
                  final_norm=last, out_rows=(m_p, m_s) if last else (m_p + m_s,))

    y_prompt = xs[0].reshape(n_p, t_p, D_MODEL)
    y_sample = xs[1].reshape(n_s, t_s, D_MODEL)
    return (y_prompt, y_sample, stacked_p[0], stacked_p[1], stacked_s[0], stacked_s[1])
```

```python
import functools

import jax
import jax.numpy as jnp
from jax import lax
from jax.experimental import pallas as pl
from jax.experimental.pallas import tpu as pltpu

F32 = jnp.float32
BF16 = jnp.bfloat16

D_MODEL = 2048
DEPTH = 2
EPS = 1e-6
GLA_WIDTH = D_MODEL // 2
GLA_HEADS = 4
GLA_DV = GLA_WIDTH // GLA_HEADS
GLA_DK = GLA_DV // 2
GLA_LOWRANK = 16
GK_NORM = 16.0
GLA_CHUNK = 64
POOL_WIDTH = D_MODEL - GLA_WIDTH
POOL_WINDOWS = (2, 4, 8, 16)
POOL_GC = POOL_WIDTH // len(POOL_WINDOWS)
POOL_BUF = 15
D_FF = ((8 * D_MODEL // 3 + 255) // 256) * 256
QK_WIDTH = GLA_HEADS * GLA_DK
MAIN_WIDTH = 2 * QK_WIDTH + 2 * GLA_WIDTH + POOL_WIDTH
K_OFF = QK_WIDTH
V_OFF = 2 * QK_WIDTH
G_OFF = V_OFF + GLA_WIDTH
U_OFF = G_OFF + GLA_WIDTH
Q_SCALE = GLA_DK ** -0.5
PAST_LEN = 16384

LANES = 128
SUBLANES = 8
VMEM_LIMIT_BYTES = 58 * 1024 * 1024

TM_PROJ = 1024
TN_PROJ = 1024
N_ALIGNED_TILES = (2 * QK_WIDTH + 2 * GLA_WIDTH) // TN_PROJ
PROJ_K_CHUNK = 512
TM_FFN = 512
TF_FFN = 512
TT_PROMPT = 256
SUB = 16
ROW_CHUNK = 128
SEQ_PER_STEP = 8
BUF_ROWS = POOL_BUF + 1
EXT0 = BUF_ROWS + SUBLANES
N_SIDE_WEIGHTS = 4


def _rmsnorm(x, g):
    r = lax.rsqrt(jnp.mean(x * x, axis=-1, keepdims=True) + EPS)
    return (x * r) * g


def _dot(a, b):
    return jnp.dot(a, b, preferred_element_type=F32)


def _dot_nt(a, b):
    return lax.dot_general(a, b, (((1,), (1,)), ((), ())), preferred_element_type=F32)


def _dot_tn(a, b):
    return lax.dot_general(a, b, (((0,), (0,)), ((), ())), preferred_element_type=F32)


def _split3(x):
    hi = x.astype(BF16)
    r1 = x - hi.astype(F32)
    mid = r1.astype(BF16)
    lo = (r1 - mid.astype(F32)).astype(BF16)
    return hi, mid, lo


def _cumsum_rows(tri, g):
    hi, mid, lo = _split3(g)
    b3 = _dot(tri, jnp.concatenate([hi, mid, lo], axis=1))
    n = g.shape[1]
    return b3[:, :n] + b3[:, n:2 * n] + b3[:, 2 * n:]


def _layer_spec(block, index_map, **kw):
    return pl.BlockSpec((None,) + tuple(block), index_map, **kw)


def _split_rows_specs(tm, n_first_tiles, two_inputs):
    if not two_inputs:
        return [pl.BlockSpec((tm, D_MODEL), lambda i, j: (i, 0))]
    last = n_first_tiles - 1
    return [
        pl.BlockSpec((tm, D_MODEL), lambda i, j: (jnp.minimum(i, last), 0)),
        pl.BlockSpec((tm, D_MODEL), lambda i, j: (jnp.maximum(i - n_first_tiles, 0), 0),
                     pipeline_mode=pl.Buffered(1)),
    ]


def _for_row_source(x_refs, n_first_tiles, fn):
    if len(x_refs) == 1:
        fn(x_refs[0])
        return
    i = pl.program_id(0)
    pl.when(i < n_first_tiles)(lambda: fn(x_refs[0]))
    pl.when(i >= n_first_tiles)(lambda: fn(x_refs[1]))


def _inproj_kernel(*refs, n_x, n_first_tiles):
    x_refs = refs[:n_x]
    nrm_ref, w_ref, wlr_ref, wup_ref, bgk_ref, main_ref, gk_ref, h_ref = refs[n_x:]
    j = pl.program_id(1)

    @pl.when(j == 0)
    def _():
        def normalize(x_ref):
            def body(r, carry):
                rows = pl.ds(pl.multiple_of(r * ROW_CHUNK, ROW_CHUNK), ROW_CHUNK)
                h_ref[rows, :] = _rmsnorm(x_ref[rows, :], nrm_ref[...]).astype(BF16)
                return carry

            lax.fori_loop(0, TM_PROJ // ROW_CHUNK, body, 0)

        _for_row_source(x_refs, n_first_tiles, normalize)

        lr = _dot_nt(h_ref[...], wlr_ref[...])
        z = _dot(lr.astype(BF16), wup_ref[...]) + bgk_ref[...]
        gk_ref[...] = jax.nn.log_sigmoid(z) / GK_NORM

    kc = PROJ_K_CHUNK
    t = _dot_nt(h_ref[:, 0:kc], w_ref[0, :, 0:kc])
    for k0 in range(kc, D_MODEL, kc):
        t = t + _dot_nt(h_ref[:, k0:k0 + kc], w_ref[0, :, k0:k0 + kc])
    main_ref[...] = t


def _inproj(xs, layer, nrm, w_in_t, w_lr_t, w_up, b_gk):
    m = sum(x.shape[0] for x in xs)
    n_first_tiles = xs[0].shape[0] // TM_PROJ
    grid = (m // TM_PROJ, MAIN_WIDTH // TN_PROJ)

    def w_rows(j):
        start = jnp.where(j < N_ALIGNED_TILES, j * TN_PROJ, j * TN_PROJ + GLA_LOWRANK)
        return pl.multiple_of(start, GLA_LOWRANK)

    return pl.pallas_call(
        functools.partial(_inproj_kernel, n_x=len(xs), n_first_tiles=n_first_tiles),
        grid=grid,
        in_specs=_split_rows_specs(TM_PROJ, n_first_tiles, len(xs) == 2) + [
            _layer_spec((1, D_MODEL), lambda i, j: (layer, 0, 0)),
            pl.BlockSpec((pl.Element(1), pl.Element(TN_PROJ), pl.Element(D_MODEL)),
                         lambda i, j: (layer, w_rows(j), 0)),
            _layer_spec((LANES, D_MODEL), lambda i, j: (layer, 0, 0)),
            _layer_spec((LANES, QK_WIDTH), lambda i, j: (layer, 0, 0)),
            _layer_spec((1, QK_WIDTH), lambda i, j: (layer, 0, 0)),
        ],
        out_specs=[
            pl.BlockSpec((TM_PROJ, TN_PROJ), lambda i, j: (i, j)),
            pl.BlockSpec((TM_PROJ, QK_WIDTH), lambda i, j: (i, 0)),
        ],
        out_shape=[
            jax.ShapeDtypeStruct((m, MAIN_WIDTH), F32),
            jax.ShapeDtypeStruct((m, QK_WIDTH), F32),
        ],
        scratch_shapes=[pltpu.VMEM((TM_PROJ, D_MODEL), BF16)],
        compiler_params=pltpu.CompilerParams(
            dimension_semantics=("arbitrary", "arbitrary"), vmem_limit_bytes=VMEM_LIMIT_BYTES),
        name="inproj",
    )(*xs, nrm, w_in_t, w_lr_t, w_up, b_gk)


def _pool_output(s, u_cols, pos, gi, wpool_ref, ps_ref):
    cnt = jnp.minimum(POOL_WINDOWS[gi], pos + 1).astype(F32)
    d = s / cnt - u_cols
    cols = slice(gi * POOL_GC, (gi + 1) * POOL_GC)
    return _dot(d.astype(BF16), wpool_ref[gi]) * ps_ref[:, cols]


def _window_sums_doubling(ext_ref, lvl_ref, cols, w, n_rows):
    end = EXT0 + n_rows
    n_lvl = w.bit_length() - 1

    def read(lo, hi):
        return ext_ref[lo:hi, cols]

    for lvl in range(n_lvl):
        sh = 1 << lvl
        lo = EXT0 if lvl == n_lvl - 1 else SUBLANES
        val = read(lo, end) + read(lo - sh, end - sh)
        if lvl == n_lvl - 1:
            return val
        buf = lvl_ref.at[lvl % 2]
        buf[lo:end, :] = val

        def read(lo_, hi_, buf=buf):
            return buf[lo_:hi_, :]


def _gla_chunk(q, k, v, g, st_ref, masks):
    tri, row_id, pair_level, on_diag = masks
    c = GLA_CHUNK
    b = _cumsum_rows(tri, g)

    a = jnp.where(on_diag, jnp.sum(q * k, axis=-1, keepdims=True), 0.0)
    for lvl in range(c.bit_length() - 1):
        s = 1 << lvl
        f = jnp.exp(-jnp.abs(b - _segment_mid_rows(b, row_id, s)))
        upper = jnp.bitwise_and(row_id, s) != 0
        x = f * jnp.where(upper, q, k)
        qt = jnp.where(upper, x, 0.0).astype(BF16)
        kt = jnp.where(upper, 0.0, x).astype(BF16)
        a = jnp.where(pair_level == lvl, _dot_nt(qt, kt), a)

    vb = v.astype(BF16)
    o_intra = _dot(a.astype(BF16), vb)
    qe = (q * jnp.exp(b)).astype(BF16)
    outs = []
    for h in range(GLA_HEADS):
        sl = slice(h * c, (h + 1) * c)
        st = st_ref[h]
        outs.append(o_intra[sl] + _dot_nt(qe[sl], st.astype(BF16)))
        b_last = b[(h + 1) * c - 1:(h + 1) * c, :]
        k_dec = k[sl] * jnp.exp(b_last - b[sl])
        st_ref[h] = st * jnp.exp(b_last) + _dot_tn(vb[sl], k_dec.astype(BF16))
    return outs


def _segment_mid_rows(b, row_id, s):
    c, dk = b.shape
    seg = 2 * s
    if seg >= SUBLANES:
        b3 = b.reshape(c // seg, seg, dk)
        return jnp.broadcast_to(b3[:, s:s + 1, :], b3.shape).reshape(c, dk)
    b3 = b.reshape(c // SUBLANES, SUBLANES, dk)
    place = jnp.bitwise_and(row_id, seg - 1)
    out = b
    for p in range(seg):
        if p != s:
            below = pltpu.roll(b3, (p - s) % SUBLANES, axis=1).reshape(c, dk)
            out = jnp.where(place == p, below, out)
    return out


def _gla_masks():
    n = GLA_HEADS * GLA_CHUNK
    row = lax.broadcasted_iota(jnp.int32, (n, n), 0)
    col = lax.broadcasted_iota(jnp.int32, (n, n), 1)
    top_differing_bit = 31 - lax.clz(jnp.bitwise_xor(row, col))
    same_head = top_differing_bit < GLA_CHUNK.bit_length() - 1
    tri = jnp.logical_and(row >= col, same_head).astype(BF16)
    row_id = lax.broadcasted_iota(jnp.int32, (n, GLA_DK), 0)
    pair_level = jnp.where(col < row, top_differing_bit, -1)
    return tri, row_id, pair_level, row == col


def _gated_head_out(o, gt, gn):
    return _rmsnorm(o, gn) * (gt * jax.nn.sigmoid(gt))


def _mixer_prompt_kernel(main_ref, gk_ref, s0_ref, buf0_ref, gn_ref, wpool_ref, ps_ref, *rest,
                         pos0, n_alias):
    n_w = N_SIDE_WEIGHTS
    w32_refs = rest[:n_w]
    rest = rest[n_w + n_alias:]
    cat_ref, sout_ref, bufout_ref = rest[:3]
    w16_refs = rest[3:3 + n_w]
    st_ref, ext_ref, lvl_ref = rest[3 + n_w:]
    t = pl.program_id(1)
    nt = pl.num_programs(1)
    tt = TT_PROMPT

    for w32_ref, w16_ref in zip(w32_refs, w16_refs):
        w16_ref[...] = w32_ref[...].astype(BF16)

    @pl.when(t == 0)
    def _():
        for h in range(GLA_HEADS):
            st_ref[h] = s0_ref[0, h].T
        ext_ref[0:SUBLANES, :] = jnp.zeros((SUBLANES, POOL_WIDTH), F32)
        lvl_ref[:, 0:SUBLANES, :] = jnp.zeros((2, SUBLANES, POOL_GC), F32)
        ext_ref[SUBLANES:EXT0, :] = buf0_ref[0]

    ext_ref[EXT0:EXT0 + tt, :] = main_ref[:, U_OFF:U_OFF + POOL_WIDTH]
    pos = pos0 + t * tt + lax.broadcasted_iota(jnp.int32, (tt, POOL_GC), 0)
    for gi, w in enumerate(POOL_WINDOWS):
        cols = slice(gi * POOL_GC, (gi + 1) * POOL_GC)
        ucols = slice(U_OFF + gi * POOL_GC, U_OFF + (gi + 1) * POOL_GC)
        s = _window_sums_doubling(ext_ref, lvl_ref, cols, w, tt)
        y = _pool_output(s, main_ref[:, ucols], pos, gi, wpool_ref, ps_ref)
        cat_ref[:, GLA_WIDTH + gi * POOL_GC:GLA_WIDTH + (gi + 1) * POOL_GC] = y.astype(BF16)
    ext_ref[SUBLANES:EXT0, :] = ext_ref[tt + SUBLANES:tt + EXT0, :]

    masks = _gla_masks()

    def chunk_body(ci, carry_):
        rows = pl.ds(pl.multiple_of(ci * GLA_CHUNK, GLA_CHUNK), GLA_CHUNK)

        def heads_on_rows(ref, off, width):
            return jnp.concatenate(
                [ref[rows, off + h * width:off + (h + 1) * width] for h in range(GLA_HEADS)], axis=0)

        outs = _gla_chunk(heads_on_rows(main_ref, 0, GLA_DK) * Q_SCALE,
                          heads_on_rows(main_ref, K_OFF, GLA_DK),
                          heads_on_rows(main_ref, V_OFF, GLA_DV),
                          heads_on_rows(gk_ref, 0, GLA_DK), st_ref, masks)
        for h in range(GLA_HEADS):
            vcols = slice(h * GLA_DV, (h + 1) * GLA_DV)
            gt = main_ref[rows, G_OFF + h * GLA_DV:G_OFF + (h + 1) * GLA_DV]
            cat_ref[rows, vcols] = _gated_head_out(outs[h], gt, gn_ref[...]).astype(BF16)
        return carry_

    lax.fori_loop(0, tt // GLA_CHUNK, chunk_body, 0, unroll=True)

    @pl.when(t == nt - 1)
    def _():
        for h in range(GLA_HEADS):
            sout_ref[0, h] = st_ref[h].T
        bufout_ref[0] = ext_ref[EXT0 - POOL_BUF:EXT0, :]


def _mixer_prompt(main, gk, s0, buf0, layer, gla_norm, w_pool, pool_scale, ffn_weights, stacked, *,
                  n_seq, seq_len, pos0):
    tt = TT_PROMPT
    nt = seq_len // tt
    n_steps = n_seq * nt
    row = lambda b, t: b * nt + t
    n_alias = len(stacked)
    n_in = 7 + len(ffn_weights)
    w_blocks = [(w.shape[1] // n_steps, w.shape[2]) for w in ffn_weights]
    return pl.pallas_call(
        functools.partial(_mixer_prompt_kernel, pos0=pos0, n_alias=n_alias),
        grid=(n_seq, nt),
        in_specs=[
            pl.BlockSpec((tt, MAIN_WIDTH), lambda b, t: (row(b, t), 0)),
            pl.BlockSpec((tt, QK_WIDTH), lambda b, t: (row(b, t), 0)),
            pl.BlockSpec((1, GLA_HEADS, GLA_DK, GLA_DV), lambda b, t: (b, 0, 0, 0)),
            pl.BlockSpec((1, BUF_ROWS, POOL_WIDTH), lambda b, t: (b, 0, 0)),
            _layer_spec((1, GLA_DV), lambda b, t: (layer, 0, 0)),
            _layer_spec((len(POOL_WINDOWS), POOL_GC, POOL_GC), lambda b, t: (layer, 0, 0, 0)),
            _layer_spec((1, POOL_WIDTH), lambda b, t: (layer, 0, 0)),
        ] + [_layer_spec(blk, lambda b, t: (layer, row(b, t), 0)) for blk in w_blocks]
        + [pl.BlockSpec(memory_space=pl.ANY)] * n_alias,
        out_specs=[
            pl.BlockSpec((tt, D_MODEL), lambda b, t: (row(b, t), 0)),
            _layer_spec((1, GLA_HEADS, GLA_DK, GLA_DV), lambda b, t: (layer, b, 0, 0, 0)),
            _layer_spec((1, POOL_BUF, POOL_WIDTH), lambda b, t: (layer, b, 0, 0)),
        ] + [pl.BlockSpec(blk, lambda b, t: (row(b, t), 0)) for blk in w_blocks],
        out_shape=[
            jax.ShapeDtypeStruct((main.shape[0], D_MODEL), BF16),
            jax.ShapeDtypeStruct((DEPTH, n_seq, GLA_HEADS, GLA_DK, GLA_DV), F32),
            jax.ShapeDtypeStruct((DEPTH, n_seq, POOL_BUF, POOL_WIDTH), F32),
        ] + [jax.ShapeDtypeStruct(w.shape[1:], BF16) for w in ffn_weights],
        scratch_shapes=[
            pltpu.VMEM((GLA_HEADS, GLA_DV, GLA_DK), F32),
            pltpu.VMEM((EXT0 + tt, POOL_WIDTH), F32),
            pltpu.VMEM((2, EXT0 + tt, POOL_GC), F32),
        ],
        input_output_aliases={n_in + k: 1 + k for k in range(n_alias)},
        compiler_params=pltpu.CompilerParams(
            dimension_semantics=("arbitrary", "arbitrary"), vmem_limit_bytes=VMEM_LIMIT_BYTES),
        name="mixer_prompt",
    )(main, gk, s0, buf0, gla_norm, w_pool, pool_scale, *ffn_weights, *stacked)


def _mixer_sample_kernel(main_ref, gk_ref, s0_ref, buf0_ref, gn_ref, wpool_ref, ps_ref, *rest,
                         pos0, t_len, n_alias):
    cat_ref, sout_ref, bufout_ref, ext_ref = rest[n_alias:]
    ns = SEQ_PER_STEP
    rows = ns * t_len

    ext_ref[:, 1:BUF_ROWS, :] = buf0_ref[...]
    ext_ref[:, BUF_ROWS:BUF_ROWS + t_len, :] = (
        main_ref[:, U_OFF:U_OFF + POOL_WIDTH].reshape(ns, t_len, POOL_WIDTH))
    pos = pos0 + lax.broadcasted_iota(jnp.int32, (ns, t_len, POOL_GC), 1)
    for gi in range(len(POOL_WINDOWS)):
        cols = slice(gi * POOL_GC, (gi + 1) * POOL_GC)
        u_cols = main_ref[:, U_OFF + gi * POOL_GC:U_OFF + (gi + 1) * POOL_GC].reshape(
            ns, t_len, POOL_GC)
        w = POOL_WINDOWS[gi]
        s = u_cols
        for sft in range(1, w):
            s = s + ext_ref[:, BUF_ROWS - sft:BUF_ROWS - sft + t_len, cols]
        cnt = jnp.minimum(w, pos + 1).astype(F32)
        d = (s / cnt - u_cols).reshape(rows, POOL_GC)
        y = _dot(d.astype(BF16), wpool_ref[gi]) * ps_ref[:, cols]
        cat_ref[:, GLA_WIDTH + gi * POOL_GC:GLA_WIDTH + (gi + 1) * POOL_GC] = y.astype(BF16)
    bufout_ref[...] = ext_ref[:, t_len + 1:t_len + BUF_ROWS, :]

    r_i = lax.broadcasted_iota(jnp.int32, (rows, rows), 0)
    c_i = lax.broadcasted_iota(jnp.int32, (rows, rows), 1)
    same_seq = (r_i - jnp.bitwise_and(r_i, t_len - 1)) == (c_i - jnp.bitwise_and(c_i, t_len - 1))
    tri = jnp.logical_and(r_i >= c_i, same_seq).astype(BF16)
    row_in_seq = lax.broadcasted_iota(jnp.int32, (ns, t_len, 1), 1)
    seq_of_row = lax.broadcasted_iota(jnp.int32, (rows, 1), 0) // t_len
    zero_pad = jnp.zeros((LANES - rows, GLA_DK), F32)

    for h in range(GLA_HEADS):
        kcols = slice(h * GLA_DK, (h + 1) * GLA_DK)
        vcols = slice(h * GLA_DV, (h + 1) * GLA_DV)
        qh = main_ref[:, h * GLA_DK:(h + 1) * GLA_DK] * Q_SCALE
        kh = main_ref[:, K_OFF + h * GLA_DK:K_OFF + (h + 1) * GLA_DK]
        vh = main_ref[:, V_OFF + h * GLA_DV:V_OFF + (h + 1) * GLA_DV]
        gt = main_ref[:, G_OFF + h * GLA_DV:G_OFF + (h + 1) * GLA_DV]
        b = _cumsum_rows(tri, gk_ref[:, kcols])
        q3 = qh.reshape(ns, t_len, GLA_DK)
        k3 = kh.reshape(ns, t_len, GLA_DK)
        b3 = b.reshape(ns, t_len, GLA_DK)
        v3 = vh.reshape(ns, t_len, GLA_DV)
        o3 = jnp.zeros((ns, t_len, GLA_DV), F32)
        for jj in range(t_len):
            p = q3 * (k3[:, jj:jj + 1, :] * jnp.exp(jnp.minimum(b3 - b3[:, jj:jj + 1, :], 0.0)))
            col = jnp.sum(p, axis=-1, keepdims=True)
            col = jnp.where(row_in_seq >= jj, col, 0.0)
            o3 = o3 + col * v3[:, jj:jj + 1, :]
        o = o3.reshape(rows, GLA_DV)

        qe = qh * jnp.exp(b)
        k_dec = (k3 * jnp.exp(b3[:, t_len - 1:t_len, :] - b3)).reshape(rows, GLA_DK)
        b_t = jnp.concatenate([b, zero_pad], axis=0).T
        vb = vh.astype(BF16)
        for s in range(ns):
            mine = seq_of_row == s
            s0 = s0_ref[s, h]
            o = o + _dot(jnp.where(mine, qe, 0.0).astype(BF16), s0.astype(BF16))
            last = s * t_len + t_len - 1
            a_col = jnp.exp(b_t[:, last:last + 1])
            upd = _dot_tn(jnp.where(mine, k_dec, 0.0).astype(BF16), vb)
            sout_ref[s, h] = a_col * s0 + upd
        cat_ref[:, vcols] = _gated_head_out(o, gt, gn_ref[...]).astype(BF16)


def _mixer_sample(main, gk, s_in, buf_in, layer, gla_norm, w_pool, pool_scale, cat, stacked, *,
                  n_seq, t_len, pos0, row0):
    ns = SEQ_PER_STEP
    rows = ns * t_len
    rb0 = row0 // rows
    n_alias = 1 + len(stacked)
    n_in = 7
    return pl.pallas_call(
        functools.partial(_mixer_sample_kernel, pos0=pos0, t_len=t_len, n_alias=n_alias),
        grid=(n_seq // ns,),
        in_specs=[
            pl.BlockSpec((rows, MAIN_WIDTH), lambda i: (rb0 + i, 0)),
            pl.BlockSpec((rows, QK_WIDTH), lambda i: (rb0 + i, 0)),
            _layer_spec((ns, GLA_HEADS, GLA_DK, GLA_DV), lambda i: (layer, i, 0, 0, 0)),
            _layer_spec((ns, POOL_BUF, POOL_WIDTH), lambda i: (layer, i, 0, 0)),
            _layer_spec((1, GLA_DV), lambda i: (layer, 0, 0)),
            _layer_spec((len(POOL_WINDOWS), POOL_GC, POOL_GC), lambda i: (layer, 0, 0, 0)),
            _layer_spec((1, POOL_WIDTH), lambda i: (layer, 0, 0)),
        ] + [pl.BlockSpec(memory_space=pl.ANY)] * n_alias,
        out_specs=[
            pl.BlockSpec((rows, D_MODEL), lambda i: (rb0 + i, 0)),
            _layer_spec((ns, GLA_HEADS, GLA_DK, GLA_DV), lambda i: (layer, i, 0, 0, 0)),
            _layer_spec((ns, POOL_BUF, POOL_WIDTH), lambda i: (layer, i, 0, 0)),
        ],
        out_shape=[
            jax.ShapeDtypeStruct(cat.shape, BF16),
            jax.ShapeDtypeStruct((DEPTH, n_seq, GLA_HEADS, GLA_DK, GLA_DV), F32),
            jax.ShapeDtypeStruct((DEPTH, n_seq, POOL_BUF, POOL_WIDTH), F32),
        ],
        scratch_shapes=[pltpu.VMEM((ns, BUF_ROWS + t_len, POOL_WIDTH), F32)],
        input_output_aliases={n_in + k: k for k in range(n_alias)},
        compiler_params=pltpu.CompilerParams(
            dimension_semantics=("arbitrary",), vmem_limit_bytes=VMEM_LIMIT_BYTES),
        name="mixer_sample",
    )(main, gk, s_in, buf_in, gla_norm, w_pool, pool_scale, cat, *stacked)


def _ffn_kernel(*refs, n_x, n_o, n_first_tiles, final_norm):
    x_refs = refs[:n_x]
    cat_ref, wout_ref, nrm_ref, wg_ref, wu_ref, wd_ref, nf_ref = refs[n_x:n_x + 7]
    o_refs = refs[n_x + 7:n_x + 7 + n_o]
    h_ref = refs[-1]
    j = pl.program_id(1)
    nj = pl.num_programs(1)

    @pl.when(j == 0)
    def _():
        if n_o == 2:
            @pl.when(pl.program_id(0) == 0)
            def _():
                o_refs[1][...] = jnp.zeros(o_refs[1].shape, F32)

        def start(x_ref):
            x2 = x_ref[...] + _dot(cat_ref[...], wout_ref[...])
            h_ref[...] = _rmsnorm(x2, nrm_ref[...]).astype(BF16)

            def init(o_ref):
                o_ref[...] = x2

            _for_row_source(o_refs, n_first_tiles, init)

        _for_row_source(x_refs, n_first_tiles, start)

    h = h_ref[...]
    gate = _dot(h, wg_ref[...])
    up = _dot(h, wu_ref[...])
    act = (gate * jax.nn.sigmoid(gate)) * up
    delta = _dot(act.astype(BF16), wd_ref[...])
    if n_o == 1:
        o_refs[0][...] += delta
    else:
        first = pl.program_id(0) < n_first_tiles
        o_refs[0][...] += jnp.where(first, delta, 0.0)
        o_refs[1][...] += jnp.where(first, 0.0, delta)

    if final_norm:
        @pl.when(j == nj - 1)
        def _():
            def finish(o_ref):
                o_ref[...] = _rmsnorm(o_ref[...], nf_ref[...])

            _for_row_source(o_refs, n_first_tiles, finish)


def _ffn(xs, cat, layer, w_out, nrm, w_gate, w_up, w_down, norm_final, *, final_norm, out_rows):
    m = sum(x.shape[0] for x in xs)
    n_first_tiles = (xs[0].shape[0] if len(xs) == 2 else out_rows[0]) // TM_FFN
    grid = (m // TM_FFN, D_FF // TF_FFN)
    if len(out_rows) == 1:
        out_specs = [pl.BlockSpec((TM_FFN, D_MODEL), lambda i, j: (i, 0))]
    else:
        last = n_first_tiles - 1
        out_specs = [
            pl.BlockSpec((TM_FFN, D_MODEL), lambda i, j: (jnp.minimum(i, last), 0)),
            pl.BlockSpec((TM_FFN, D_MODEL), lambda i, j: (jnp.maximum(i - n_first_tiles, 0), 0)),
        ]
    return pl.pallas_call(
        functools.partial(_ffn_kernel, n_x=len(xs), n_o=len(out_rows),
                          n_first_tiles=n_first_tiles, final_norm=final_norm),
        grid=grid,
        in_specs=_split_rows_specs(TM_FFN, n_first_tiles, len(xs) == 2) + [
            pl.BlockSpec((TM_FFN, D_MODEL), lambda i, j: (i, 0)),
            pl.BlockSpec((D_MODEL, D_MODEL), lambda i, j: (0, 0), pipeline_mode=pl.Buffered(1)),
            _layer_spec((1, D_MODEL), lambda i, j: (layer, 0, 0)),
            pl.BlockSpec((D_MODEL, TF_FFN), lambda i, j: (0, j)),
            pl.BlockSpec((D_MODEL, TF_FFN), lambda i, j: (0, j)),
            pl.BlockSpec((TF_FFN, D_MODEL), lambda i, j: (j, 0)),
            pl.BlockSpec((1, D_MODEL), lambda i, j: (0, 0)),
        ],
        out_specs=out_specs,
        out_shape=[jax.ShapeDtypeStruct((r, D_MODEL), F32) for r in out_rows],
        scratch_shapes=[pltpu.VMEM((TM_FFN, D_MODEL), BF16)],
        compiler_params=pltpu.CompilerParams(
            dimension_semantics=("arbitrary", "arbitrary"), vmem_limit_bytes=VMEM_LIMIT_BYTES),
        name="ffn",
    )(*xs, cat, w_out, nrm, w_gate, w_up, w_down, norm_final)


def kernel(x_prompt, x_sample, state_gla, state_pool, norm_mix, w_in, w_gk_up, b_gk, gla_norm,
           w_pool, pool_scale, w_out, norm_ffn, w_gate, w_up, w_down, norm_final):
    n_p, t_p, _ = x_prompt.shape
    n_s, t_s, _ = x_sample.shape
    m_p = n_p * t_p
    m_s = n_s * t_s

    w_in_t = jnp.swapaxes(w_in, 1, 2).astype(BF16)
    gate0 = G_OFF + GLA_WIDTH
    w_lr_t = jnp.pad(w_in_t[:, gate0:gate0 + GLA_LOWRANK],
                     ((0, 0), (0, LANES - GLA_LOWRANK), (0, 0)))
    w_upp = jnp.pad(w_gk_up, ((0, 0), (0, LANES - GLA_LOWRANK), (0, 0))).astype(BF16)
    w_pool_b = w_pool.astype(BF16)
    ffn_weights = (w_out, w_gate, w_up, w_down)
    row3 = lambda a: a[:, None, :]
    norm_mix3, norm_ffn3, b_gk3 = row3(norm_mix), row3(norm_ffn), row3(b_gk)
    gla_norm3, pool_scale3 = row3(gla_norm), row3(pool_scale)

    s0_p = jnp.zeros((n_p, GLA_HEADS, GLA_DK, GLA_DV), F32)
    buf0_p = jnp.zeros((n_p, BUF_ROWS, POOL_WIDTH), F32)

    xs = [x_prompt.reshape(m_p, D_MODEL), x_sample.reshape(m_s, D_MODEL)]
    stacked_p, stacked_s = (), ()
    for l in range(DEPTH):
        last = l == DEPTH - 1
        main, gk = _inproj(xs, l, norm_mix3, w_in_t, w_lr_t, w_upp, b_gk3)
        cat, *rest = _mixer_prompt(main, gk, s0_p, buf0_p, l, gla_norm3, w_pool_b, pool_scale3,
                                   ffn_weights, stacked_p, n_seq=n_p, seq_len=t_p, pos0=0)
        stacked_p, (w_out_b, w_gate_b, w_up_b, w_down_b) = rest[:2], rest[2:]
        cat, *stacked_s = _mixer_sample(main, gk, state_gla, state_pool, l, gla_norm3, w_pool_b,
                                        pool_scale3, cat, stacked_s, n_seq=n_s, t_len=t_s,
                                        pos0=PAST_LEN, row0=m_p)
        xs = _ffn(xs, cat, l, w_out_b, norm_ffn3, w_gate_b, w_up_b, w_down_b, norm_final[None],
                  final_norm=last, out_rows=(m_p, m_s) if last else (m_p + m_s,))

    y_prompt = xs[0].reshape(n_p, t_p, D_MODEL)
    y_sample = xs[1].reshape(n_s, t_s, D_MODEL)
    return (y_prompt, y_sample, stacked_p[0], stacked_p[1], stacked_s[0], stacked_s[1])
```

```python
import functools

import jax
import jax.numpy as jnp
from jax import lax
from jax.experimental import pallas as pl
from jax.experimental.pallas import tpu as pltpu

F32 = jnp.float32
BF16 = jnp.bfloat16

D_MODEL = 2048
DEPTH = 2
EPS = 1e-6
GLA_WIDTH = D_MODEL // 2
GLA_HEADS = 4
GLA_DV = GLA_WIDTH // GLA_HEADS
GLA_DK = GLA_DV // 2
GLA_LOWRANK = 16
GK_NORM = 16.0
GLA_CHUNK = 64
POOL_WIDTH = D_MODEL - GLA_WIDTH
POOL_WINDOWS = (2, 4, 8, 16)
POOL_GC = POOL_WIDTH // len(POOL_WINDOWS)
POOL_BUF = 15
D_FF = ((8 * D_MODEL // 3 + 255) // 256) * 256
QK_WIDTH = GLA_HEADS * GLA_DK
W_IN_GATE0 = 2 * QK_WIDTH + 2 * GLA_WIDTH
MAIN_WIDTH = W_IN_GATE0 + POOL_WIDTH
VGU_WIDTH = MAIN_WIDTH - 2 * QK_WIDTH
K_OFF = QK_WIDTH
V_OFF = 0
G_OFF = GLA_WIDTH
U_OFF = 2 * GLA_WIDTH
Q_SCALE = GLA_DK ** -0.5
PAST_LEN = 16384

LANES = 128
SUBLANES = 8
VMEM_LIMIT_BYTES = 58 * 1024 * 1024

TM_PROJ = 1024
TN_PROJ = 2 * QK_WIDTH
N_ALIGNED_TILES = W_IN_GATE0 // TN_PROJ
PROJ_K_CHUNK = 512
TM_FFN = 512
TF_FFN = 512
TT_PROMPT = 256
SUB = 16
ROW_CHUNK = 128
SEQ_PER_STEP = 8
BUF_ROWS = POOL_BUF + 1
EXT0 = BUF_ROWS + SUBLANES


def _rmsnorm(x, g):
    r = lax.rsqrt(jnp.mean(x * x, axis=-1, keepdims=True) + EPS)
    return (x * r) * g


def _dot(a, b):
    return jnp.dot(a, b, preferred_element_type=F32)


def _dot_nt(a, b):
    return lax.dot_general(a, b, (((1,), (1,)), ((), ())), preferred_element_type=F32)


def _dot_tn(a, b):
    return lax.dot_general(a, b, (((0,), (0,)), ((), ())), preferred_element_type=F32)


def _split3(x):
    hi = x.astype(BF16)
    r1 = x - hi.astype(F32)
    mid = r1.astype(BF16)
    lo = (r1 - mid.astype(F32)).astype(BF16)
    return hi, mid, lo


def _cumsum_rows(tri, g):
    hi, mid, lo = _split3(g)
    b3 = _dot(tri, jnp.concatenate([hi, mid, lo], axis=1))
    n = g.shape[1]
    return b3[:, :n] + b3[:, n:2 * n] + b3[:, 2 * n:]


def _layer_spec(block, index_map, **kw):
    return pl.BlockSpec((None,) + tuple(block), index_map, **kw)


def _split_rows_specs(tm, n_first_tiles, two_inputs):
    if not two_inputs:
        return [pl.BlockSpec((tm, D_MODEL), lambda i, j: (i, 0))]
    last = n_first_tiles - 1
    return [
        pl.BlockSpec((tm, D_MODEL), lambda i, j: (jnp.minimum(i, last), 0)),
        pl.BlockSpec((tm, D_MODEL), lambda i, j: (jnp.maximum(i - n_first_tiles, 0), 0),
                     pipeline_mode=pl.Buffered(1)),
    ]


def _for_row_source(x_refs, n_first_tiles, fn):
    if len(x_refs) == 1:
        fn(x_refs[0])
        return
    i = pl.program_id(0)
    pl.when(i < n_first_tiles)(lambda: fn(x_refs[0]))
    pl.when(i >= n_first_tiles)(lambda: fn(x_refs[1]))


def _inproj_kernel(*refs, n_x, n_first_tiles):
    x_refs = refs[:n_x]
    nrm_ref, w_ref, wlr_ref, wup_ref, bgk_ref, qk_ref, vgu_ref, gk_ref, h_ref = refs[n_x:]
    j = pl.program_id(1)

    @pl.when(j == 0)
    def _():
        def normalize(x_ref):
            def body(r, carry):
                rows = pl.ds(pl.multiple_of(r * ROW_CHUNK, ROW_CHUNK), ROW_CHUNK)
                h_ref[rows, :] = _rmsnorm(x_ref[rows, :], nrm_ref[...]).astype(BF16)
                return carry

            lax.fori_loop(0, TM_PROJ // ROW_CHUNK, body, 0)

        _for_row_source(x_refs, n_first_tiles, normalize)

        lr = _dot_nt(h_ref[...], wlr_ref[...])
        z = _dot(lr.astype(BF16), wup_ref[...]) + bgk_ref[...]
        gk_ref[...] = jax.nn.log_sigmoid(z) / GK_NORM

    kc = PROJ_K_CHUNK
    t = _dot_nt(h_ref[:, 0:kc], w_ref[0, :, 0:kc])
    for k0 in range(kc, D_MODEL, kc):
        t = t + _dot_nt(h_ref[:, k0:k0 + kc], w_ref[0, :, k0:k0 + kc])

    @pl.when(j == 0)
    def _():
        qk_ref[...] = t

    @pl.when(j > 0)
    def _():
        vgu_ref[...] = t.astype(BF16)


def _inproj(xs, layer, nrm, w_in_t, w_lr_t, w_up, b_gk):
    m = sum(x.shape[0] for x in xs)
    n_first_tiles = xs[0].shape[0] // TM_PROJ
    grid = (m // TM_PROJ, MAIN_WIDTH // TN_PROJ)

    def w_rows(j):
        start = jnp.where(j < N_ALIGNED_TILES, j * TN_PROJ, j * TN_PROJ + GLA_LOWRANK)
        return pl.multiple_of(start, GLA_LOWRANK)

    return pl.pallas_call(
        functools.partial(_inproj_kernel, n_x=len(xs), n_first_tiles=n_first_tiles),
        grid=grid,
        in_specs=_split_rows_specs(TM_PROJ, n_first_tiles, len(xs) == 2) + [
            _layer_spec((1, D_MODEL), lambda i, j: (layer, 0, 0)),
            pl.BlockSpec((pl.Element(1), pl.Element(TN_PROJ), pl.Element(D_MODEL)),
                         lambda i, j: (layer, w_rows(j), 0)),
            _layer_spec((LANES, D_MODEL), lambda i, j: (layer, 0, 0)),
            _layer_spec((LANES, QK_WIDTH), lambda i, j: (layer, 0, 0)),
            _layer_spec((1, QK_WIDTH), lambda i, j: (layer, 0, 0)),
        ],
        out_specs=[
            pl.BlockSpec((TM_PROJ, TN_PROJ), lambda i, j: (i, 0)),
            pl.BlockSpec((TM_PROJ, TN_PROJ), lambda i, j: (i, jnp.maximum(j - 1, 0))),
            pl.BlockSpec((TM_PROJ, QK_WIDTH), lambda i, j: (i, 0)),
        ],
        out_shape=[
            jax.ShapeDtypeStruct((m, TN_PROJ), F32),
            jax.ShapeDtypeStruct((m, VGU_WIDTH), BF16),
            jax.ShapeDtypeStruct((m, QK_WIDTH), F32),
        ],
        scratch_shapes=[pltpu.VMEM((TM_PROJ, D_MODEL), BF16)],
        compiler_params=pltpu.CompilerParams(
            dimension_semantics=("arbitrary", "arbitrary"), vmem_limit_bytes=VMEM_LIMIT_BYTES),
        name="inproj",
    )(*xs, nrm, w_in_t, w_lr_t, w_up, b_gk)


def _pool_output(s, u_cols, pos, gi, wpool_ref, ps_ref):
    cnt = jnp.minimum(POOL_WINDOWS[gi], pos + 1).astype(F32)
    d = s / cnt - u_cols
    cols = slice(gi * POOL_GC, (gi + 1) * POOL_GC)
    return _dot(d.astype(BF16), wpool_ref[gi]) * ps_ref[:, cols]


def _window_sums_doubling(ext_ref, lvl_ref, cols, w, n_rows):
    end = EXT0 + n_rows
    n_lvl = w.bit_length() - 1

    def read(lo, hi):
        return ext_ref[lo:hi, cols]

    for lvl in range(n_lvl):
        sh = 1 << lvl
        lo = EXT0 if lvl == n_lvl - 1 else SUBLANES
        val = read(lo, end) + read(lo - sh, end - sh)
        if lvl == n_lvl - 1:
            return val
        buf = lvl_ref.at[lvl % 2]
        buf[lo:end, :] = val

        def read(lo_, hi_, buf=buf):
            return buf[lo_:hi_, :]


def _gla_chunk(q, k, v, g, st_ref, masks):
    tri, row_id, pair_level, on_diag = masks
    c = GLA_CHUNK
    b = _cumsum_rows(tri, g)

    a = jnp.where(on_diag, jnp.sum(q * k, axis=-1, keepdims=True), 0.0)
    for lvl in range(c.bit_length() - 1):
        s = 1 << lvl
        f = jnp.exp(-jnp.abs(b - _segment_mid_rows(b, row_id, s)))
        upper = jnp.bitwise_and(row_id, s) != 0
        x = f * jnp.where(upper, q, k)
        qt = jnp.where(upper, x, 0.0).astype(BF16)
        kt = jnp.where(upper, 0.0, x).astype(BF16)
        a = jnp.where(pair_level == lvl, _dot_nt(qt, kt), a)

    vb = v.astype(BF16)
    o_intra = _dot(a.astype(BF16), vb)
    qe = (q * jnp.exp(b)).astype(BF16)
    outs = []
    for h in range(GLA_HEADS):
        sl = slice(h * c, (h + 1) * c)
        st = st_ref[h]
        outs.append(o_intra[sl] + _dot_nt(qe[sl], st.astype(BF16)))
        b_last = b[(h + 1) * c - 1:(h + 1) * c, :]
        k_dec = k[sl] * jnp.exp(b_last - b[sl])
        st_ref[h] = st * jnp.exp(b_last) + _dot_tn(vb[sl], k_dec.astype(BF16))
    return outs


def _segment_mid_rows(b, row_id, s):
    c, dk = b.shape
    seg = 2 * s
    if seg >= SUBLANES:
        b3 = b.reshape(c // seg, seg, dk)
        return jnp.broadcast_to(b3[:, s:s + 1, :], b3.shape).reshape(c, dk)
    b3 = b.reshape(c // SUBLANES, SUBLANES, dk)
    place = jnp.bitwise_and(row_id, seg - 1)
    out = b
    for p in range(seg):
        if p != s:
            below = pltpu.roll(b3, (p - s) % SUBLANES, axis=1).reshape(c, dk)
            out = jnp.where(place == p, below, out)
    return out


def _gla_masks():
    n = GLA_HEADS * GLA_CHUNK
    row = lax.broadcasted_iota(jnp.int32, (n, n), 0)
    col = lax.broadcasted_iota(jnp.int32, (n, n), 1)
    top_differing_bit = 31 - lax.clz(jnp.bitwise_xor(row, col))
    same_head = top_differing_bit < GLA_CHUNK.bit_length() - 1
    tri = jnp.logical_and(row >= col, same_head).astype(BF16)
    row_id = lax.broadcasted_iota(jnp.int32, (n, GLA_DK), 0)
    pair_level = jnp.where(col < row, top_differing_bit, -1)
    return tri, row_id, pair_level, row == col


def _gated_head_out(o, gt, gn):
    return _rmsnorm(o, gn) * (gt * jax.nn.sigmoid(gt))


def _mixer_prompt_kernel(qk_ref, vgu_ref, gk_ref, s0_ref, buf0_ref, gn_ref, wpool_ref, ps_ref, *rest,
                         pos0, n_alias, n_w):
    w32_refs = rest[:n_w]
    rest = rest[n_w + n_alias:]
    cat_ref, sout_ref, bufout_ref = rest[:3]
    w16_refs = rest[3:3 + n_w]
    st_ref, ext_ref, lvl_ref = rest[3 + n_w:]
    t = pl.program_id(1)
    nt = pl.num_programs(1)
    tt = TT_PROMPT

    for w32_ref, w16_ref in zip(w32_refs, w16_refs):
        w16_ref[...] = w32_ref[...].astype(BF16)

    @pl.when(t == 0)
    def _():
        for h in range(GLA_HEADS):
            st_ref[h] = s0_ref[0, h].T
        ext_ref[0:SUBLANES, :] = jnp.zeros((SUBLANES, POOL_WIDTH), F32)
        lvl_ref[:, 0:SUBLANES, :] = jnp.zeros((2, SUBLANES, POOL_GC), F32)
        ext_ref[SUBLANES:EXT0, :] = buf0_ref[0]

    ext_ref[EXT0:EXT0 + tt, :] = vgu_ref[:, U_OFF:U_OFF + POOL_WIDTH].astype(F32)
    pos = pos0 + t * tt + lax.broadcasted_iota(jnp.int32, (tt, POOL_GC), 0)
    for gi, w in enumerate(POOL_WINDOWS):
        cols = slice(gi * POOL_GC, (gi + 1) * POOL_GC)
        s = _window_sums_doubling(ext_ref, lvl_ref, cols, w, tt)
        y = _pool_output(s, ext_ref[EXT0:EXT0 + tt, cols], pos, gi, wpool_ref, ps_ref)
        cat_ref[:, GLA_WIDTH + gi * POOL_GC:GLA_WIDTH + (gi + 1) * POOL_GC] = y.astype(BF16)
    ext_ref[SUBLANES:EXT0, :] = ext_ref[tt + SUBLANES:tt + EXT0, :]

    masks = _gla_masks()

    def chunk_body(ci, carry_):
        rows = pl.ds(pl.multiple_of(ci * GLA_CHUNK, GLA_CHUNK), GLA_CHUNK)

        def heads_on_rows(ref, off, width):
            return jnp.concatenate(
                [ref[rows, off + h * width:off + (h + 1) * width] for h in range(GLA_HEADS)], axis=0)

        outs = _gla_chunk(heads_on_rows(qk_ref, 0, GLA_DK) * Q_SCALE,
                          heads_on_rows(qk_ref, K_OFF, GLA_DK),
                          heads_on_rows(vgu_ref, V_OFF, GLA_DV),
                          heads_on_rows(gk_ref, 0, GLA_DK), st_ref, masks)
        for h in range(GLA_HEADS):
            vcols = slice(h * GLA_DV, (h + 1) * GLA_DV)
            gt = vgu_ref[rows, G_OFF + h * GLA_DV:G_OFF + (h + 1) * GLA_DV].astype(F32)
            cat_ref[rows, vcols] = _gated_head_out(outs[h], gt, gn_ref[...]).astype(BF16)
        return carry_

    lax.fori_loop(0, tt // GLA_CHUNK, chunk_body, 0, unroll=True)

    @pl.when(t == nt - 1)
    def _():
        for h in range(GLA_HEADS):
            sout_ref[0, h] = st_ref[h].T
        bufout_ref[0] = ext_ref[EXT0 - POOL_BUF:EXT0, :]


def _mixer_prompt(qk, vgu, gk, s0, buf0, layer, gla_norm, w_pool, pool_scale, ffn_weights, stacked,
                  *, n_seq, seq_len, pos0):
    tt = TT_PROMPT
    nt = seq_len // tt
    n_steps = n_seq * nt
    row = lambda b, t: b * nt + t
    n_alias = len(stacked)
    n_in = 8 + len(ffn_weights)
    w_blocks = [(w.shape[1] // n_steps, w.shape[2]) for w in ffn_weights]
    return pl.pallas_call(
        functools.partial(_mixer_prompt_kernel, pos0=pos0, n_alias=n_alias, n_w=len(ffn_weights)),
        grid=(n_seq, nt),
        in_specs=[
            pl.BlockSpec((tt, 2 * QK_WIDTH), lambda b, t: (row(b, t), 0)),
            pl.BlockSpec((tt, VGU_WIDTH), lambda b, t: (row(b, t), 0)),
            pl.BlockSpec((tt, QK_WIDTH), lambda b, t: (row(b, t), 0)),
            pl.BlockSpec((1, GLA_HEADS, GLA_DK, GLA_DV), lambda b, t: (b, 0, 0, 0)),
            pl.BlockSpec((1, BUF_ROWS, POOL_WIDTH), lambda b, t: (b, 0, 0)),
            _layer_spec((1, GLA_DV), lambda b, t: (layer, 0, 0)),
            _layer_spec((len(POOL_WINDOWS), POOL_GC, POOL_GC), lambda b, t: (layer, 0, 0, 0)),
            _layer_spec((1, POOL_WIDTH), lambda b, t: (layer, 0, 0)),
        ] + [_layer_spec(blk, lambda b, t: (layer, row(b, t), 0)) for blk in w_blocks]
        + [pl.BlockSpec(memory_space=pl.ANY)] * n_alias,
        out_specs=[
            pl.BlockSpec((tt, D_MODEL), lambda b, t: (row(b, t), 0)),
            _layer_spec((1, GLA_HEADS, GLA_DK, GLA_DV), lambda b, t: (layer, b, 0, 0, 0)),
            _layer_spec((1, POOL_BUF, POOL_WIDTH), lambda b, t: (layer, b, 0, 0)),
        ] + [pl.BlockSpec(blk, lambda b, t: (row(b, t), 0)) for blk in w_blocks],
        out_shape=[
            jax.ShapeDtypeStruct((qk.shape[0], D_MODEL), BF16),
            jax.ShapeDtypeStruct((DEPTH, n_seq, GLA_HEADS, GLA_DK, GLA_DV), F32),
            jax.ShapeDtypeStruct((DEPTH, n_seq, POOL_BUF, POOL_WIDTH), F32),
        ] + [jax.ShapeDtypeStruct(w.shape[1:], BF16) for w in ffn_weights],
        scratch_shapes=[
            pltpu.VMEM((GLA_HEADS, GLA_DV, GLA_DK), F32),
            pltpu.VMEM((EXT0 + tt, POOL_WIDTH), F32),
            pltpu.VMEM((2, EXT0 + tt, POOL_GC), F32),
        ],
        input_output_aliases={n_in + k: 1 + k for k in range(n_alias)},
        compiler_params=pltpu.CompilerParams(
            dimension_semantics=("arbitrary", "arbitrary"), vmem_limit_bytes=VMEM_LIMIT_BYTES),
        name="mixer_prompt",
    )(qk, vgu, gk, s0, buf0, gla_norm, w_pool, pool_scale, *ffn_weights, *stacked)


def _mixer_sample_kernel(qk_ref, vgu_ref, gk_ref, s0_ref, buf0_ref, gn_ref, wpool_ref, ps_ref, *rest,
                         pos0, t_len, n_alias):
    cat_ref, sout_ref, bufout_ref, ext_ref = rest[n_alias:]
    ns = SEQ_PER_STEP
    rows = ns * t_len

    ext_ref[:, 1:BUF_ROWS, :] = buf0_ref[...]
    ext_ref[:, BUF_ROWS:BUF_ROWS + t_len, :] = (
        vgu_ref[:, U_OFF:U_OFF + POOL_WIDTH].astype(F32).reshape(ns, t_len, POOL_WIDTH))
    pos = pos0 + lax.broadcasted_iota(jnp.int32, (ns, t_len, POOL_GC), 1)
    for gi in range(len(POOL_WINDOWS)):
        cols = slice(gi * POOL_GC, (gi + 1) * POOL_GC)
        u_cols = ext_ref[:, BUF_ROWS:BUF_ROWS + t_len, cols]
        w = POOL_WINDOWS[gi]
        s = u_cols
        for sft in range(1, w):
            s = s + ext_ref[:, BUF_ROWS - sft:BUF_ROWS - sft + t_len, cols]
        cnt = jnp.minimum(w, pos + 1).astype(F32)
        d = (s / cnt - u_cols).reshape(rows, POOL_GC)
        y = _dot(d.astype(BF16), wpool_ref[gi]) * ps_ref[:, cols]
        cat_ref[:, GLA_WIDTH + gi * POOL_GC:GLA_WIDTH + (gi + 1) * POOL_GC] = y.astype(BF16)
    bufout_ref[...] = ext_ref[:, t_len + 1:t_len + BUF_ROWS, :]

    r_i = lax.broadcasted_iota(jnp.int32, (rows, rows), 0)
    c_i = lax.broadcasted_iota(jnp.int32, (rows, rows), 1)
    same_seq = (r_i - jnp.bitwise_and(r_i, t_len - 1)) == (c_i - jnp.bitwise_and(c_i, t_len - 1))
    tri = jnp.logical_and(r_i >= c_i, same_seq).astype(BF16)
    row_in_seq = lax.broadcasted_iota(jnp.int32, (ns, t_len, 1), 1)
    seq_of_row = lax.broadcasted_iota(jnp.int32, (rows, 1), 0) // t_len
    zero_pad = jnp.zeros((LANES - rows, GLA_DK), F32)

    for h in range(GLA_HEADS):
        kcols = slice(h * GLA_DK, (h + 1) * GLA_DK)
        vcols = slice(h * GLA_DV, (h + 1) * GLA_DV)
        qh = qk_ref[:, h * GLA_DK:(h + 1) * GLA_DK] * Q_SCALE
        kh = qk_ref[:, K_OFF + h * GLA_DK:K_OFF + (h + 1) * GLA_DK]
        vb = vgu_ref[:, V_OFF + h * GLA_DV:V_OFF + (h + 1) * GLA_DV]
        vh = vb.astype(F32)
        gt = vgu_ref[:, G_OFF + h * GLA_DV:G_OFF + (h + 1) * GLA_DV].astype(F32)
        b = _cumsum_rows(tri, gk_ref[:, kcols])
        q3 = qh.reshape(ns, t_len, GLA_DK)
        k3 = kh.reshape(ns, t_len, GLA_DK)
        b3 = b.reshape(ns, t_len, GLA_DK)
        v3 = vh.reshape(ns, t_len, GLA_DV)
        o3 = jnp.zeros((ns, t_len, GLA_DV), F32)
        for jj in range(t_len):
            p = q3 * (k3[:, jj:jj + 1, :] * jnp.exp(jnp.minimum(b3 - b3[:, jj:jj + 1, :], 0.0)))
            col = jnp.sum(p, axis=-1, keepdims=True)
            col = jnp.where(row_in_seq >= jj, col, 0.0)
            o3 = o3 + col * v3[:, jj:jj + 1, :]
        o = o3.reshape(rows, GLA_DV)

        qe = qh * jnp.exp(b)
        k_dec = (k3 * jnp.exp(b3[:, t_len - 1:t_len, :] - b3)).reshape(rows, GLA_DK)
        b_t = jnp.concatenate([b, zero_pad], axis=0).T
        for s in range(ns):
            mine = seq_of_row == s
            s0 = s0_ref[s, h]
            o = o + _dot(jnp.where(mine, qe, 0.0).astype(BF16), s0.astype(BF16))
            last = s * t_len + t_len - 1
            a_col = jnp.exp(b_t[:, last:last + 1])
            upd = _dot_tn(jnp.where(mine, k_dec, 0.0).astype(BF16), vb)
            sout_ref[s, h] = a_col * s0 + upd
        cat_ref[:, vcols] = _gated_head_out(o, gt, gn_ref[...]).astype(BF16)


def _mixer_sample(qk, vgu, gk, s_in, buf_in, layer, gla_norm, w_pool, pool_scale, cat, stacked, *,
                  n_seq, t_len, pos0, row0):
    ns = SEQ_PER_STEP
    rows = ns * t_len
    rb0 = row0 // rows
    n_alias = 1 + len(stacked)
    n_in = 8
    return pl.pallas_call(
        functools.partial(_mixer_sample_kernel, pos0=pos0, t_len=t_len, n_alias=n_alias),
        grid=(n_seq // ns,),
        in_specs=[
            pl.BlockSpec((rows, 2 * QK_WIDTH), lambda i: (rb0 + i, 0)),
            pl.BlockSpec((rows, VGU_WIDTH), lambda i: (rb0 + i, 0)),
            pl.BlockSpec((rows, QK_WIDTH), lambda i: (rb0 + i, 0)),
            _layer_spec((ns, GLA_HEADS, GLA_DK, GLA_DV), lambda i: (layer, i, 0, 0, 0)),
            _layer_spec((ns, POOL_BUF, POOL_WIDTH), lambda i: (layer, i, 0, 0)),
            _layer_spec((1, GLA_DV), lambda i: (layer, 0, 0)),
            _layer_spec((len(POOL_WINDOWS), POOL_GC, POOL_GC), lambda i: (layer, 0, 0, 0)),
            _layer_spec((1, POOL_WIDTH), lambda i: (layer, 0, 0)),
        ] + [pl.BlockSpec(memory_space=pl.ANY)] * n_alias,
        out_specs=[
            pl.BlockSpec((rows, D_MODEL), lambda i: (rb0 + i, 0)),
            _layer_spec((ns, GLA_HEADS, GLA_DK, GLA_DV), lambda i: (layer, i, 0, 0, 0)),
            _layer_spec((ns, POOL_BUF, POOL_WIDTH), lambda i: (layer, i, 0, 0)),
        ],
        out_shape=[
            jax.ShapeDtypeStruct(cat.shape, BF16),
            jax.ShapeDtypeStruct((DEPTH, n_seq, GLA_HEADS, GLA_DK, GLA_DV), F32),
            jax.ShapeDtypeStruct((DEPTH, n_seq, POOL_BUF, POOL_WIDTH), F32),
        ],
        scratch_shapes=[pltpu.VMEM((ns, BUF_ROWS + t_len, POOL_WIDTH), F32)],
        input_output_aliases={n_in + k: k for k in range(n_alias)},
        compiler_params=pltpu.CompilerParams(
            dimension_semantics=("arbitrary",), vmem_limit_bytes=VMEM_LIMIT_BYTES),
        name="mixer_sample",
    )(qk, vgu, gk, s_in, buf_in, gla_norm, w_pool, pool_scale, cat, *stacked)


def _ffn_kernel(*refs, n_x, n_o, n_side, n_first_tiles, final_norm):
    x_refs = refs[:n_x]
    cat_ref, wout_ref, nrm_ref, wg_ref, wu_ref, wd_ref, nf_ref = refs[n_x:n_x + 7]
    w32_refs = refs[n_x + 7:n_x + 7 + n_side]
    o_refs = refs[n_x + 7 + n_side:n_x + 7 + n_side + n_o]
    w16_refs = refs[n_x + 7 + n_side + n_o:n_x + 7 + 2 * n_side + n_o]
    h_ref = refs[-1]
    j = pl.program_id(1)
    nj = pl.num_programs(1)

    for w32_ref, w16_ref in zip(w32_refs, w16_refs):
        w16_ref[...] = w32_ref[...].astype(BF16)

    @pl.when(j == 0)
    def _():
        if n_o == 2:
            @pl.when(pl.program_id(0) == 0)
            def _():
                o_refs[1][...] = jnp.zeros(o_refs[1].shape, F32)

        def start(x_ref):
            x2 = x_ref[...] + _dot(cat_ref[...], wout_ref[...])
            h_ref[...] = _rmsnorm(x2, nrm_ref[...]).astype(BF16)

            def init(o_ref):
                o_ref[...] = x2

            _for_row_source(o_refs, n_first_tiles, init)

        _for_row_source(x_refs, n_first_tiles, start)

    h = h_ref[...]
    gate = _dot(h, wg_ref[...])
    up = _dot(h, wu_ref[...])
    act = (gate * jax.nn.sigmoid(gate)) * up
    delta = _dot(act.astype(BF16), wd_ref[...])
    if n_o == 1:
        o_refs[0][...] += delta
    else:
        first = pl.program_id(0) < n_first_tiles
        o_refs[0][...] += jnp.where(first, delta, 0.0)
        o_refs[1][...] += jnp.where(first, 0.0, delta)

    if final_norm:
        @pl.when(j == nj - 1)
        def _():
            def finish(o_ref):
                o_ref[...] = _rmsnorm(o_ref[...], nf_ref[...])

            _for_row_source(o_refs, n_first_tiles, finish)


def _side_cast_blocks(weights, n_grid_steps):
    blocks = []
    for w in weights:
        rows = 2 * SUBLANES
        while w.shape[1] % rows or w.shape[1] // rows > n_grid_steps:
            rows += 2 * SUBLANES
        blocks.append(rows)
    return blocks


def _ffn(xs, cat, layer, w_out, nrm, w_gate, w_up, w_down, norm_final, side_weights, *, final_norm,
         out_rows):
    m = sum(x.shape[0] for x in xs)
    n_first_tiles = (xs[0].shape[0] if len(xs) == 2 else out_rows[0]) // TM_FFN
    grid = (m // TM_FFN, D_FF // TF_FFN)
    nj = grid[1]
    side_rows = _side_cast_blocks(side_weights, grid[0] * nj)

    def side_block(w, rows):
        return lambda i, j: jnp.minimum(i * nj + j, w.shape[1] // rows - 1)

    side_in = [_layer_spec((r, w.shape[2]), lambda i, j, f=side_block(w, r): (layer + 1, f(i, j), 0))
               for w, r in zip(side_weights, side_rows)]
    side_out = [pl.BlockSpec((r, w.shape[2]), lambda i, j, f=side_block(w, r): (f(i, j), 0))
                for w, r in zip(side_weights, side_rows)]
    if len(out_rows) == 1:
        out_specs = [pl.BlockSpec((TM_FFN, D_MODEL), lambda i, j: (i, 0))]
    else:
        last = n_first_tiles - 1
        out_specs = [
            pl.BlockSpec((TM_FFN, D_MODEL), lambda i, j: (jnp.minimum(i, last), 0)),
            pl.BlockSpec((TM_FFN, D_MODEL), lambda i, j: (jnp.maximum(i - n_first_tiles, 0), 0)),
        ]
    return pl.pallas_call(
        functools.partial(_ffn_kernel, n_x=len(xs), n_o=len(out_rows), n_side=len(side_weights),
                          n_first_tiles=n_first_tiles, final_norm=final_norm),
        grid=grid,
        in_specs=_split_rows_specs(TM_FFN, n_first_tiles, len(xs) == 2) + [
            pl.BlockSpec((TM_FFN, D_MODEL), lambda i, j: (i, 0)),
            pl.BlockSpec((D_MODEL, D_MODEL), lambda i, j: (0, 0), pipeline_mode=pl.Buffered(1)),
            _layer_spec((1, D_MODEL), lambda i, j: (layer, 0, 0)),
            pl.BlockSpec((D_MODEL, TF_FFN), lambda i, j: (0, j)),
            pl.BlockSpec((D_MODEL, TF_FFN), lambda i, j: (0, j)),
            pl.BlockSpec((TF_FFN, D_MODEL), lambda i, j: (j, 0)),
            pl.BlockSpec((1, D_MODEL), lambda i, j: (0, 0)),
        ] + side_in,
        out_specs=out_specs + side_out,
        out_shape=[jax.ShapeDtypeStruct((r, D_MODEL), F32) for r in out_rows]
        + [jax.ShapeDtypeStruct(w.shape[1:], BF16) for w in side_weights],
        scratch_shapes=[pltpu.VMEM((TM_FFN, D_MODEL), BF16)],
        compiler_params=pltpu.CompilerParams(
            dimension_semantics=("arbitrary", "arbitrary"), vmem_limit_bytes=VMEM_LIMIT_BYTES),
        name="ffn",
    )(*xs, cat, w_out, nrm, w_gate, w_up, w_down, norm_final, *side_weights)


def kernel(x_prompt, x_sample, state_gla, state_pool, norm_mix, w_in, w_gk_up, b_gk, gla_norm,
           w_pool, pool_scale, w_out, norm_ffn, w_gate, w_up, w_down, norm_final):
    n_p, t_p, _ = x_prompt.shape
    n_s, t_s, _ = x_sample.shape
    m_p = n_p * t_p
    m_s = n_s * t_s

    w_in_t = jnp.swapaxes(w_in, 1, 2).astype(BF16)
    gate0 = W_IN_GATE0
    w_lr_t = jnp.pad(w_in_t[:, gate0:gate0 + GLA_LOWRANK],
                     ((0, 0), (0, LANES - GLA_LOWRANK), (0, 0)))
    w_upp = jnp.pad(w_gk_up, ((0, 0), (0, LANES - GLA_LOWRANK), (0, 0))).astype(BF16)
    w_pool_b = w_pool.astype(BF16)
    ffn_weights = (w_out, w_gate, w_up, w_down)
    row3 = lambda a: a[:, None, :]
    norm_mix3, norm_ffn3, b_gk3 = row3(norm_mix), row3(norm_ffn), row3(b_gk)
    gla_norm3, pool_scale3 = row3(gla_norm), row3(pool_scale)

    s0_p = jnp.zeros((n_p, GLA_HEADS, GLA_DK, GLA_DV), F32)
    buf0_p = jnp.zeros((n_p, BUF_ROWS, POOL_WIDTH), F32)

    xs = [x_prompt.reshape(m_p, D_MODEL), x_sample.reshape(m_s, D_MODEL)]
    stacked_p, stacked_s = (), ()
    ffn_weights_b = None
    for l in range(DEPTH):
        last = l == DEPTH - 1
        qk, vgu, gk = _inproj(xs, l, norm_mix3, w_in_t, w_lr_t, w_upp, b_gk3)
        to_cast = () if ffn_weights_b else ffn_weights
        cat, *rest = _mixer_prompt(qk, vgu, gk, s0_p, buf0_p, l, gla_norm3, w_pool_b, pool_scale3,
                                   to_cast, stacked_p, n_seq=n_p, seq_len=t_p, pos0=0)
        stacked_p, ffn_weights_b = rest[:2], ffn_weights_b or rest[2:]
        cat, *stacked_s = _mixer_sample(qk, vgu, gk, state_gla, state_pool, l, gla_norm3, w_pool_b,
                                        pool_scale3, cat, stacked_s, n_seq=n_s, t_len=t_s,
                                        pos0=PAST_LEN, row0=m_p)
        n_out = 2 if last else 1
        outs = _ffn(xs, cat, l, ffn_weights_b[0], norm_ffn3, *ffn_weights_b[1:], norm_final[None],
                    () if last else ffn_weights, final_norm=last,
                    out_rows=(m_p, m_s) if last else (m_p + m_s,))
        xs, ffn_weights_b = outs[:n_out], outs[n_out:]

    y_prompt = xs[0].reshape(n_p, t_p, D_MODEL)
    y_sample = xs[1].reshape(n_s, t_s, D_MODEL)
    return (y_prompt, y_sample, stacked_p[0], stacked_p[1], stacked_s[0], stacked_s[1])
```

```python
import functools

import jax
import jax.numpy as jnp
from jax import lax
from jax.experimental import pallas as pl
from jax.experimental.pallas import tpu as pltpu

F32 = jnp.float32
BF16 = jnp.bfloat16

D_MODEL = 2048
DEPTH = 2
EPS = 1e-6
GLA_WIDTH = D_MODEL // 2
GLA_HEADS = 4
GLA_DV = GLA_WIDTH // GLA_HEADS
GLA_DK = GLA_DV // 2
GLA_LOWRANK = 16
GK_NORM = 16.0
GLA_CHUNK = 64
POOL_WIDTH = D_MODEL - GLA_WIDTH
POOL_WINDOWS = (2, 4, 8, 16)
POOL_GC = POOL_WIDTH // len(POOL_WINDOWS)
POOL_BUF = 15
D_FF = ((8 * D_MODEL // 3 + 255) // 256) * 256
QK_WIDTH = GLA_HEADS * GLA_DK
W_IN_GATE0 = 2 * QK_WIDTH + 2 * GLA_WIDTH
MAIN_WIDTH = W_IN_GATE0 + POOL_WIDTH
VGU_WIDTH = MAIN_WIDTH - 2 * QK_WIDTH
K_OFF = QK_WIDTH
V_OFF = 0
G_OFF = GLA_WIDTH
U_OFF = 2 * GLA_WIDTH
Q_SCALE = GLA_DK ** -0.5
PAST_LEN = 16384

LANES = 128
SUBLANES = 8
VMEM_LIMIT_BYTES = 58 * 1024 * 1024

TM_PROJ = 1024
TN_PROJ = 2 * QK_WIDTH
N_ALIGNED_TILES = W_IN_GATE0 // TN_PROJ
PROJ_K_CHUNK = 512
TM_FFN = 512
TF_FFN = 512
TT_PROMPT = 256
SUB = 16
ROW_CHUNK = 128
SEQ_PER_STEP = 8
BUF_ROWS = POOL_BUF + 1
EXT0 = BUF_ROWS + SUBLANES


def _rmsnorm(x, g):
    r = lax.rsqrt(jnp.mean(x * x, axis=-1, keepdims=True) + EPS)
    return (x * r) * g


def _dot(a, b):
    return jnp.dot(a, b, preferred_element_type=F32)


def _dot_nt(a, b):
    return lax.dot_general(a, b, (((1,), (1,)), ((), ())), preferred_element_type=F32)


def _dot_tn(a, b):
    return lax.dot_general(a, b, (((0,), (0,)), ((), ())), preferred_element_type=F32)


def _split3(x):
    hi = x.astype(BF16)
    r1 = x - hi.astype(F32)
    mid = r1.astype(BF16)
    lo = (r1 - mid.astype(F32)).astype(BF16)
    return hi, mid, lo


def _cumsum_rows(tri, g):
    hi, mid, lo = _split3(g)
    b3 = _dot(tri, jnp.concatenate([hi, mid, lo], axis=1))
    n = g.shape[1]
    return b3[:, :n] + b3[:, n:2 * n] + b3[:, 2 * n:]


def _layer_spec(block, index_map, **kw):
    return pl.BlockSpec((None,) + tuple(block), index_map, **kw)


def _split_rows_specs(tm, n_first_tiles, two_inputs):
    if not two_inputs:
        return [pl.BlockSpec((tm, D_MODEL), lambda i, j: (i, 0))]
    last = n_first_tiles - 1
    return [
        pl.BlockSpec((tm, D_MODEL), lambda i, j: (jnp.minimum(i, last), 0)),
        pl.BlockSpec((tm, D_MODEL), lambda i, j: (jnp.maximum(i - n_first_tiles, 0), 0),
                     pipeline_mode=pl.Buffered(1)),
    ]


def _for_row_source(x_refs, n_first_tiles, fn):
    if len(x_refs) == 1:
        fn(x_refs[0])
        return
    i = pl.program_id(0)
    pl.when(i < n_first_tiles)(lambda: fn(x_refs[0]))
    pl.when(i >= n_first_tiles)(lambda: fn(x_refs[1]))


def _inproj_kernel(*refs, n_x, n_first_tiles):
    x_refs = refs[:n_x]
    nrm_ref, w_ref, wlr_ref, wup_ref, bgk_ref, qk_ref, vgu_ref, gk_ref, h_ref = refs[n_x:]
    j = pl.program_id(1)

    @pl.when(j == 0)
    def _():
        def normalize(x_ref):
            def body(r, carry):
                rows = pl.ds(pl.multiple_of(r * ROW_CHUNK, ROW_CHUNK), ROW_CHUNK)
                h_ref[rows, :] = _rmsnorm(x_ref[rows, :], nrm_ref[...]).astype(BF16)
                return carry

            lax.fori_loop(0, TM_PROJ // ROW_CHUNK, body, 0)

        _for_row_source(x_refs, n_first_tiles, normalize)

        lr = _dot_nt(h_ref[...], wlr_ref[...])
        z = _dot(lr.astype(BF16), wup_ref[...]) + bgk_ref[...]
        gk_ref[...] = jax.nn.log_sigmoid(z) / GK_NORM

    kc = PROJ_K_CHUNK
    t = _dot_nt(h_ref[:, 0:kc], w_ref[0, :, 0:kc])
    for k0 in range(kc, D_MODEL, kc):
        t = t + _dot_nt(h_ref[:, k0:k0 + kc], w_ref[0, :, k0:k0 + kc])

    @pl.when(j == 0)
    def _():
        qk_ref[...] = t

    @pl.when(j > 0)
    def _():
        vgu_ref[...] = t.astype(BF16)


def _inproj(xs, layer, nrm, w_in_t, w_lr_t, w_up, b_gk):
    m = sum(x.shape[0] for x in xs)
    n_first_tiles = xs[0].shape[0] // TM_PROJ
    grid = (m // TM_PROJ, MAIN_WIDTH // TN_PROJ)

    def w_rows(j):
        start = jnp.where(j < N_ALIGNED_TILES, j * TN_PROJ, j * TN_PROJ + GLA_LOWRANK)
        return pl.multiple_of(start, GLA_LOWRANK)

    return pl.pallas_call(
        functools.partial(_inproj_kernel, n_x=len(xs), n_first_tiles=n_first_tiles),
        grid=grid,
        in_specs=_split_rows_specs(TM_PROJ, n_first_tiles, len(xs) == 2) + [
            _layer_spec((1, D_MODEL), lambda i, j: (layer, 0, 0)),
            pl.BlockSpec((pl.Element(1), pl.Element(TN_PROJ), pl.Element(D_MODEL)),
                         lambda i, j: (layer, w_rows(j), 0)),
            _layer_spec((LANES, D_MODEL), lambda i, j: (layer, 0, 0)),
            _layer_spec((LANES, QK_WIDTH), lambda i, j: (layer, 0, 0)),
            _layer_spec((1, QK_WIDTH), lambda i, j: (layer, 0, 0)),
        ],
        out_specs=[
            pl.BlockSpec((TM_PROJ, TN_PROJ), lambda i, j: (i, 0)),
            pl.BlockSpec((TM_PROJ, TN_PROJ), lambda i, j: (i, jnp.maximum(j - 1, 0))),
            pl.BlockSpec((TM_PROJ, QK_WIDTH), lambda i, j: (i, 0)),
        ],
        out_shape=[
            jax.ShapeDtypeStruct((m, TN_PROJ), F32),
            jax.ShapeDtypeStruct((m, VGU_WIDTH), BF16),
            jax.ShapeDtypeStruct((m, QK_WIDTH), F32),
        ],
        scratch_shapes=[pltpu.VMEM((TM_PROJ, D_MODEL), BF16)],
        compiler_params=pltpu.CompilerParams(
            dimension_semantics=("arbitrary", "arbitrary"), vmem_limit_bytes=VMEM_LIMIT_BYTES),
        name="inproj",
    )(*xs, nrm, w_in_t, w_lr_t, w_up, b_gk)


def _pool_output(s, u_cols, pos, gi, wpool_ref, ps_ref):
    cnt = jnp.minimum(POOL_WINDOWS[gi], pos + 1).astype(F32)
    d = s / cnt - u_cols
    cols = slice(gi * POOL_GC, (gi + 1) * POOL_GC)
    return _dot(d.astype(BF16), wpool_ref[gi]) * ps_ref[:, cols]


def _window_sums_doubling(ext_ref, lvl_ref, cols, w, n_rows):
    end = EXT0 + n_rows
    n_lvl = w.bit_length() - 1

    def read(lo, hi):
        return ext_ref[lo:hi, cols]

    for lvl in range(n_lvl):
        sh = 1 << lvl
        lo = EXT0 if lvl == n_lvl - 1 else SUBLANES
        val = read(lo, end) + read(lo - sh, end - sh)
        if lvl == n_lvl - 1:
            return val
        buf = lvl_ref.at[lvl % 2]
        buf[lo:end, :] = val

        def read(lo_, hi_, buf=buf):
            return buf[lo_:hi_, :]


def _gla_chunk(q, k, v, g, st_ref, masks):
    tri, row_id, pair_level, on_diag = masks
    c = GLA_CHUNK
    b = _cumsum_rows(tri, g)

    a = jnp.where(on_diag, jnp.sum(q * k, axis=-1, keepdims=True), 0.0)
    for lvl in range(c.bit_length() - 1):
        s = 1 << lvl
        f = jnp.exp(-jnp.abs(b - _segment_mid_rows(b, row_id, s)))
        upper = jnp.bitwise_and(row_id, s) != 0
        x = f * jnp.where(upper, q, k)
        qt = jnp.where(upper, x, 0.0).astype(BF16)
        kt = jnp.where(upper, 0.0, x).astype(BF16)
        a = jnp.where(pair_level == lvl, _dot_nt(qt, kt), a)

    vb = v.astype(BF16)
    o_intra = _dot(a.astype(BF16), vb)
    qe = (q * jnp.exp(b)).astype(BF16)
    outs = []
    for h in range(GLA_HEADS):
        sl = slice(h * c, (h + 1) * c)
        st = st_ref[h]
        outs.append(o_intra[sl] + _dot_nt(qe[sl], st.astype(BF16)))
        b_last = b[(h + 1) * c - 1:(h + 1) * c, :]
        k_dec = k[sl] * jnp.exp(b_last - b[sl])
        st_ref[h] = st * jnp.exp(b_last) + _dot_tn(vb[sl], k_dec.astype(BF16))
    return outs


def _segment_mid_rows(b, row_id, s):
    c, dk = b.shape
    seg = 2 * s
    if seg >= SUBLANES:
        b3 = b.reshape(c // seg, seg, dk)
        return jnp.broadcast_to(b3[:, s:s + 1, :], b3.shape).reshape(c, dk)
    b3 = b.reshape(c // SUBLANES, SUBLANES, dk)
    place = jnp.bitwise_and(row_id, seg - 1)
    out = b
    for p in range(seg):
        if p != s:
            below = pltpu.roll(b3, (p - s) % SUBLANES, axis=1).reshape(c, dk)
            out = jnp.where(place == p, below, out)
    return out


def _gla_masks():
    n = GLA_HEADS * GLA_CHUNK
    row = lax.broadcasted_iota(jnp.int32, (n, n), 0)
    col = lax.broadcasted_iota(jnp.int32, (n, n), 1)
    top_differing_bit = 31 - lax.clz(jnp.bitwise_xor(row, col))
    same_head = top_differing_bit < GLA_CHUNK.bit_length() - 1
    tri = jnp.logical_and(row >= col, same_head).astype(BF16)
    row_id = lax.broadcasted_iota(jnp.int32, (n, GLA_DK), 0)
    pair_level = jnp.where(col < row, top_differing_bit, -1)
    return tri, row_id, pair_level, row == col


def _gated_head_out(o, gt, gn):
    return _rmsnorm(o, gn) * (gt * jax.nn.sigmoid(gt))


def _mixer_prompt_kernel(qk_ref, vgu_ref, gk_ref, s0_ref, buf0_ref, gn_ref, wpool_ref, ps_ref, *rest,
                         pos0, n_alias, n_w):
    w32_refs = rest[:n_w]
    rest = rest[n_w + n_alias:]
    cat_ref, sout_ref, bufout_ref = rest[:3]
    w16_refs = rest[3:3 + n_w]
    st_ref, ext_ref, lvl_ref = rest[3 + n_w:]
    t = pl.program_id(1)
    nt = pl.num_programs(1)
    tt = TT_PROMPT

    for w32_ref, w16_ref in zip(w32_refs, w16_refs):
        if len(w16_ref.shape) == 2:
            w16_ref[...] = w32_ref[...].astype(BF16)
        else:
            width = w16_ref.shape[2]
            for ct in range(w16_ref.shape[0]):
                w16_ref[ct] = w32_ref[:, ct * width:(ct + 1) * width].astype(BF16)

    @pl.when(t == 0)
    def _():
        for h in range(GLA_HEADS):
            st_ref[h] = s0_ref[0, h].T
        ext_ref[0:SUBLANES, :] = jnp.zeros((SUBLANES, POOL_WIDTH), F32)
        lvl_ref[:, 0:SUBLANES, :] = jnp.zeros((2, SUBLANES, POOL_GC), F32)
        ext_ref[SUBLANES:EXT0, :] = buf0_ref[0]

    ext_ref[EXT0:EXT0 + tt, :] = vgu_ref[:, U_OFF:U_OFF + POOL_WIDTH].astype(F32)
    pos = pos0 + t * tt + lax.broadcasted_iota(jnp.int32, (tt, POOL_GC), 0)
    for gi, w in enumerate(POOL_WINDOWS):
        cols = slice(gi * POOL_GC, (gi + 1) * POOL_GC)
        s = _window_sums_doubling(ext_ref, lvl_ref, cols, w, tt)
        y = _pool_output(s, ext_ref[EXT0:EXT0 + tt, cols], pos, gi, wpool_ref, ps_ref)
        cat_ref[:, GLA_WIDTH + gi * POOL_GC:GLA_WIDTH + (gi + 1) * POOL_GC] = y.astype(BF16)
    ext_ref[SUBLANES:EXT0, :] = ext_ref[tt + SUBLANES:tt + EXT0, :]

    masks = _gla_masks()

    def chunk_body(ci, carry_):
        rows = pl.ds(pl.multiple_of(ci * GLA_CHUNK, GLA_CHUNK), GLA_CHUNK)

        def heads_on_rows(ref, off, width):
            return jnp.concatenate(
                [ref[rows, off + h * width:off + (h + 1) * width] for h in range(GLA_HEADS)], axis=0)

        outs = _gla_chunk(heads_on_rows(qk_ref, 0, GLA_DK) * Q_SCALE,
                          heads_on_rows(qk_ref, K_OFF, GLA_DK),
                          heads_on_rows(vgu_ref, V_OFF, GLA_DV),
                          heads_on_rows(gk_ref, 0, GLA_DK), st_ref, masks)
        for h in range(GLA_HEADS):
            vcols = slice(h * GLA_DV, (h + 1) * GLA_DV)
            gt = vgu_ref[rows, G_OFF + h * GLA_DV:G_OFF + (h + 1) * GLA_DV].astype(F32)
            cat_ref[rows, vcols] = _gated_head_out(outs[h], gt, gn_ref[...]).astype(BF16)
        return carry_

    lax.fori_loop(0, tt // GLA_CHUNK, chunk_body, 0, unroll=True)

    @pl.when(t == nt - 1)
    def _():
        for h in range(GLA_HEADS):
            sout_ref[0, h] = st_ref[h].T
        bufout_ref[0] = ext_ref[EXT0 - POOL_BUF:EXT0, :]


def _mixer_prompt(qk, vgu, gk, s0, buf0, layer, gla_norm, w_pool, pool_scale, ffn_weights, col_tiles,
                  stacked, *, n_seq, seq_len, pos0):
    tt = TT_PROMPT
    nt = seq_len // tt
    n_steps = n_seq * nt
    row = lambda b, t: b * nt + t
    n_alias = len(stacked)
    n_in = 8 + len(ffn_weights)
    w_blocks = [(w.shape[1] // n_steps, w.shape[2]) for w in ffn_weights]
    w_out_specs, w_out_shapes = [], []
    for w, (rows, cols), ct in zip(ffn_weights, w_blocks, col_tiles):
        if ct is None:
            w_out_specs.append(pl.BlockSpec((rows, cols), lambda b, t: (row(b, t), 0)))
            w_out_shapes.append(jax.ShapeDtypeStruct(w.shape[1:], BF16))
        else:
            w_out_specs.append(pl.BlockSpec((cols // ct, rows, ct), lambda b, t: (0, row(b, t), 0)))
            w_out_shapes.append(jax.ShapeDtypeStruct((cols // ct, w.shape[1], ct), BF16))
    return pl.pallas_call(
        functools.partial(_mixer_prompt_kernel, pos0=pos0, n_alias=n_alias, n_w=len(ffn_weights)),
        grid=(n_seq, nt),
        in_specs=[
            pl.BlockSpec((tt, 2 * QK_WIDTH), lambda b, t: (row(b, t), 0)),
            pl.BlockSpec((tt, VGU_WIDTH), lambda b, t: (row(b, t), 0)),
            pl.BlockSpec((tt, QK_WIDTH), lambda b, t: (row(b, t), 0)),
            pl.BlockSpec((1, GLA_HEADS, GLA_DK, GLA_DV), lambda b, t: (b, 0, 0, 0)),
            pl.BlockSpec((1, BUF_ROWS, POOL_WIDTH), lambda b, t: (b, 0, 0)),
            _layer_spec((1, GLA_DV), lambda b, t: (layer, 0, 0)),
            _layer_spec((len(POOL_WINDOWS), POOL_GC, POOL_GC), lambda b, t: (layer, 0, 0, 0)),
            _layer_spec((1, POOL_WIDTH), lambda b, t: (layer, 0, 0)),
        ] + [_layer_spec(blk, lambda b, t: (layer, row(b, t), 0)) for blk in w_blocks]
        + [pl.BlockSpec(memory_space=pl.ANY)] * n_alias,
        out_specs=[
            pl.BlockSpec((tt, D_MODEL), lambda b, t: (row(b, t), 0)),
            _layer_spec((1, GLA_HEADS, GLA_DK, GLA_DV), lambda b, t: (layer, b, 0, 0, 0)),
            _layer_spec((1, POOL_BUF, POOL_WIDTH), lambda b, t: (layer, b, 0, 0)),
        ] + w_out_specs,
        out_shape=[
            jax.ShapeDtypeStruct((qk.shape[0], D_MODEL), BF16),
            jax.ShapeDtypeStruct((DEPTH, n_seq, GLA_HEADS, GLA_DK, GLA_DV), F32),
            jax.ShapeDtypeStruct((DEPTH, n_seq, POOL_BUF, POOL_WIDTH), F32),
        ] + w_out_shapes,
        scratch_shapes=[
            pltpu.VMEM((GLA_HEADS, GLA_DV, GLA_DK), F32),
            pltpu.VMEM((EXT0 + tt, POOL_WIDTH), F32),
            pltpu.VMEM((2, EXT0 + tt, POOL_GC), F32),
        ],
        input_output_aliases={n_in + k: 1 + k for k in range(n_alias)},
        compiler_params=pltpu.CompilerParams(
            dimension_semantics=("arbitrary", "arbitrary"), vmem_limit_bytes=VMEM_LIMIT_BYTES),
        name="mixer_prompt",
    )(qk, vgu, gk, s0, buf0, gla_norm, w_pool, pool_scale, *ffn_weights, *stacked)


def _mixer_sample_kernel(qk_ref, vgu_ref, gk_ref, s0_ref, buf0_ref, gn_ref, wpool_ref, ps_ref, *rest,
                         pos0, t_len, n_alias):
    cat_ref, sout_ref, bufout_ref, ext_ref = rest[n_alias:]
    ns = SEQ_PER_STEP
    rows = ns * t_len

    ext_ref[:, 1:BUF_ROWS, :] = buf0_ref[...]
    ext_ref[:, BUF_ROWS:BUF_ROWS + t_len, :] = (
        vgu_ref[:, U_OFF:U_OFF + POOL_WIDTH].astype(F32).reshape(ns, t_len, POOL_WIDTH))
    pos = pos0 + lax.broadcasted_iota(jnp.int32, (ns, t_len, POOL_GC), 1)
    for gi in range(len(POOL_WINDOWS)):
        cols = slice(gi * POOL_GC, (gi + 1) * POOL_GC)
        u_cols = ext_ref[:, BUF_ROWS:BUF_ROWS + t_len, cols]
        w = POOL_WINDOWS[gi]
        s = u_cols
        for sft in range(1, w):
            s = s + ext_ref[:, BUF_ROWS - sft:BUF_ROWS - sft + t_len, cols]
        cnt = jnp.minimum(w, pos + 1).astype(F32)
        d = (s / cnt - u_cols).reshape(rows, POOL_GC)
        y = _dot(d.astype(BF16), wpool_ref[gi]) * ps_ref[:, cols]
        cat_ref[:, GLA_WIDTH + gi * POOL_GC:GLA_WIDTH + (gi + 1) * POOL_GC] = y.astype(BF16)
    bufout_ref[...] = ext_ref[:, t_len + 1:t_len + BUF_ROWS, :]

    r_i = lax.broadcasted_iota(jnp.int32, (rows, rows), 0)
    c_i = lax.broadcasted_iota(jnp.int32, (rows, rows), 1)
    same_seq = (r_i - jnp.bitwise_and(r_i, t_len - 1)) == (c_i - jnp.bitwise_and(c_i, t_len - 1))
    tri = jnp.logical_and(r_i >= c_i, same_seq).astype(BF16)
    row_in_seq = lax.broadcasted_iota(jnp.int32, (ns, t_len, 1), 1)
    seq_of_row = lax.broadcasted_iota(jnp.int32, (rows, 1), 0) // t_len
    zero_pad = jnp.zeros((LANES - rows, GLA_DK), F32)

    for h in range(GLA_HEADS):
        kcols = slice(h * GLA_DK, (h + 1) * GLA_DK)
        vcols = slice(h * GLA_DV, (h + 1) * GLA_DV)
        qh = qk_ref[:, h * GLA_DK:(h + 1) * GLA_DK] * Q_SCALE
        kh = qk_ref[:, K_OFF + h * GLA_DK:K_OFF + (h + 1) * GLA_DK]
        vb = vgu_ref[:, V_OFF + h * GLA_DV:V_OFF + (h + 1) * GLA_DV]
        vh = vb.astype(F32)
        gt = vgu_ref[:, G_OFF + h * GLA_DV:G_OFF + (h + 1) * GLA_DV].astype(F32)
        b = _cumsum_rows(tri, gk_ref[:, kcols])
        q3 = qh.reshape(ns, t_len, GLA_DK)
        k3 = kh.reshape(ns, t_len, GLA_DK)
        b3 = b.reshape(ns, t_len, GLA_DK)
        v3 = vh.reshape(ns, t_len, GLA_DV)
        o3 = jnp.zeros((ns, t_len, GLA_DV), F32)
        for jj in range(t_len):
            p = q3 * (k3[:, jj:jj + 1, :] * jnp.exp(jnp.minimum(b3 - b3[:, jj:jj + 1, :], 0.0)))
            col = jnp.sum(p, axis=-1, keepdims=True)
            col = jnp.where(row_in_seq >= jj, col, 0.0)
            o3 = o3 + col * v3[:, jj:jj + 1, :]
        o = o3.reshape(rows, GLA_DV)

        qe = qh * jnp.exp(b)
        k_dec = (k3 * jnp.exp(b3[:, t_len - 1:t_len, :] - b3)).reshape(rows, GLA_DK)
        b_t = jnp.concatenate([b, zero_pad], axis=0).T
        for s in range(ns):
            mine = seq_of_row == s
            s0 = s0_ref[s, h]
            o = o + _dot(jnp.where(mine, qe, 0.0).astype(BF16), s0.astype(BF16))
            last = s * t_len + t_len - 1
            a_col = jnp.exp(b_t[:, last:last + 1])
            upd = _dot_tn(jnp.where(mine, k_dec, 0.0).astype(BF16), vb)
            sout_ref[s, h] = a_col * s0 + upd
        cat_ref[:, vcols] = _gated_head_out(o, gt, gn_ref[...]).astype(BF16)


def _mixer_sample(qk, vgu, gk, s_in, buf_in, layer, gla_norm, w_pool, pool_scale, cat, stacked, *,
                  n_seq, t_len, pos0, row0):
    ns = SEQ_PER_STEP
    rows = ns * t_len
    rb0 = row0 // rows
    n_alias = 1 + len(stacked)
    n_in = 8
    return pl.pallas_call(
        functools.partial(_mixer_sample_kernel, pos0=pos0, t_len=t_len, n_alias=n_alias),
        grid=(n_seq // ns,),
        in_specs=[
            pl.BlockSpec((rows, 2 * QK_WIDTH), lambda i: (rb0 + i, 0)),
            pl.BlockSpec((rows, VGU_WIDTH), lambda i: (rb0 + i, 0)),
            pl.BlockSpec((rows, QK_WIDTH), lambda i: (rb0 + i, 0)),
            _layer_spec((ns, GLA_HEADS, GLA_DK, GLA_DV), lambda i: (layer, i, 0, 0, 0)),
            _layer_spec((ns, POOL_BUF, POOL_WIDTH), lambda i: (layer, i, 0, 0)),
            _layer_spec((1, GLA_DV), lambda i: (layer, 0, 0)),
            _layer_spec((len(POOL_WINDOWS), POOL_GC, POOL_GC), lambda i: (layer, 0, 0, 0)),
            _layer_spec((1, POOL_WIDTH), lambda i: (layer, 0, 0)),
        ] + [pl.BlockSpec(memory_space=pl.ANY)] * n_alias,
        out_specs=[
            pl.BlockSpec((rows, D_MODEL), lambda i: (rb0 + i, 0)),
            _layer_spec((ns, GLA_HEADS, GLA_DK, GLA_DV), lambda i: (layer, i, 0, 0, 0)),
            _layer_spec((ns, POOL_BUF, POOL_WIDTH), lambda i: (layer, i, 0, 0)),
        ],
        out_shape=[
            jax.ShapeDtypeStruct(cat.shape, BF16),
            jax.ShapeDtypeStruct((DEPTH, n_seq, GLA_HEADS, GLA_DK, GLA_DV), F32),
            jax.ShapeDtypeStruct((DEPTH, n_seq, POOL_BUF, POOL_WIDTH), F32),
        ],
        scratch_shapes=[pltpu.VMEM((ns, BUF_ROWS + t_len, POOL_WIDTH), F32)],
        input_output_aliases={n_in + k: k for k in range(n_alias)},
        compiler_params=pltpu.CompilerParams(
            dimension_semantics=("arbitrary",), vmem_limit_bytes=VMEM_LIMIT_BYTES),
        name="mixer_sample",
    )(qk, vgu, gk, s_in, buf_in, gla_norm, w_pool, pool_scale, cat, *stacked)


def _ffn_kernel(*refs, n_x, n_o, n_first_tiles, final_norm):
    x_refs = refs[:n_x]
    cat_ref, wout_ref, nrm_ref, wg_ref, wu_ref, wd_ref, nf_ref = refs[n_x:n_x + 7]
    o_refs = refs[n_x + 7:n_x + 7 + n_o]
    h_ref = refs[-1]
    j = pl.program_id(1)
    nj = pl.num_programs(1)

    @pl.when(j == 0)
    def _():
        if n_o == 2:
            @pl.when(pl.program_id(0) == 0)
            def _():
                o_refs[1][...] = jnp.zeros(o_refs[1].shape, F32)

        def start(x_ref):
            x2 = x_ref[...] + _dot(cat_ref[...], wout_ref[...])
            h_ref[...] = _rmsnorm(x2, nrm_ref[...]).astype(BF16)

            def init(o_ref):
                o_ref[...] = x2

            _for_row_source(o_refs, n_first_tiles, init)

        _for_row_source(x_refs, n_first_tiles, start)

    h = h_ref[...]
    gate = _dot(h, wg_ref[...])
    up = _dot(h, wu_ref[...])
    act = (gate * jax.nn.sigmoid(gate)) * up
    delta = _dot(act.astype(BF16), wd_ref[...])
    if n_o == 1:
        o_refs[0][...] += delta
    else:
        first = pl.program_id(0) < n_first_tiles
        o_refs[0][...] += jnp.where(first, delta, 0.0)
        o_refs[1][...] += delta

    if final_norm:
        @pl.when(j == nj - 1)
        def _():
            def finish(o_ref):
                o_ref[...] = _rmsnorm(o_ref[...], nf_ref[...])

            _for_row_source(o_refs, n_first_tiles, finish)


def _ffn(xs, cat, layer, w_out, nrm, w_gate, w_up, w_down, norm_final, *, final_norm, out_rows):
    m = sum(x.shape[0] for x in xs)
    n_first_tiles = (xs[0].shape[0] if len(xs) == 2 else out_rows[0]) // TM_FFN
    grid = (m // TM_FFN, D_FF // TF_FFN)
    if len(out_rows) == 1:
        out_specs = [pl.BlockSpec((TM_FFN, D_MODEL), lambda i, j: (i, 0))]
    else:
        last = n_first_tiles - 1
        out_specs = [
            pl.BlockSpec((TM_FFN, D_MODEL), lambda i, j: (jnp.minimum(i, last), 0)),
            pl.BlockSpec((TM_FFN, D_MODEL), lambda i, j: (jnp.maximum(i - n_first_tiles, 0), 0)),
        ]
    return pl.pallas_call(
        functools.partial(_ffn_kernel, n_x=len(xs), n_o=len(out_rows),
                          n_first_tiles=n_first_tiles, final_norm=final_norm),
        grid=grid,
        in_specs=_split_rows_specs(TM_FFN, n_first_tiles, len(xs) == 2) + [
            pl.BlockSpec((TM_FFN, D_MODEL), lambda i, j: (i, 0)),
            pl.BlockSpec((D_MODEL, D_MODEL), lambda i, j: (0, 0), pipeline_mode=pl.Buffered(1)),
            _layer_spec((1, D_MODEL), lambda i, j: (layer, 0, 0)),
            pl.BlockSpec((None, D_MODEL, TF_FFN), lambda i, j: (j, 0, 0)),
            pl.BlockSpec((None, D_MODEL, TF_FFN), lambda i, j: (j, 0, 0)),
            pl.BlockSpec((TF_FFN, D_MODEL), lambda i, j: (j, 0)),
            pl.BlockSpec((1, D_MODEL), lambda i, j: (0, 0)),
        ],
        out_specs=out_specs,
        out_shape=[jax.ShapeDtypeStruct((r, D_MODEL), F32) for r in out_rows],
        scratch_shapes=[pltpu.VMEM((TM_FFN, D_MODEL), BF16)],
        compiler_params=pltpu.CompilerParams(
            dimension_semantics=("arbitrary", "arbitrary"), vmem_limit_bytes=VMEM_LIMIT_BYTES),
        name="ffn",
    )(*xs, cat, w_out, nrm, w_gate, w_up, w_down, norm_final)


def kernel(x_prompt, x_sample, state_gla, state_pool, norm_mix, w_in, w_gk_up, b_gk, gla_norm,
           w_pool, pool_scale, w_out, norm_ffn, w_gate, w_up, w_down, norm_final):
    n_p, t_p, _ = x_prompt.shape
    n_s, t_s, _ = x_sample.shape
    m_p = n_p * t_p
    m_s = n_s * t_s

    w_in_t = jnp.swapaxes(w_in, 1, 2).astype(BF16)
    gate0 = W_IN_GATE0
    w_lr_t = jnp.pad(w_in_t[:, gate0:gate0 + GLA_LOWRANK],
                     ((0, 0), (0, LANES - GLA_LOWRANK), (0, 0)))
    w_upp = jnp.pad(w_gk_up, ((0, 0), (0, LANES - GLA_LOWRANK), (0, 0))).astype(BF16)
    w_pool_b = w_pool.astype(BF16)
    ffn_weights = (w_out, w_gate, w_up, w_down)
    row3 = lambda a: a[:, None, :]
    norm_mix3, norm_ffn3, b_gk3 = row3(norm_mix), row3(norm_ffn), row3(b_gk)
    gla_norm3, pool_scale3 = row3(gla_norm), row3(pool_scale)

    s0_p = jnp.zeros((n_p, GLA_HEADS, GLA_DK, GLA_DV), F32)
    buf0_p = jnp.zeros((n_p, BUF_ROWS, POOL_WIDTH), F32)

    xs = [x_prompt.reshape(m_p, D_MODEL), x_sample.reshape(m_s, D_MODEL)]
    stacked_p, stacked_s = (), ()
    for l in range(DEPTH):
        last = l == DEPTH - 1
        qk, vgu, gk = _inproj(xs, l, norm_mix3, w_in_t, w_lr_t, w_upp, b_gk3)
        cat, *rest = _mixer_prompt(qk, vgu, gk, s0_p, buf0_p, l, gla_norm3, w_pool_b, pool_scale3,
                                   ffn_weights, (None, TF_FFN, TF_FFN, None), stacked_p,
                                   n_seq=n_p, seq_len=t_p, pos0=0)
        stacked_p, (w_out_b, w_gate_b, w_up_b, w_down_b) = rest[:2], rest[2:]
        cat, *stacked_s = _mixer_sample(qk, vgu, gk, state_gla, state_pool, l, gla_norm3, w_pool_b,
                                        pool_scale3, cat, stacked_s, n_seq=n_s, t_len=t_s,
                                        pos0=PAST_LEN, row0=m_p)
        xs = _ffn(xs, cat, l, w_out_b, norm_ffn3, w_gate_b, w_up_b, w_down_b, norm_final[None],
                  final_norm=last, out_rows=(m_p, m_s) if last else (m_p + m_s,))

    y_prompt = xs[0].reshape(n_p, t_p, D_MODEL)
    y_sample = xs[1].reshape(n_s, t_s, D_MODEL)
    return (y_prompt, y_sample, stacked_p[0], stacked_p[1], stacked_s[0], stacked_s[1])
```

```python
import functools

import jax
import jax.numpy as jnp
from jax import lax
from jax.experimental import pallas as pl
from jax.experimental.pallas import tpu as pltpu

F32 = jnp.float32
BF16 = jnp.bfloat16

D_MODEL = 2048
DEPTH = 2
EPS = 1e-6
GLA_WIDTH = D_MODEL // 2
GLA_HEADS = 4
GLA_DV = GLA_WIDTH // GLA_HEADS
GLA_DK = GLA_DV // 2
GLA_LOWRANK = 16
GK_NORM = 16.0
GLA_CHUNK = 64
POOL_WIDTH = D_MODEL - GLA_WIDTH
POOL_WINDOWS = (2, 4, 8, 16)
POOL_GC = POOL_WIDTH // len(POOL_WINDOWS)
POOL_BUF = 15
D_FF = ((8 * D_MODEL // 3 + 255) // 256) * 256
QK_WIDTH = GLA_HEADS * GLA_DK
W_IN_GATE0 = 2 * QK_WIDTH + 2 * GLA_WIDTH
MAIN_WIDTH = W_IN_GATE0 + POOL_WIDTH
VGU_WIDTH = MAIN_WIDTH - 2 * QK_WIDTH
K_OFF = QK_WIDTH
V_OFF = 0
G_OFF = GLA_WIDTH
U_OFF = 2 * GLA_WIDTH
Q_SCALE = GLA_DK ** -0.5
PAST_LEN = 16384

LANES = 128
SUBLANES = 8
VMEM_LIMIT_BYTES = 58 * 1024 * 1024

TM_PROJ = 1024
TN_PROJ = 2 * QK_WIDTH
N_ALIGNED_TILES = W_IN_GATE0 // TN_PROJ
PROJ_K_CHUNK = 512
TM_FFN = 512
TF_FFN = 512
TT_PROMPT = 256
SUB = 16
ROW_CHUNK = 128
SEQ_PER_STEP = 8
BUF_ROWS = POOL_BUF + 1
EXT0 = BUF_ROWS + SUBLANES


def _rmsnorm(x, g):
    r = lax.rsqrt(jnp.mean(x * x, axis=-1, keepdims=True) + EPS)
    return (x * r) * g


def _dot(a, b):
    return jnp.dot(a, b, preferred_element_type=F32)


def _dot_nt(a, b):
    return lax.dot_general(a, b, (((1,), (1,)), ((), ())), preferred_element_type=F32)


def _dot_tn(a, b):
    return lax.dot_general(a, b, (((0,), (0,)), ((), ())), preferred_element_type=F32)


def _split3(x):
    hi = x.astype(BF16)
    r1 = x - hi.astype(F32)
    mid = r1.astype(BF16)
    lo = (r1 - mid.astype(F32)).astype(BF16)
    return hi, mid, lo


def _cumsum_rows(tri, g):
    hi, mid, lo = _split3(g)
    b3 = _dot(tri, jnp.concatenate([hi, mid, lo], axis=1))
    n = g.shape[1]
    return b3[:, :n] + b3[:, n:2 * n] + b3[:, 2 * n:]


def _layer_spec(block, index_map, **kw):
    return pl.BlockSpec((None,) + tuple(block), index_map, **kw)


def _split_rows_specs(tm, n_first_tiles, two_inputs):
    if not two_inputs:
        return [pl.BlockSpec((tm, D_MODEL), lambda i, j: (i, 0))]
    last = n_first_tiles - 1
    return [
        pl.BlockSpec((tm, D_MODEL), lambda i, j: (jnp.minimum(i, last), 0)),
        pl.BlockSpec((tm, D_MODEL), lambda i, j: (jnp.maximum(i - n_first_tiles, 0), 0),
                     pipeline_mode=pl.Buffered(1)),
    ]


def _for_row_source(x_refs, n_first_tiles, fn):
    if len(x_refs) == 1:
        fn(x_refs[0])
        return
    i = pl.program_id(0)
    pl.when(i < n_first_tiles)(lambda: fn(x_refs[0]))
    pl.when(i >= n_first_tiles)(lambda: fn(x_refs[1]))


def _inproj_kernel(*refs, n_x, n_first_tiles):
    x_refs = refs[:n_x]
    nrm_ref, w_ref, wlr_ref, wup_ref, bgk_ref, qk_ref, vgu_ref, gk_ref, h_ref = refs[n_x:]
    j = pl.program_id(1)

    @pl.when(j == 0)
    def _():
        def normalize(x_ref):
            def body(r, carry):
                rows = pl.ds(pl.multiple_of(r * ROW_CHUNK, ROW_CHUNK), ROW_CHUNK)
                h_ref[rows, :] = _rmsnorm(x_ref[rows, :], nrm_ref[...]).astype(BF16)
                return carry

            lax.fori_loop(0, TM_PROJ // ROW_CHUNK, body, 0)

        _for_row_source(x_refs, n_first_tiles, normalize)

        lr = _dot_nt(h_ref[...], wlr_ref[...])
        z = _dot(lr.astype(BF16), wup_ref[...]) + bgk_ref[...]
        gk_ref[...] = jax.nn.log_sigmoid(z) / GK_NORM

    kc = PROJ_K_CHUNK
    t = _dot_nt(h_ref[:, 0:kc], w_ref[0, :, 0:kc])
    for k0 in range(kc, D_MODEL, kc):
        t = t + _dot_nt(h_ref[:, k0:k0 + kc], w_ref[0, :, k0:k0 + kc])

    @pl.when(j == 0)
    def _():
        qk_ref[...] = t

    @pl.when(j > 0)
    def _():
        vgu_ref[...] = t.astype(BF16)


def _inproj(xs, layer, nrm, w_in_t, w_lr_t, w_up, b_gk):
    m = sum(x.shape[0] for x in xs)
    n_first_tiles = xs[0].shape[0] // TM_PROJ
    grid = (m // TM_PROJ, MAIN_WIDTH // TN_PROJ)

    def w_rows(j):
        start = jnp.where(j < N_ALIGNED_TILES, j * TN_PROJ, j * TN_PROJ + GLA_LOWRANK)
        return pl.multiple_of(start, GLA_LOWRANK)

    return pl.pallas_call(
        functools.partial(_inproj_kernel, n_x=len(xs), n_first_tiles=n_first_tiles),
        grid=grid,
        in_specs=_split_rows_specs(TM_PROJ, n_first_tiles, len(xs) == 2) + [
            _layer_spec((1, D_MODEL), lambda i, j: (layer, 0, 0)),
            pl.BlockSpec((pl.Element(1), pl.Element(TN_PROJ), pl.Element(D_MODEL)),
                         lambda i, j: (layer, w_rows(j), 0)),
            _layer_spec((LANES, D_MODEL), lambda i, j: (layer, 0, 0)),
            _layer_spec((LANES, QK_WIDTH), lambda i, j: (layer, 0, 0)),
            _layer_spec((1, QK_WIDTH), lambda i, j: (layer, 0, 0)),
        ],
        out_specs=[
            pl.BlockSpec((TM_PROJ, TN_PROJ), lambda i, j: (i, 0)),
            pl.BlockSpec((TM_PROJ, TN_PROJ), lambda i, j: (i, jnp.maximum(j - 1, 0))),
            pl.BlockSpec((TM_PROJ, QK_WIDTH), lambda i, j: (i, 0)),
        ],
        out_shape=[
            jax.ShapeDtypeStruct((m, TN_PROJ), F32),
            jax.ShapeDtypeStruct((m, VGU_WIDTH), BF16),
            jax.ShapeDtypeStruct((m, QK_WIDTH), F32),
        ],
        scratch_shapes=[pltpu.VMEM((TM_PROJ, D_MODEL), BF16)],
        compiler_params=pltpu.CompilerParams(
            dimension_semantics=("arbitrary", "arbitrary"), vmem_limit_bytes=VMEM_LIMIT_BYTES),
        name="inproj",
    )(*xs, nrm, w_in_t, w_lr_t, w_up, b_gk)


def _pool_output(s, u_cols, pos, gi, wpool_ref, ps_ref):
    cnt = jnp.minimum(POOL_WINDOWS[gi], pos + 1).astype(F32)
    d = s / cnt - u_cols
    cols = slice(gi * POOL_GC, (gi + 1) * POOL_GC)
    return _dot(d.astype(BF16), wpool_ref[gi]) * ps_ref[:, cols]


def _window_sums_doubling(ext_ref, lvl_ref, cols, w, n_rows):
    end = EXT0 + n_rows
    n_lvl = w.bit_length() - 1

    def read(lo, hi):
        return ext_ref[lo:hi, cols]

    for lvl in range(n_lvl):
        sh = 1 << lvl
        lo = EXT0 if lvl == n_lvl - 1 else SUBLANES
        val = read(lo, end) + read(lo - sh, end - sh)
        if lvl == n_lvl - 1:
            return val
        buf = lvl_ref.at[lvl % 2]
        buf[lo:end, :] = val

        def read(lo_, hi_, buf=buf):
            return buf[lo_:hi_, :]


def _gla_chunk(q, k, v, g, st_ref, masks):
    tri, row_id, pair_level, on_diag = masks
    c = GLA_CHUNK
    b = _cumsum_rows(tri, g)

    a = jnp.where(on_diag, jnp.sum(q * k, axis=-1, keepdims=True), 0.0)
    for lvl in range(c.bit_length() - 1):
        s = 1 << lvl
        f = jnp.exp(-jnp.abs(b - _segment_mid_rows(b, row_id, s)))
        upper = jnp.bitwise_and(row_id, s) != 0
        x = f * jnp.where(upper, q, k)
        qt = jnp.where(upper, x, 0.0).astype(BF16)
        kt = jnp.where(upper, 0.0, x).astype(BF16)
        a = jnp.where(pair_level == lvl, _dot_nt(qt, kt), a)

    vb = v.astype(BF16)
    o_intra = _dot(a.astype(BF16), vb)
    qe = (q * jnp.exp(b)).astype(BF16)
    outs = []
    for h in range(GLA_HEADS):
        sl = slice(h * c, (h + 1) * c)
        st = st_ref[h]
        outs.append(o_intra[sl] + _dot_nt(qe[sl], st.astype(BF16)))
        b_last = b[(h + 1) * c - 1:(h + 1) * c, :]
        k_dec = k[sl] * jnp.exp(b_last - b[sl])
        st_ref[h] = st * jnp.exp(b_last) + _dot_tn(vb[sl], k_dec.astype(BF16))
    return outs


def _segment_mid_rows(b, row_id, s):
    c, dk = b.shape
    seg = 2 * s
    if seg >= SUBLANES:
        b3 = b.reshape(c // seg, seg, dk)
        return jnp.broadcast_to(b3[:, s:s + 1, :], b3.shape).reshape(c, dk)
    b3 = b.reshape(c // SUBLANES, SUBLANES, dk)
    place = jnp.bitwise_and(row_id, seg - 1)
    out = b
    for p in range(seg):
        if p != s:
            below = pltpu.roll(b3, (p - s) % SUBLANES, axis=1).reshape(c, dk)
            out = jnp.where(place == p, below, out)
    return out


def _gla_masks():
    n = GLA_HEADS * GLA_CHUNK
    row = lax.broadcasted_iota(jnp.int32, (n, n), 0)
    col = lax.broadcasted_iota(jnp.int32, (n, n), 1)
    top_differing_bit = 31 - lax.clz(jnp.bitwise_xor(row, col))
    same_head = top_differing_bit < GLA_CHUNK.bit_length() - 1
    tri = jnp.logical_and(row >= col, same_head).astype(BF16)
    row_id = lax.broadcasted_iota(jnp.int32, (n, GLA_DK), 0)
    pair_level = jnp.where(col < row, top_differing_bit, -1)
    return tri, row_id, pair_level, row == col


def _gated_head_out(o, gt, gn):
    return _rmsnorm(o, gn) * (gt * jax.nn.sigmoid(gt))


def _mixer_prompt_kernel(qk_ref, vgu_ref, gk_ref, s0_ref, buf0_ref, gn_ref, wpool_ref, ps_ref, *rest,
                         pos0, n_alias, n_w):
    w32_refs = rest[:n_w]
    rest = rest[n_w + n_alias:]
    cat_ref, sout_ref, bufout_ref = rest[:3]
    w16_refs = rest[3:3 + n_w]
    st_ref, ext_ref, lvl_ref = rest[3 + n_w:]
    t = pl.program_id(1)
    nt = pl.num_programs(1)
    tt = TT_PROMPT

    for w32_ref, w16_ref in zip(w32_refs, w16_refs):
        if len(w16_ref.shape) == 2:
            w16_ref[...] = w32_ref[...].astype(BF16)
        else:
            width = w16_ref.shape[2]
            for ct in range(w16_ref.shape[0]):
                w16_ref[ct] = w32_ref[:, ct * width:(ct + 1) * width].astype(BF16)

    @pl.when(t == 0)
    def _():
        for h in range(GLA_HEADS):
            st_ref[h] = s0_ref[0, h].T
        ext_ref[0:SUBLANES, :] = jnp.zeros((SUBLANES, POOL_WIDTH), F32)
        lvl_ref[:, 0:SUBLANES, :] = jnp.zeros((2, SUBLANES, POOL_GC), F32)
        ext_ref[SUBLANES:EXT0, :] = buf0_ref[0]

    ext_ref[EXT0:EXT0 + tt, :] = vgu_ref[:, U_OFF:U_OFF + POOL_WIDTH].astype(F32)
    pos = pos0 + t * tt + lax.broadcasted_iota(jnp.int32, (tt, POOL_GC), 0)
    for gi, w in enumerate(POOL_WINDOWS):
        cols = slice(gi * POOL_GC, (gi + 1) * POOL_GC)
        s = _window_sums_doubling(ext_ref, lvl_ref, cols, w, tt)
        y = _pool_output(s, ext_ref[EXT0:EXT0 + tt, cols], pos, gi, wpool_ref, ps_ref)
        cat_ref[:, GLA_WIDTH + gi * POOL_GC:GLA_WIDTH + (gi + 1) * POOL_GC] = y.astype(BF16)
    ext_ref[SUBLANES:EXT0, :] = ext_ref[tt + SUBLANES:tt + EXT0, :]

    masks = _gla_masks()

    def chunk_body(ci, carry_):
        rows = pl.ds(pl.multiple_of(ci * GLA_CHUNK, GLA_CHUNK), GLA_CHUNK)

        def heads_on_rows(ref, off, width):
            return jnp.concatenate(
                [ref[rows, off + h * width:off + (h + 1) * width] for h in range(GLA_HEADS)], axis=0)

        outs = _gla_chunk(heads_on_rows(qk_ref, 0, GLA_DK) * Q_SCALE,
                          heads_on_rows(qk_ref, K_OFF, GLA_DK),
                          heads_on_rows(vgu_ref, V_OFF, GLA_DV),
                          heads_on_rows(gk_ref, 0, GLA_DK), st_ref, masks)
        for h in range(GLA_HEADS):
            vcols = slice(h * GLA_DV, (h + 1) * GLA_DV)
            gt = vgu_ref[rows, G_OFF + h * GLA_DV:G_OFF + (h + 1) * GLA_DV].astype(F32)
            cat_ref[rows, vcols] = _gated_head_out(outs[h], gt, gn_ref[...]).astype(BF16)
        return carry_

    lax.fori_loop(0, tt // GLA_CHUNK, chunk_body, 0, unroll=True)

    @pl.when(t == nt - 1)
    def _():
        for h in range(GLA_HEADS):
            sout_ref[0, h] = st_ref[h].T
        bufout_ref[0] = ext_ref[EXT0 - POOL_BUF:EXT0, :]


def _mixer_prompt(qk, vgu, gk, s0, buf0, layer, gla_norm, w_pool, pool_scale, ffn_weights, col_tiles,
                  stacked, *, n_seq, seq_len, pos0):
    tt = TT_PROMPT
    nt = seq_len // tt
    n_steps = n_seq * nt
    row = lambda b, t: b * nt + t
    n_alias = len(stacked)
    n_in = 8 + len(ffn_weights)
    w_blocks = [(w.shape[1] // n_steps, w.shape[2]) for w in ffn_weights]
    w_out_specs, w_out_shapes = [], []
    for w, (rows, cols), ct in zip(ffn_weights, w_blocks, col_tiles):
        if ct is None:
            w_out_specs.append(pl.BlockSpec((rows, cols), lambda b, t: (row(b, t), 0)))
            w_out_shapes.append(jax.ShapeDtypeStruct(w.shape[1:], BF16))
        else:
            w_out_specs.append(pl.BlockSpec((cols // ct, rows, ct), lambda b, t: (0, row(b, t), 0)))
            w_out_shapes.append(jax.ShapeDtypeStruct((cols // ct, w.shape[1], ct), BF16))
    return pl.pallas_call(
        functools.partial(_mixer_prompt_kernel, pos0=pos0, n_alias=n_alias, n_w=len(ffn_weights)),
        grid=(n_seq, nt),
        in_specs=[
            pl.BlockSpec((tt, 2 * QK_WIDTH), lambda b, t: (row(b, t), 0)),
            pl.BlockSpec((tt, VGU_WIDTH), lambda b, t: (row(b, t), 0)),
            pl.BlockSpec((tt, QK_WIDTH), lambda b, t: (row(b, t), 0)),
            pl.BlockSpec((1, GLA_HEADS, GLA_DK, GLA_DV), lambda b, t: (b, 0, 0, 0)),
            pl.BlockSpec((1, BUF_ROWS, POOL_WIDTH), lambda b, t: (b, 0, 0)),
            _layer_spec((1, GLA_DV), lambda b, t: (layer, 0, 0)),
            _layer_spec((len(POOL_WINDOWS), POOL_GC, POOL_GC), lambda b, t: (layer, 0, 0, 0)),
            _layer_spec((1, POOL_WIDTH), lambda b, t: (layer, 0, 0)),
        ] + [_layer_spec(blk, lambda b, t: (layer, row(b, t), 0)) for blk in w_blocks]
        + [pl.BlockSpec(memory_space=pl.ANY)] * n_alias,
        out_specs=[
            pl.BlockSpec((tt, D_MODEL), lambda b, t: (row(b, t), 0)),
            _layer_spec((1, GLA_HEADS, GLA_DK, GLA_DV), lambda b, t: (layer, b, 0, 0, 0)),
            _layer_spec((1, POOL_BUF, POOL_WIDTH), lambda b, t: (layer, b, 0, 0)),
        ] + w_out_specs,
        out_shape=[
            jax.ShapeDtypeStruct((qk.shape[0], D_MODEL), BF16),
            jax.ShapeDtypeStruct((DEPTH, n_seq, GLA_HEADS, GLA_DK, GLA_DV), F32),
            jax.ShapeDtypeStruct((DEPTH, n_seq, POOL_BUF, POOL_WIDTH), F32),
        ] + w_out_shapes,
        scratch_shapes=[
            pltpu.VMEM((GLA_HEADS, GLA_DV, GLA_DK), F32),
            pltpu.VMEM((EXT0 + tt, POOL_WIDTH), F32),
            pltpu.VMEM((2, EXT0 + tt, POOL_GC), F32),
        ],
        input_output_aliases={n_in + k: 1 + k for k in range(n_alias)},
        compiler_params=pltpu.CompilerParams(
            dimension_semantics=("arbitrary", "arbitrary"), vmem_limit_bytes=VMEM_LIMIT_BYTES),
        name="mixer_prompt",
    )(qk, vgu, gk, s0, buf0, gla_norm, w_pool, pool_scale, *ffn_weights, *stacked)


def _mixer_sample_kernel(qk_ref, vgu_ref, gk_ref, s0_ref, buf0_ref, gn_ref, wpool_ref, ps_ref, *rest,
                         pos0, t_len, n_alias):
    cat_ref, sout_ref, bufout_ref, ext_ref = rest[n_alias:]
    ns = SEQ_PER_STEP
    rows = ns * t_len

    ext_ref[:, 1:BUF_ROWS, :] = buf0_ref[...]
    ext_ref[:, BUF_ROWS:BUF_ROWS + t_len, :] = (
        vgu_ref[:, U_OFF:U_OFF + POOL_WIDTH].astype(F32).reshape(ns, t_len, POOL_WIDTH))
    pos = pos0 + lax.broadcasted_iota(jnp.int32, (ns, t_len, POOL_GC), 1)
    for gi in range(len(POOL_WINDOWS)):
        cols = slice(gi * POOL_GC, (gi + 1) * POOL_GC)
        u_cols = ext_ref[:, BUF_ROWS:BUF_ROWS + t_len, cols]
        w = POOL_WINDOWS[gi]
        s = u_cols
        for sft in range(1, w):
            s = s + ext_ref[:, BUF_ROWS - sft:BUF_ROWS - sft + t_len, cols]
        cnt = jnp.minimum(w, pos + 1).astype(F32)
        d = (s / cnt - u_cols).reshape(rows, POOL_GC)
        y = _dot(d.astype(BF16), wpool_ref[gi]) * ps_ref[:, cols]
        cat_ref[:, GLA_WIDTH + gi * POOL_GC:GLA_WIDTH + (gi + 1) * POOL_GC] = y.astype(BF16)
    bufout_ref[...] = ext_ref[:, t_len + 1:t_len + BUF_ROWS, :]

    r_i = lax.broadcasted_iota(jnp.int32, (rows, rows), 0)
    c_i = lax.broadcasted_iota(jnp.int32, (rows, rows), 1)
    same_seq = (r_i - jnp.bitwise_and(r_i, t_len - 1)) == (c_i - jnp.bitwise_and(c_i, t_len - 1))
    tri = jnp.logical_and(r_i >= c_i, same_seq).astype(BF16)
    row_in_seq = lax.broadcasted_iota(jnp.int32, (ns, t_len, 1), 1)
    seq_of_row = lax.broadcasted_iota(jnp.int32, (rows, 1), 0) // t_len
    zero_pad = jnp.zeros((LANES - rows, GLA_DK), F32)

    for h in range(GLA_HEADS):
        kcols = slice(h * GLA_DK, (h + 1) * GLA_DK)
        vcols = slice(h * GLA_DV, (h + 1) * GLA_DV)
        qh = qk_ref[:, h * GLA_DK:(h + 1) * GLA_DK] * Q_SCALE
        kh = qk_ref[:, K_OFF + h * GLA_DK:K_OFF + (h + 1) * GLA_DK]
        vb = vgu_ref[:, V_OFF + h * GLA_DV:V_OFF + (h + 1) * GLA_DV]
        vh = vb.astype(F32)
        gt = vgu_ref[:, G_OFF + h * GLA_DV:G_OFF + (h + 1) * GLA_DV].astype(F32)
        b = _cumsum_rows(tri, gk_ref[:, kcols])
        q3 = qh.reshape(ns, t_len, GLA_DK)
        k3 = kh.reshape(ns, t_len, GLA_DK)
        b3 = b.reshape(ns, t_len, GLA_DK)
        v3 = vh.reshape(ns, t_len, GLA_DV)
        o3 = jnp.zeros((ns, t_len, GLA_DV), F32)
        for jj in range(t_len):
            p = q3 * (k3[:, jj:jj + 1, :] * jnp.exp(jnp.minimum(b3 - b3[:, jj:jj + 1, :], 0.0)))
            col = jnp.sum(p, axis=-1, keepdims=True)
            col = jnp.where(row_in_seq >= jj, col, 0.0)
            o3 = o3 + col * v3[:, jj:jj + 1, :]
        o = o3.reshape(rows, GLA_DV)

        qe = qh * jnp.exp(b)
        k_dec = (k3 * jnp.exp(b3[:, t_len - 1:t_len, :] - b3)).reshape(rows, GLA_DK)
        b_t = jnp.concatenate([b, zero_pad], axis=0).T
        for s in range(ns):
            mine = seq_of_row == s
            s0 = s0_ref[s, h]
            o = o + _dot(jnp.where(mine, qe, 0.0).astype(BF16), s0.astype(BF16))
            last = s * t_len + t_len - 1
            a_col = jnp.exp(b_t[:, last:last + 1])
            upd = _dot_tn(jnp.where(mine, k_dec, 0.0).astype(BF16), vb)
            sout_ref[s, h] = a_col * s0 + upd
        cat_ref[:, vcols] = _gated_head_out(o, gt, gn_ref[...]).astype(BF16)


def _mixer_sample(qk, vgu, gk, s_in, buf_in, layer, gla_norm, w_pool, pool_scale, cat, stacked, *,
                  n_seq, t_len, pos0, row0):
    ns = SEQ_PER_STEP
    rows = ns * t_len
    rb0 = row0 // rows
    n_alias = 1 + len(stacked)
    n_in = 8
    return pl.pallas_call(
        functools.partial(_mixer_sample_kernel, pos0=pos0, t_len=t_len, n_alias=n_alias),
        grid=(n_seq // ns,),
        in_specs=[
            pl.BlockSpec((rows, 2 * QK_WIDTH), lambda i: (rb0 + i, 0)),
            pl.BlockSpec((rows, VGU_WIDTH), lambda i: (rb0 + i, 0)),
            pl.BlockSpec((rows, QK_WIDTH), lambda i: (rb0 + i, 0)),
            _layer_spec((ns, GLA_HEADS, GLA_DK, GLA_DV), lambda i: (layer, i, 0, 0, 0)),
            _layer_spec((ns, POOL_BUF, POOL_WIDTH), lambda i: (layer, i, 0, 0)),
            _layer_spec((1, GLA_DV), lambda i: (layer, 0, 0)),
            _layer_spec((len(POOL_WINDOWS), POOL_GC, POOL_GC), lambda i: (layer, 0, 0, 0)),
            _layer_spec((1, POOL_WIDTH), lambda i: (layer, 0, 0)),
        ] + [pl.BlockSpec(memory_space=pl.ANY)] * n_alias,
        out_specs=[
            pl.BlockSpec((rows, D_MODEL), lambda i: (rb0 + i, 0)),
            _layer_spec((ns, GLA_HEADS, GLA_DK, GLA_DV), lambda i: (layer, i, 0, 0, 0)),
            _layer_spec((ns, POOL_BUF, POOL_WIDTH), lambda i: (layer, i, 0, 0)),
        ],
        out_shape=[
            jax.ShapeDtypeStruct(cat.shape, BF16),
            jax.ShapeDtypeStruct((DEPTH, n_seq, GLA_HEADS, GLA_DK, GLA_DV), F32),
            jax.ShapeDtypeStruct((DEPTH, n_seq, POOL_BUF, POOL_WIDTH), F32),
        ],
        scratch_shapes=[pltpu.VMEM((ns, BUF_ROWS + t_len, POOL_WIDTH), F32)],
        input_output_aliases={n_in + k: k for k in range(n_alias)},
        compiler_params=pltpu.CompilerParams(
            dimension_semantics=("arbitrary",), vmem_limit_bytes=VMEM_LIMIT_BYTES),
        name="mixer_sample",
    )(qk, vgu, gk, s_in, buf_in, gla_norm, w_pool, pool_scale, cat, *stacked)


def _ffn_kernel(*refs, n_x, n_o, n_first_tiles, final_norm):
    x_refs = refs[:n_x]
    cat_ref, wout_ref, nrm_ref, wg_hbm, wu_hbm, wd_hbm, nf_ref = refs[n_x:n_x + 7]
    o_refs = refs[n_x + 7:n_x + 7 + n_o]
    h_ref, wg_buf, wu_buf, wd_buf, sem = refs[n_x + 7 + n_o:]
    i = pl.program_id(0)
    n_i = pl.num_programs(0)
    nj = wg_hbm.shape[0]

    def tile_copies(jt, slot):
        pairs = ((wg_hbm, wg_buf), (wu_hbm, wu_buf), (wd_hbm, wd_buf))
        return [pltpu.make_async_copy(src.at[jt], buf.at[slot], sem.at[k, slot])
                for k, (src, buf) in enumerate(pairs)]

    @pl.when(i == 0)
    def _():
        for c in tile_copies(0, 0):
            c.start()
        if n_o == 2:
            o_refs[1][...] = jnp.zeros(o_refs[1].shape, F32)

    def start(x_ref):
        x2 = x_ref[...] + _dot(cat_ref[...], wout_ref[...])
        h_ref[...] = _rmsnorm(x2, nrm_ref[...]).astype(BF16)

        def init(o_ref):
            o_ref[...] = x2

        _for_row_source(o_refs, n_first_tiles, init)

    _for_row_source(x_refs, n_first_tiles, start)

    first = i < n_first_tiles

    def step(jt, carry):
        s = i * nj + jt
        slot = jnp.bitwise_and(s, 1)

        @pl.when(s + 1 < n_i * nj)
        def _():
            for c in tile_copies(jnp.where(jt + 1 < nj, jt + 1, 0), 1 - slot):
                c.start()

        for c in tile_copies(jt, slot):
            c.wait()

        h = h_ref[...]
        gate = _dot(h, wg_buf[slot])
        up = _dot(h, wu_buf[slot])
        act = (gate * jax.nn.sigmoid(gate)) * up
        delta = _dot(act.astype(BF16), wd_buf[slot])
        if n_o == 1:
            o_refs[0][...] += delta
        else:
            o_refs[0][...] += jnp.where(first, delta, 0.0)
            o_refs[1][...] += delta
        return carry

    lax.fori_loop(0, nj, step, 0)

    if final_norm:
        def finish(o_ref):
            o_ref[...] = _rmsnorm(o_ref[...], nf_ref[...])

        _for_row_source(o_refs, n_first_tiles, finish)


def _ffn(xs, cat, layer, w_out, nrm, w_gate_t, w_up_t, w_down_t, norm_final, *, final_norm, out_rows):
    m = sum(x.shape[0] for x in xs)
    n_first_tiles = (xs[0].shape[0] if len(xs) == 2 else out_rows[0]) // TM_FFN
    last = n_first_tiles - 1
    first_rows = lambda i: (jnp.minimum(i, last), 0)
    second_rows = lambda i: (jnp.maximum(i - n_first_tiles, 0), 0)
    tile = (TM_FFN, D_MODEL)
    if len(xs) == 1:
        x_specs = [pl.BlockSpec(tile, lambda i: (i, 0))]
    else:
        x_specs = [pl.BlockSpec(tile, first_rows),
                   pl.BlockSpec(tile, second_rows, pipeline_mode=pl.Buffered(1))]
    if len(out_rows) == 1:
        out_specs = [pl.BlockSpec(tile, lambda i: (i, 0))]
    else:
        out_specs = [pl.BlockSpec(tile, first_rows), pl.BlockSpec(tile, second_rows)]
    return pl.pallas_call(
        functools.partial(_ffn_kernel, n_x=len(xs), n_o=len(out_rows),
                          n_first_tiles=n_first_tiles, final_norm=final_norm),
        grid=(m // TM_FFN,),
        in_specs=x_specs + [
            pl.BlockSpec(tile, lambda i: (i, 0)),
            pl.BlockSpec((D_MODEL, D_MODEL), lambda i: (0, 0), pipeline_mode=pl.Buffered(1)),
            _layer_spec((1, D_MODEL), lambda i: (layer, 0, 0)),
            pl.BlockSpec(memory_space=pl.ANY),
            pl.BlockSpec(memory_space=pl.ANY),
            pl.BlockSpec(memory_space=pl.ANY),
            pl.BlockSpec((1, D_MODEL), lambda i: (0, 0)),
        ],
        out_specs=out_specs,
        out_shape=[jax.ShapeDtypeStruct((r, D_MODEL), F32) for r in out_rows],
        scratch_shapes=[
            pltpu.VMEM(tile, BF16),
            pltpu.VMEM((2, D_MODEL, TF_FFN), BF16),
            pltpu.VMEM((2, D_MODEL, TF_FFN), BF16),
            pltpu.VMEM((2, TF_FFN, D_MODEL), BF16),
            pltpu.SemaphoreType.DMA((3, 2)),
        ],
        compiler_params=pltpu.CompilerParams(
            dimension_semantics=("arbitrary",), vmem_limit_bytes=VMEM_LIMIT_BYTES),
        name="ffn",
    )(*xs, cat, w_out, nrm, w_gate_t, w_up_t, w_down_t, norm_final)


def kernel(x_prompt, x_sample, state_gla, state_pool, norm_mix, w_in, w_gk_up, b_gk, gla_norm,
           w_pool, pool_scale, w_out, norm_ffn, w_gate, w_up, w_down, norm_final):
    n_p, t_p, _ = x_prompt.shape
    n_s, t_s, _ = x_sample.shape
    m_p = n_p * t_p
    m_s = n_s * t_s

    w_in_t = jnp.swapaxes(w_in, 1, 2).astype(BF16)
    gate0 = W_IN_GATE0
    w_lr_t = jnp.pad(w_in_t[:, gate0:gate0 + GLA_LOWRANK],
                     ((0, 0), (0, LANES - GLA_LOWRANK), (0, 0)))
    w_upp = jnp.pad(w_gk_up, ((0, 0), (0, LANES - GLA_LOWRANK), (0, 0))).astype(BF16)
    w_pool_b = w_pool.astype(BF16)
    ffn_weights = (w_out, w_gate, w_up, w_down)
    row3 = lambda a: a[:, None, :]
    norm_mix3, norm_ffn3, b_gk3 = row3(norm_mix), row3(norm_ffn), row3(b_gk)
    gla_norm3, pool_scale3 = row3(gla_norm), row3(pool_scale)

    s0_p = jnp.zeros((n_p, GLA_HEADS, GLA_DK, GLA_DV), F32)
    buf0_p = jnp.zeros((n_p, BUF_ROWS, POOL_WIDTH), F32)

    xs = [x_prompt.reshape(m_p, D_MODEL), x_sample.reshape(m_s, D_MODEL)]
    stacked_p, stacked_s = (), ()
    for l in range(DEPTH):
        last = l == DEPTH - 1
        qk, vgu, gk = _inproj(xs, l, norm_mix3, w_in_t, w_lr_t, w_upp, b_gk3)
        cat, *rest = _mixer_prompt(qk, vgu, gk, s0_p, buf0_p, l, gla_norm3, w_pool_b, pool_scale3,
                                   ffn_weights, (None, TF_FFN, TF_FFN, None), stacked_p,
                                   n_seq=n_p, seq_len=t_p, pos0=0)
        stacked_p, (w_out_b, w_gate_b, w_up_b, w_down_b) = rest[:2], rest[2:]
        cat, *stacked_s = _mixer_sample(qk, vgu, gk, state_gla, state_pool, l, gla_norm3, w_pool_b,
                                        pool_scale3, cat, stacked_s, n_seq=n_s, t_len=t_s,
                                        pos0=PAST_LEN, row0=m_p)
        w_down_t = w_down_b.reshape(D_FF // TF_FFN, TF_FFN, D_MODEL)
        xs = _ffn(xs, cat, l, w_out_b, norm_ffn3, w_gate_b, w_up_b, w_down_t, norm_final[None],
                  final_norm=last, out_rows=(m_p, m_s) if last else (m_p + m_s,))

    y_prompt = xs[0].reshape(n_p, t_p, D_MODEL)
    y_sample = xs[1].reshape(n_s, t_s, D_MODEL)
    return (y_prompt, y_sample, stacked_p[0], stacked_p[1], stacked_s[0], stacked_s[1])
```

```python
import functools

import jax
import jax.numpy as jnp
from jax import lax
from jax.experimental import pallas as pl
from jax.experimental.pallas import tpu as pltpu

F32 = jnp.float32
BF16 = jnp.bfloat16

D_MODEL = 2048
DEPTH = 2
EPS = 1e-6
GLA_WIDTH = D_MODEL // 2
GLA_HEADS = 4
GLA_DV = GLA_WIDTH // GLA_HEADS
GLA_DK = GLA_DV // 2
GLA_LOWRANK = 16
GK_NORM = 16.0
GLA_CHUNK = 64
POOL_WIDTH = D_MODEL - GLA_WIDTH
POOL_WINDOWS = (2, 4, 8, 16)
POOL_GC = POOL_WIDTH // len(POOL_WINDOWS)
POOL_BUF = 15
D_FF = ((8 * D_MODEL // 3 + 255) // 256) * 256
QK_WIDTH = GLA_HEADS * GLA_DK
W_IN_GATE0 = 2 * QK_WIDTH + 2 * GLA_WIDTH
MAIN_WIDTH = W_IN_GATE0 + POOL_WIDTH
VGU_WIDTH = MAIN_WIDTH - 2 * QK_WIDTH
K_OFF = QK_WIDTH
V_OFF = 0
G_OFF = GLA_WIDTH
U_OFF = 2 * GLA_WIDTH
Q_SCALE = GLA_DK ** -0.5
PAST_LEN = 16384

LANES = 128
SUBLANES = 8
VMEM_LIMIT_BYTES = 58 * 1024 * 1024

TM_PROJ = 1024
TN_PROJ = 2 * QK_WIDTH
N_ALIGNED_TILES = W_IN_GATE0 // TN_PROJ
PROJ_K_CHUNK = 512
TM_FFN = 512
TF_FFN = 512
TT_PROMPT = 256
SUB = 16
ROW_CHUNK = 128
SEQ_PER_STEP = 8
BUF_ROWS = POOL_BUF + 1
EXT0 = BUF_ROWS + SUBLANES


def _rmsnorm(x, g):
    r = lax.rsqrt(jnp.mean(x * x, axis=-1, keepdims=True) + EPS)
    return (x * r) * g


def _dot(a, b):
    return jnp.dot(a, b, preferred_element_type=F32)


def _dot_nt(a, b):
    return lax.dot_general(a, b, (((1,), (1,)), ((), ())), preferred_element_type=F32)


def _dot_tn(a, b):
    return lax.dot_general(a, b, (((0,), (0,)), ((), ())), preferred_element_type=F32)


def _split3(x):
    hi = x.astype(BF16)
    r1 = x - hi.astype(F32)
    mid = r1.astype(BF16)
    lo = (r1 - mid.astype(F32)).astype(BF16)
    return hi, mid, lo


def _cumsum_rows(tri, g):
    hi, mid, lo = _split3(g)
    b3 = _dot(tri, jnp.concatenate([hi, mid, lo], axis=1))
    n = g.shape[1]
    return b3[:, :n] + b3[:, n:2 * n] + b3[:, 2 * n:]


def _layer_spec(block, index_map, **kw):
    return pl.BlockSpec((None,) + tuple(block), index_map, **kw)


def _split_rows_specs(tm, n_first_tiles, two_inputs):
    if not two_inputs:
        return [pl.BlockSpec((tm, D_MODEL), lambda i, j: (i, 0))]
    last = n_first_tiles - 1
    return [
        pl.BlockSpec((tm, D_MODEL), lambda i, j: (jnp.minimum(i, last), 0)),
        pl.BlockSpec((tm, D_MODEL), lambda i, j: (jnp.maximum(i - n_first_tiles, 0), 0),
                     pipeline_mode=pl.Buffered(1)),
    ]


def _for_row_source(x_refs, n_first_tiles, fn):
    if len(x_refs) == 1:
        fn(x_refs[0])
        return
    i = pl.program_id(0)
    pl.when(i < n_first_tiles)(lambda: fn(x_refs[0]))
    pl.when(i >= n_first_tiles)(lambda: fn(x_refs[1]))


def _inproj_kernel(*refs, n_x, n_first_tiles):
    x_refs = refs[:n_x]
    nrm_ref, w_ref, wlr_ref, wup_ref, bgk_ref, qk_ref, vgu_ref, gk_ref, h_ref = refs[n_x:]
    j = pl.program_id(1)

    @pl.when(j == 0)
    def _():
        def normalize(x_ref):
            def body(r, carry):
                rows = pl.ds(pl.multiple_of(r * ROW_CHUNK, ROW_CHUNK), ROW_CHUNK)
                h_ref[rows, :] = _rmsnorm(x_ref[rows, :], nrm_ref[...]).astype(BF16)
                return carry

            lax.fori_loop(0, TM_PROJ // ROW_CHUNK, body, 0)

        _for_row_source(x_refs, n_first_tiles, normalize)

        lr = _dot_nt(h_ref[...], wlr_ref[...])
        z = _dot(lr.astype(BF16), wup_ref[...]) + bgk_ref[...]
        gk_ref[...] = jax.nn.log_sigmoid(z) / GK_NORM

    kc = PROJ_K_CHUNK
    t = _dot_nt(h_ref[:, 0:kc], w_ref[0, :, 0:kc])
    for k0 in range(kc, D_MODEL, kc):
        t = t + _dot_nt(h_ref[:, k0:k0 + kc], w_ref[0, :, k0:k0 + kc])

    @pl.when(j == 0)
    def _():
        qk_ref[...] = t

    @pl.when(j > 0)
    def _():
        vgu_ref[...] = t.astype(BF16)


def _inproj(xs, layer, nrm, w_in_t, w_lr_t, w_up, b_gk):
    m = sum(x.shape[0] for x in xs)
    n_first_tiles = xs[0].shape[0] // TM_PROJ
    grid = (m // TM_PROJ, MAIN_WIDTH // TN_PROJ)

    def w_rows(j):
        start = jnp.where(j < N_ALIGNED_TILES, j * TN_PROJ, j * TN_PROJ + GLA_LOWRANK)
        return pl.multiple_of(start, GLA_LOWRANK)

    return pl.pallas_call(
        functools.partial(_inproj_kernel, n_x=len(xs), n_first_tiles=n_first_tiles),
        grid=grid,
        in_specs=_split_rows_specs(TM_PROJ, n_first_tiles, len(xs) == 2) + [
            _layer_spec((1, D_MODEL), lambda i, j: (layer, 0, 0)),
            pl.BlockSpec((pl.Element(1), pl.Element(TN_PROJ), pl.Element(D_MODEL)),
                         lambda i, j: (layer, w_rows(j), 0)),
            _layer_spec((LANES, D_MODEL), lambda i, j: (layer, 0, 0)),
            _layer_spec((LANES, QK_WIDTH), lambda i, j: (layer, 0, 0)),
            _layer_spec((1, QK_WIDTH), lambda i, j: (layer, 0, 0)),
        ],
        out_specs=[
            pl.BlockSpec((TM_PROJ, TN_PROJ), lambda i, j: (i, 0)),
            pl.BlockSpec((TM_PROJ, TN_PROJ), lambda i, j: (i, jnp.maximum(j - 1, 0))),
            pl.BlockSpec((TM_PROJ, QK_WIDTH), lambda i, j: (i, 0)),
        ],
        out_shape=[
            jax.ShapeDtypeStruct((m, TN_PROJ), F32),
            jax.ShapeDtypeStruct((m, VGU_WIDTH), BF16),
            jax.ShapeDtypeStruct((m, QK_WIDTH), F32),
        ],
        scratch_shapes=[pltpu.VMEM((TM_PROJ, D_MODEL), BF16)],
        compiler_params=pltpu.CompilerParams(
            dimension_semantics=("arbitrary", "arbitrary"), vmem_limit_bytes=VMEM_LIMIT_BYTES),
        name="inproj",
    )(*xs, nrm, w_in_t, w_lr_t, w_up, b_gk)


def _pool_output(s, u_cols, pos, gi, wpool_ref, ps_ref):
    cnt = jnp.minimum(POOL_WINDOWS[gi], pos + 1).astype(F32)
    d = s / cnt - u_cols
    cols = slice(gi * POOL_GC, (gi + 1) * POOL_GC)
    return _dot(d.astype(BF16), wpool_ref[gi]) * ps_ref[:, cols]


def _window_sums_doubling(ext_ref, lvl_ref, cols, w, n_rows):
    end = EXT0 + n_rows
    n_lvl = w.bit_length() - 1

    def read(lo, hi):
        return ext_ref[lo:hi, cols]

    for lvl in range(n_lvl):
        sh = 1 << lvl
        lo = EXT0 if lvl == n_lvl - 1 else SUBLANES
        val = read(lo, end) + read(lo - sh, end - sh)
        if lvl == n_lvl - 1:
            return val
        buf = lvl_ref.at[lvl % 2]
        buf[lo:end, :] = val

        def read(lo_, hi_, buf=buf):
            return buf[lo_:hi_, :]


def _gla_chunk(q, k, v, g, st_ref, masks):
    tri, row_id, pair_level, on_diag = masks
    c = GLA_CHUNK
    b = _cumsum_rows(tri, g)

    a = jnp.where(on_diag, jnp.sum(q * k, axis=-1, keepdims=True), 0.0)
    for lvl in range(c.bit_length() - 1):
        s = 1 << lvl
        f = jnp.exp(-jnp.abs(b - _segment_mid_rows(b, row_id, s)))
        upper = jnp.bitwise_and(row_id, s) != 0
        x = f * jnp.where(upper, q, k)
        qt = jnp.where(upper, x, 0.0).astype(BF16)
        kt = jnp.where(upper, 0.0, x).astype(BF16)
        a = jnp.where(pair_level == lvl, _dot_nt(qt, kt), a)

    vb = v.astype(BF16)
    o_intra = _dot(a.astype(BF16), vb)
    qe = (q * jnp.exp(b)).astype(BF16)
    outs = []
    for h in range(GLA_HEADS):
        sl = slice(h * c, (h + 1) * c)
        st = st_ref[h]
        outs.append(o_intra[sl] + _dot_nt(qe[sl], st.astype(BF16)))
        b_last = b[(h + 1) * c - 1:(h + 1) * c, :]
        k_dec = k[sl] * jnp.exp(b_last - b[sl])
        st_ref[h] = st * jnp.exp(b_last) + _dot_tn(vb[sl], k_dec.astype(BF16))
    return outs


def _segment_mid_rows(b, row_id, s):
    c, dk = b.shape
    seg = 2 * s
    if seg >= SUBLANES:
        b3 = b.reshape(c // seg, seg, dk)
        return jnp.broadcast_to(b3[:, s:s + 1, :], b3.shape).reshape(c, dk)
    b3 = b.reshape(c // SUBLANES, SUBLANES, dk)
    place = jnp.bitwise_and(row_id, seg - 1)
    out = b
    for p in range(seg):
        if p != s:
            below = pltpu.roll(b3, (p - s) % SUBLANES, axis=1).reshape(c, dk)
            out = jnp.where(place == p, below, out)
    return out


def _gla_masks():
    n = GLA_HEADS * GLA_CHUNK
    row = lax.broadcasted_iota(jnp.int32, (n, n), 0)
    col = lax.broadcasted_iota(jnp.int32, (n, n), 1)
    top_differing_bit = 31 - lax.clz(jnp.bitwise_xor(row, col))
    same_head = top_differing_bit < GLA_CHUNK.bit_length() - 1
    tri = jnp.logical_and(row >= col, same_head).astype(BF16)
    row_id = lax.broadcasted_iota(jnp.int32, (n, GLA_DK), 0)
    pair_level = jnp.where(col < row, top_differing_bit, -1)
    return tri, row_id, pair_level, row == col


def _gated_head_out(o, gt, gn):
    return _rmsnorm(o, gn) * (gt * jax.nn.sigmoid(gt))


def _mixer_prompt_kernel(qk_ref, vgu_ref, gk_ref, s0_ref, buf0_ref, gn_ref, wpool_ref, ps_ref, *rest,
                         pos0, n_alias, n_w):
    w32_refs = rest[:n_w]
    rest = rest[n_w + n_alias:]
    cat_ref, sout_ref, bufout_ref = rest[:3]
    w16_refs = rest[3:3 + n_w]
    st_ref, ext_ref, lvl_ref = rest[3 + n_w:]
    t = pl.program_id(1)
    nt = pl.num_programs(1)
    tt = TT_PROMPT

    for w32_ref, w16_ref in zip(w32_refs, w16_refs):
        if len(w16_ref.shape) == 2:
            w16_ref[...] = w32_ref[...].astype(BF16)
        else:
            width = w16_ref.shape[2]
            for ct in range(w16_ref.shape[0]):
                w16_ref[ct] = w32_ref[:, ct * width:(ct + 1) * width].astype(BF16)

    @pl.when(t == 0)
    def _():
        for h in range(GLA_HEADS):
            st_ref[h] = s0_ref[0, h].T
        ext_ref[0:SUBLANES, :] = jnp.zeros((SUBLANES, POOL_WIDTH), F32)
        lvl_ref[:, 0:SUBLANES, :] = jnp.zeros((2, SUBLANES, POOL_GC), F32)
        ext_ref[SUBLANES:EXT0, :] = buf0_ref[0]

    ext_ref[EXT0:EXT0 + tt, :] = vgu_ref[:, U_OFF:U_OFF + POOL_WIDTH].astype(F32)
    pos = pos0 + t * tt + lax.broadcasted_iota(jnp.int32, (tt, POOL_GC), 0)
    for gi, w in enumerate(POOL_WINDOWS):
        cols = slice(gi * POOL_GC, (gi + 1) * POOL_GC)
        s = _window_sums_doubling(ext_ref, lvl_ref, cols, w, tt)
        y = _pool_output(s, ext_ref[EXT0:EXT0 + tt, cols], pos, gi, wpool_ref, ps_ref)
        cat_ref[:, GLA_WIDTH + gi * POOL_GC:GLA_WIDTH + (gi + 1) * POOL_GC] = y.astype(BF16)
    ext_ref[SUBLANES:EXT0, :] = ext_ref[tt + SUBLANES:tt + EXT0, :]

    masks = _gla_masks()

    def chunk_body(ci, carry_):
        rows = pl.ds(pl.multiple_of(ci * GLA_CHUNK, GLA_CHUNK), GLA_CHUNK)

        def heads_on_rows(ref, off, width):
            return jnp.concatenate(
                [ref[rows, off + h * width:off + (h + 1) * width] for h in range(GLA_HEADS)], axis=0)

        outs = _gla_chunk(heads_on_rows(qk_ref, 0, GLA_DK) * Q_SCALE,
                          heads_on_rows(qk_ref, K_OFF, GLA_DK),
                          heads_on_rows(vgu_ref, V_OFF, GLA_DV),
                          heads_on_rows(gk_ref, 0, GLA_DK), st_ref, masks)
        for h in range(GLA_HEADS):
            vcols = slice(h * GLA_DV, (h + 1) * GLA_DV)
            gt = vgu_ref[rows, G_OFF + h * GLA_DV:G_OFF + (h + 1) * GLA_DV].astype(F32)
            cat_ref[rows, vcols] = _gated_head_out(outs[h], gt, gn_ref[...]).astype(BF16)
        return carry_

    lax.fori_loop(0, tt // GLA_CHUNK, chunk_body, 0, unroll=True)

    @pl.when(t == nt - 1)
    def _():
        for h in range(GLA_HEADS):
            sout_ref[0, h] = st_ref[h].T
        bufout_ref[0] = ext_ref[EXT0 - POOL_BUF:EXT0, :]


def _mixer_prompt(qk, vgu, gk, s0, buf0, layer, gla_norm, w_pool, pool_scale, ffn_weights, col_tiles,
                  stacked, *, n_seq, seq_len, pos0):
    tt = TT_PROMPT
    nt = seq_len // tt
    n_steps = n_seq * nt
    row = lambda b, t: b * nt + t
    n_alias = len(stacked)
    n_in = 8 + len(ffn_weights)
    w_blocks = [(w.shape[1] // n_steps, w.shape[2]) for w in ffn_weights]
    w_out_specs, w_out_shapes = [], []
    for w, (rows, cols), ct in zip(ffn_weights, w_blocks, col_tiles):
        if ct is None:
            w_out_specs.append(pl.BlockSpec((rows, cols), lambda b, t: (row(b, t), 0)))
            w_out_shapes.append(jax.ShapeDtypeStruct(w.shape[1:], BF16))
        else:
            w_out_specs.append(pl.BlockSpec((cols // ct, rows, ct), lambda b, t: (0, row(b, t), 0)))
            w_out_shapes.append(jax.ShapeDtypeStruct((cols // ct, w.shape[1], ct), BF16))
    return pl.pallas_call(
        functools.partial(_mixer_prompt_kernel, pos0=pos0, n_alias=n_alias, n_w=len(ffn_weights)),
        grid=(n_seq, nt),
        in_specs=[
            pl.BlockSpec((tt, 2 * QK_WIDTH), lambda b, t: (row(b, t), 0)),
            pl.BlockSpec((tt, VGU_WIDTH), lambda b, t: (row(b, t), 0)),
            pl.BlockSpec((tt, QK_WIDTH), lambda b, t: (row(b, t), 0)),
            pl.BlockSpec((1, GLA_HEADS, GLA_DK, GLA_DV), lambda b, t: (b, 0, 0, 0)),
            pl.BlockSpec((1, BUF_ROWS, POOL_WIDTH), lambda b, t: (b, 0, 0)),
            _layer_spec((1, GLA_DV), lambda b, t: (layer, 0, 0)),
            _layer_spec((len(POOL_WINDOWS), POOL_GC, POOL_GC), lambda b, t: (layer, 0, 0, 0)),
            _layer_spec((1, POOL_WIDTH), lambda b, t: (layer, 0, 0)),
        ] + [_layer_spec(blk, lambda b, t: (layer, row(b, t), 0)) for blk in w_blocks]
        + [pl.BlockSpec(memory_space=pl.ANY)] * n_alias,
        out_specs=[
            pl.BlockSpec((tt, D_MODEL), lambda b, t: (row(b, t), 0)),
            _layer_spec((1, GLA_HEADS, GLA_DK, GLA_DV), lambda b, t: (layer, b, 0, 0, 0)),
            _layer_spec((1, POOL_BUF, POOL_WIDTH), lambda b, t: (layer, b, 0, 0)),
        ] + w_out_specs,
        out_shape=[
            jax.ShapeDtypeStruct((qk.shape[0], D_MODEL), BF16),
            jax.ShapeDtypeStruct((DEPTH, n_seq, GLA_HEADS, GLA_DK, GLA_DV), F32),
            jax.ShapeDtypeStruct((DEPTH, n_seq, POOL_BUF, POOL_WIDTH), F32),
        ] + w_out_shapes,
        scratch_shapes=[
            pltpu.VMEM((GLA_HEADS, GLA_DV, GLA_DK), F32),
            pltpu.VMEM((EXT0 + tt, POOL_WIDTH), F32),
            pltpu.VMEM((2, EXT0 + tt, POOL_GC), F32),
        ],
        input_output_aliases={n_in + k: 1 + k for k in range(n_alias)},
        compiler_params=pltpu.CompilerParams(
            dimension_semantics=("arbitrary", "arbitrary"), vmem_limit_bytes=VMEM_LIMIT_BYTES),
        name="mixer_prompt",
    )(qk, vgu, gk, s0, buf0, gla_norm, w_pool, pool_scale, *ffn_weights, *stacked)


def _mixer_sample_kernel(qk_ref, vgu_ref, gk_ref, s0_ref, buf0_ref, gn_ref, wpool_ref, ps_ref, *rest,
                         pos0, t_len, n_alias):
    cat_ref, sout_ref, bufout_ref, ext_ref = rest[n_alias:]
    ns = SEQ_PER_STEP
    rows = ns * t_len

    ext_ref[:, 1:BUF_ROWS, :] = buf0_ref[...]
    ext_ref[:, BUF_ROWS:BUF_ROWS + t_len, :] = (
        vgu_ref[:, U_OFF:U_OFF + POOL_WIDTH].astype(F32).reshape(ns, t_len, POOL_WIDTH))
    pos = pos0 + lax.broadcasted_iota(jnp.int32, (ns, t_len, POOL_GC), 1)
    for gi in range(len(POOL_WINDOWS)):
        cols = slice(gi * POOL_GC, (gi + 1) * POOL_GC)
        u_cols = ext_ref[:, BUF_ROWS:BUF_ROWS + t_len, cols]
        w = POOL_WINDOWS[gi]
        s = u_cols
        for sft in range(1, w):
            s = s + ext_ref[:, BUF_ROWS - sft:BUF_ROWS - sft + t_len, cols]
        cnt = jnp.minimum(w, pos + 1).astype(F32)
        d = (s / cnt - u_cols).reshape(rows, POOL_GC)
        y = _dot(d.astype(BF16), wpool_ref[gi]) * ps_ref[:, cols]
        cat_ref[:, GLA_WIDTH + gi * POOL_GC:GLA_WIDTH + (gi + 1) * POOL_GC] = y.astype(BF16)
    bufout_ref[...] = ext_ref[:, t_len + 1:t_len + BUF_ROWS, :]

    r_i = lax.broadcasted_iota(jnp.int32, (rows, rows), 0)
    c_i = lax.broadcasted_iota(jnp.int32, (rows, rows), 1)
    same_seq = (r_i - jnp.bitwise_and(r_i, t_len - 1)) == (c_i - jnp.bitwise_and(c_i, t_len - 1))
    tri = jnp.logical_and(r_i >= c_i, same_seq).astype(BF16)
    row_in_seq = lax.broadcasted_iota(jnp.int32, (ns, t_len, 1), 1)
    seq_of_row = lax.broadcasted_iota(jnp.int32, (rows, 1), 0) // t_len
    zero_pad = jnp.zeros((LANES - rows, GLA_DK), F32)

    for h in range(GLA_HEADS):
        kcols = slice(h * GLA_DK, (h + 1) * GLA_DK)
        vcols = slice(h * GLA_DV, (h + 1) * GLA_DV)
        qh = qk_ref[:, h * GLA_DK:(h + 1) * GLA_DK] * Q_SCALE
        kh = qk_ref[:, K_OFF + h * GLA_DK:K_OFF + (h + 1) * GLA_DK]
        vb = vgu_ref[:, V_OFF + h * GLA_DV:V_OFF + (h + 1) * GLA_DV]
        vh = vb.astype(F32)
        gt = vgu_ref[:, G_OFF + h * GLA_DV:G_OFF + (h + 1) * GLA_DV].astype(F32)
        b = _cumsum_rows(tri, gk_ref[:, kcols])
        q3 = qh.reshape(ns, t_len, GLA_DK)
        k3 = kh.reshape(ns, t_len, GLA_DK)
        b3 = b.reshape(ns, t_len, GLA_DK)
        v3 = vh.reshape(ns, t_len, GLA_DV)
        o3 = jnp.zeros((ns, t_len, GLA_DV), F32)
        for jj in range(t_len):
            p = q3 * (k3[:, jj:jj + 1, :] * jnp.exp(jnp.minimum(b3 - b3[:, jj:jj + 1, :], 0.0)))
            col = jnp.sum(p, axis=-1, keepdims=True)
            col = jnp.where(row_in_seq >= jj, col, 0.0)
            o3 = o3 + col * v3[:, jj:jj + 1, :]
        o = o3.reshape(rows, GLA_DV)

        qe = qh * jnp.exp(b)
        k_dec = (k3 * jnp.exp(b3[:, t_len - 1:t_len, :] - b3)).reshape(rows, GLA_DK)
        b_t = jnp.concatenate([b, zero_pad], axis=0).T
        for s in range(ns):
            mine = seq_of_row == s
            s0 = s0_ref[s, h]
            o = o + _dot(jnp.where(mine, qe, 0.0).astype(BF16), s0.astype(BF16))
            last = s * t_len + t_len - 1
            a_col = jnp.exp(b_t[:, last:last + 1])
            upd = _dot_tn(jnp.where(mine, k_dec, 0.0).astype(BF16), vb)
            sout_ref[s, h] = a_col * s0 + upd
        cat_ref[:, vcols] = _gated_head_out(o, gt, gn_ref[...]).astype(BF16)


def _mixer_sample(qk, vgu, gk, s_in, buf_in, layer, gla_norm, w_pool, pool_scale, cat, stacked, *,
                  n_seq, t_len, pos0, row0):
    ns = SEQ_PER_STEP
    rows = ns * t_len
    rb0 = row0 // rows
    n_alias = 1 + len(stacked)
    n_in = 8
    return pl.pallas_call(
        functools.partial(_mixer_sample_kernel, pos0=pos0, t_len=t_len, n_alias=n_alias),
        grid=(n_seq // ns,),
        in_specs=[
            pl.BlockSpec((rows, 2 * QK_WIDTH), lambda i: (rb0 + i, 0)),
            pl.BlockSpec((rows, VGU_WIDTH), lambda i: (rb0 + i, 0)),
            pl.BlockSpec((rows, QK_WIDTH), lambda i: (rb0 + i, 0)),
            _layer_spec((ns, GLA_HEADS, GLA_DK, GLA_DV), lambda i: (layer, i, 0, 0, 0)),
            _layer_spec((ns, POOL_BUF, POOL_WIDTH), lambda i: (layer, i, 0, 0)),
            _layer_spec((1, GLA_DV), lambda i: (layer, 0, 0)),
            _layer_spec((len(POOL_WINDOWS), POOL_GC, POOL_GC), lambda i: (layer, 0, 0, 0)),
            _layer_spec((1, POOL_WIDTH), lambda i: (layer, 0, 0)),
        ] + [pl.BlockSpec(memory_space=pl.ANY)] * n_alias,
        out_specs=[
            pl.BlockSpec((rows, D_MODEL), lambda i: (rb0 + i, 0)),
            _layer_spec((ns, GLA_HEADS, GLA_DK, GLA_DV), lambda i: (layer, i, 0, 0, 0)),
            _layer_spec((ns, POOL_BUF, POOL_WIDTH), lambda i: (layer, i, 0, 0)),
        ],
        out_shape=[
            jax.ShapeDtypeStruct(cat.shape, BF16),
            jax.ShapeDtypeStruct((DEPTH, n_seq, GLA_HEADS, GLA_DK, GLA_DV), F32),
            jax.ShapeDtypeStruct((DEPTH, n_seq, POOL_BUF, POOL_WIDTH), F32),
        ],
        scratch_shapes=[pltpu.VMEM((ns, BUF_ROWS + t_len, POOL_WIDTH), F32)],
        input_output_aliases={n_in + k: k for k in range(n_alias)},
        compiler_params=pltpu.CompilerParams(
            dimension_semantics=("arbitrary",), vmem_limit_bytes=VMEM_LIMIT_BYTES),
        name="mixer_sample",
    )(qk, vgu, gk, s_in, buf_in, gla_norm, w_pool, pool_scale, cat, *stacked)


def _ffn_kernel(*refs, n_x, n_o, n_first_tiles, final_norm):
    x_refs = refs[:n_x]
    cat_ref, wout_ref, nrm_ref, wg_hbm, wu_hbm, wd_hbm, nf_ref = refs[n_x:n_x + 7]
    o_refs = refs[n_x + 7:n_x + 7 + n_o]
    h_ref, wg_buf, wu_buf, wd_buf, sem = refs[n_x + 7 + n_o:n_x + 12 + n_o]
    acc_ref = o_refs[0] if n_o == 1 else refs[n_x + 12 + n_o]
    i = pl.program_id(0)
    n_i = pl.num_programs(0)
    nj = wg_hbm.shape[0]

    def tile_copies(jt, slot):
        pairs = ((wg_hbm, wg_buf), (wu_hbm, wu_buf), (wd_hbm, wd_buf))
        return [pltpu.make_async_copy(src.at[jt], buf.at[slot], sem.at[k, slot])
                for k, (src, buf) in enumerate(pairs)]

    @pl.when(i == 0)
    def _():
        for c in tile_copies(0, 0):
            c.start()

    def start(x_ref):
        x2 = x_ref[...] + _dot(cat_ref[...], wout_ref[...])
        h_ref[...] = _rmsnorm(x2, nrm_ref[...]).astype(BF16)
        acc_ref[...] = x2

    _for_row_source(x_refs, n_first_tiles, start)

    def step(jt, carry):
        s = i * nj + jt
        slot = jnp.bitwise_and(s, 1)

        @pl.when(s + 1 < n_i * nj)
        def _():
            for c in tile_copies(jnp.where(jt + 1 < nj, jt + 1, 0), 1 - slot):
                c.start()

        for c in tile_copies(jt, slot):
            c.wait()

        h = h_ref[...]
        gate = _dot(h, wg_buf[slot])
        up = _dot(h, wu_buf[slot])
        act = (gate * jax.nn.sigmoid(gate)) * up
        acc_ref[...] += _dot(act.astype(BF16), wd_buf[slot])
        return carry

    lax.fori_loop(0, nj, step, 0)

    def result():
        return _rmsnorm(acc_ref[...], nf_ref[...]) if final_norm else acc_ref[...]

    if n_o == 1:
        if final_norm:
            acc_ref[...] = result()
    else:
        @pl.when(i < n_first_tiles)
        def _():
            o_refs[0][...] = result()

        @pl.when(i >= n_first_tiles)
        def _():
            acc_ref[...] = result()
            rows = pl.ds(pl.multiple_of((i - n_first_tiles) * TM_FFN, TM_FFN), TM_FFN)
            pltpu.sync_copy(acc_ref, o_refs[1].at[rows, :])


def _ffn(xs, cat, layer, w_out, nrm, w_gate_t, w_up_t, w_down_t, norm_final, *, final_norm, out_rows):
    m = sum(x.shape[0] for x in xs)
    n_first_tiles = (xs[0].shape[0] if len(xs) == 2 else out_rows[0]) // TM_FFN
    last = n_first_tiles - 1
    first_rows = lambda i: (jnp.minimum(i, last), 0)
    second_rows = lambda i: (jnp.maximum(i - n_first_tiles, 0), 0)
    tile = (TM_FFN, D_MODEL)
    if len(xs) == 1:
        x_specs = [pl.BlockSpec(tile, lambda i: (i, 0))]
    else:
        x_specs = [pl.BlockSpec(tile, first_rows),
                   pl.BlockSpec(tile, second_rows, pipeline_mode=pl.Buffered(1))]
    if len(out_rows) == 1:
        out_specs = [pl.BlockSpec(tile, lambda i: (i, 0))]
        acc_scratch = []
    else:
        out_specs = [pl.BlockSpec(tile, first_rows), pl.BlockSpec(memory_space=pl.ANY)]
        acc_scratch = [pltpu.VMEM(tile, F32)]
    return pl.pallas_call(
        functools.partial(_ffn_kernel, n_x=len(xs), n_o=len(out_rows),
                          n_first_tiles=n_first_tiles, final_norm=final_norm),
        grid=(m // TM_FFN,),
        in_specs=x_specs + [
            pl.BlockSpec(tile, lambda i: (i, 0)),
            pl.BlockSpec((D_MODEL, D_MODEL), lambda i: (0, 0), pipeline_mode=pl.Buffered(1)),
            _layer_spec((1, D_MODEL), lambda i: (layer, 0, 0)),
            pl.BlockSpec(memory_space=pl.ANY),
            pl.BlockSpec(memory_space=pl.ANY),
            pl.BlockSpec(memory_space=pl.ANY),
            pl.BlockSpec((1, D_MODEL), lambda i: (0, 0)),
        ],
        out_specs=out_specs,
        out_shape=[jax.ShapeDtypeStruct((r, D_MODEL), F32) for r in out_rows],
        scratch_shapes=[
            pltpu.VMEM(tile, BF16),
            pltpu.VMEM((2, D_MODEL, TF_FFN), BF16),
            pltpu.VMEM((2, D_MODEL, TF_FFN), BF16),
            pltpu.VMEM((2, TF_FFN, D_MODEL), BF16),
            pltpu.SemaphoreType.DMA((3, 2)),
        ] + acc_scratch,
        compiler_params=pltpu.CompilerParams(
            dimension_semantics=("arbitrary",), vmem_limit_bytes=VMEM_LIMIT_BYTES),
        name="ffn",
    )(*xs, cat, w_out, nrm, w_gate_t, w_up_t, w_down_t, norm_final)


def kernel(x_prompt, x_sample, state_gla, state_pool, norm_mix, w_in, w_gk_up, b_gk, gla_norm,
           w_pool, pool_scale, w_out, norm_ffn, w_gate, w_up, w_down, norm_final):
    n_p, t_p, _ = x_prompt.shape
    n_s, t_s, _ = x_sample.shape
    m_p = n_p * t_p
    m_s = n_s * t_s

    w_in_t = jnp.swapaxes(w_in, 1, 2).astype(BF16)
    gate0 = W_IN_GATE0
    w_lr_t = jnp.pad(w_in_t[:, gate0:gate0 + GLA_LOWRANK],
                     ((0, 0), (0, LANES - GLA_LOWRANK), (0, 0)))
    w_upp = jnp.pad(w_gk_up, ((0, 0), (0, LANES - GLA_LOWRANK), (0, 0))).astype(BF16)
    w_pool_b = w_pool.astype(BF16)
    ffn_weights = (w_out, w_gate, w_up, w_down)
    row3 = lambda a: a[:, None, :]
    norm_mix3, norm_ffn3, b_gk3 = row3(norm_mix), row3(norm_ffn), row3(b_gk)
    gla_norm3, pool_scale3 = row3(gla_norm), row3(pool_scale)

    s0_p = jnp.zeros((n_p, GLA_HEADS, GLA_DK, GLA_DV), F32)
    buf0_p = jnp.zeros((n_p, BUF_ROWS, POOL_WIDTH), F32)

    xs = [x_prompt.reshape(m_p, D_MODEL), x_sample.reshape(m_s, D_MODEL)]
    stacked_p, stacked_s = (), ()
    for l in range(DEPTH):
        last = l == DEPTH - 1
        qk, vgu, gk = _inproj(xs, l, norm_mix3, w_in_t, w_lr_t, w_upp, b_gk3)
        cat, *rest = _mixer_prompt(qk, vgu, gk, s0_p, buf0_p, l, gla_norm3, w_pool_b, pool_scale3,
                                   ffn_weights, (None, TF_FFN, TF_FFN, None), stacked_p,
                                   n_seq=n_p, seq_len=t_p, pos0=0)
        stacked_p, (w_out_b, w_gate_b, w_up_b, w_down_b) = rest[:2], rest[2:]
        cat, *stacked_s = _mixer_sample(qk, vgu, gk, state_gla, state_pool, l, gla_norm3, w_pool_b,
                                        pool_scale3, cat, stacked_s, n_seq=n_s, t_len=t_s,
                                        pos0=PAST_LEN, row0=m_p)
        w_down_t = w_down_b.reshape(D_FF // TF_FFN, TF_FFN, D_MODEL)
        xs = _ffn(xs, cat, l, w_out_b, norm_ffn3, w_gate_b, w_up_b, w_down_t, norm_final[None],
                  final_norm=last, out_rows=(m_p, m_s) if last else (m_p + m_s,))

    y_prompt = xs[0].reshape(n_p, t_p, D_MODEL)
    y_sample = xs[1].reshape(n_s, t_s, D_MODEL)
    return (y_prompt, y_sample, stacked_p[0], stacked_p[1], stacked_s[0], stacked_s[1])
```

```python
import functools

import jax
import jax.numpy as jnp
from jax import lax
from jax.experimental import pallas as pl
from jax.experimental.pallas import tpu as pltpu

F32 = jnp.float32
BF16 = jnp.bfloat16

D_MODEL = 2048
DEPTH = 2
EPS = 1e-6
GLA_WIDTH = D_MODEL // 2
GLA_HEADS = 4
GLA_DV = GLA_WIDTH // GLA_HEADS
GLA_DK = GLA_DV // 2
GLA_LOWRANK = 16
GK_NORM = 16.0
GLA_CHUNK = 64
POOL_WIDTH = D_MODEL - GLA_WIDTH
POOL_WINDOWS = (2, 4, 8, 16)
POOL_GC = POOL_WIDTH // len(POOL_WINDOWS)
POOL_BUF = 15
D_FF = ((8 * D_MODEL // 3 + 255) // 256) * 256
QK_WIDTH = GLA_HEADS * GLA_DK
W_IN_GATE0 = 2 * QK_WIDTH + 2 * GLA_WIDTH
MAIN_WIDTH = W_IN_GATE0 + POOL_WIDTH
VGU_WIDTH = MAIN_WIDTH - 2 * QK_WIDTH
K_OFF = QK_WIDTH
V_OFF = 0
G_OFF = GLA_WIDTH
U_OFF = 2 * GLA_WIDTH
Q_SCALE = GLA_DK ** -0.5
PAST_LEN = 16384

LANES = 128
SUBLANES = 8
VMEM_LIMIT_BYTES = 58 * 1024 * 1024

TM_PROJ = 1024
TN_PROJ = 2 * QK_WIDTH
N_ALIGNED_TILES = W_IN_GATE0 // TN_PROJ
PROJ_K_CHUNK = 512
TM_FFN = 512
TF_FFN = 512
FFN_RING = 3
TT_PROMPT = 256
SUB = 16
ROW_CHUNK = 128
SEQ_PER_STEP = 8
BUF_ROWS = POOL_BUF + 1
EXT0 = BUF_ROWS + SUBLANES


def _rmsnorm(x, g):
    r = lax.rsqrt(jnp.mean(x * x, axis=-1, keepdims=True) + EPS)
    return (x * r) * g


def _dot(a, b):
    return jnp.dot(a, b, preferred_element_type=F32)


def _dot_nt(a, b):
    return lax.dot_general(a, b, (((1,), (1,)), ((), ())), preferred_element_type=F32)


def _dot_tn(a, b):
    return lax.dot_general(a, b, (((0,), (0,)), ((), ())), preferred_element_type=F32)


def _split3(x):
    hi = x.astype(BF16)
    r1 = x - hi.astype(F32)
    mid = r1.astype(BF16)
    lo = (r1 - mid.astype(F32)).astype(BF16)
    return hi, mid, lo


def _cumsum_rows(tri, g):
    hi, mid, lo = _split3(g)
    b3 = _dot(tri, jnp.concatenate([hi, mid, lo], axis=1))
    n = g.shape[1]
    return b3[:, :n] + b3[:, n:2 * n] + b3[:, 2 * n:]


def _layer_spec(block, index_map, **kw):
    return pl.BlockSpec((None,) + tuple(block), index_map, **kw)


def _split_rows_specs(tm, n_first_tiles, two_inputs):
    if not two_inputs:
        return [pl.BlockSpec((tm, D_MODEL), lambda i, j: (i, 0))]
    last = n_first_tiles - 1
    return [
        pl.BlockSpec((tm, D_MODEL), lambda i, j: (jnp.minimum(i, last), 0)),
        pl.BlockSpec((tm, D_MODEL), lambda i, j: (jnp.maximum(i - n_first_tiles, 0), 0),
                     pipeline_mode=pl.Buffered(1)),
    ]


def _for_row_source(x_refs, n_first_tiles, fn):
    if len(x_refs) == 1:
        fn(x_refs[0])
        return
    i = pl.program_id(0)
    pl.when(i < n_first_tiles)(lambda: fn(x_refs[0]))
    pl.when(i >= n_first_tiles)(lambda: fn(x_refs[1]))


def _inproj_kernel(*refs, n_x, n_first_tiles):
    x_refs = refs[:n_x]
    nrm_ref, w_ref, wlr_ref, wup_ref, bgk_ref, qk_ref, vgu_ref, gk_ref, h_ref = refs[n_x:]
    j = pl.program_id(1)

    @pl.when(j == 0)
    def _():
        def normalize(x_ref):
            def body(r, carry):
                rows = pl.ds(pl.multiple_of(r * ROW_CHUNK, ROW_CHUNK), ROW_CHUNK)
                h_ref[rows, :] = _rmsnorm(x_ref[rows, :], nrm_ref[...]).astype(BF16)
                return carry

            lax.fori_loop(0, TM_PROJ // ROW_CHUNK, body, 0)

        _for_row_source(x_refs, n_first_tiles, normalize)

        lr = _dot_nt(h_ref[...], wlr_ref[...])
        z = _dot(lr.astype(BF16), wup_ref[...]) + bgk_ref[...]
        gk_ref[...] = jax.nn.log_sigmoid(z) / GK_NORM

    kc = PROJ_K_CHUNK
    t = _dot_nt(h_ref[:, 0:kc], w_ref[0, :, 0:kc])
    for k0 in range(kc, D_MODEL, kc):
        t = t + _dot_nt(h_ref[:, k0:k0 + kc], w_ref[0, :, k0:k0 + kc])

    @pl.when(j == 0)
    def _():
        qk_ref[...] = t

    @pl.when(j > 0)
    def _():
        vgu_ref[...] = t.astype(BF16)


def _inproj(xs, layer, nrm, w_in_t, w_lr_t, w_up, b_gk):
    m = sum(x.shape[0] for x in xs)
    n_first_tiles = xs[0].shape[0] // TM_PROJ
    grid = (m // TM_PROJ, MAIN_WIDTH // TN_PROJ)

    def w_rows(j):
        start = jnp.where(j < N_ALIGNED_TILES, j * TN_PROJ, j * TN_PROJ + GLA_LOWRANK)
        return pl.multiple_of(start, GLA_LOWRANK)

    return pl.pallas_call(
        functools.partial(_inproj_kernel, n_x=len(xs), n_first_tiles=n_first_tiles),
        grid=grid,
        in_specs=_split_rows_specs(TM_PROJ, n_first_tiles, len(xs) == 2) + [
            _layer_spec((1, D_MODEL), lambda i, j: (layer, 0, 0)),
            pl.BlockSpec((pl.Element(1), pl.Element(TN_PROJ), pl.Element(D_MODEL)),
                         lambda i, j: (layer, w_rows(j), 0)),
            _layer_spec((LANES, D_MODEL), lambda i, j: (layer, 0, 0)),
            _layer_spec((LANES, QK_WIDTH), lambda i, j: (layer, 0, 0)),
            _layer_spec((1, QK_WIDTH), lambda i, j: (layer, 0, 0)),
        ],
        out_specs=[
            pl.BlockSpec((TM_PROJ, TN_PROJ), lambda i, j: (i, 0)),
            pl.BlockSpec((TM_PROJ, TN_PROJ), lambda i, j: (i, jnp.maximum(j - 1, 0))),
            pl.BlockSpec((TM_PROJ, QK_WIDTH), lambda i, j: (i, 0)),
        ],
        out_shape=[
            jax.ShapeDtypeStruct((m, TN_PROJ), F32),
            jax.ShapeDtypeStruct((m, VGU_WIDTH), BF16),
            jax.ShapeDtypeStruct((m, QK_WIDTH), F32),
        ],
        scratch_shapes=[pltpu.VMEM((TM_PROJ, D_MODEL), BF16)],
        compiler_params=pltpu.CompilerParams(
            dimension_semantics=("arbitrary", "arbitrary"), vmem_limit_bytes=VMEM_LIMIT_BYTES),
        name="inproj",
    )(*xs, nrm, w_in_t, w_lr_t, w_up, b_gk)


def _pool_output(s, u_cols, pos, gi, wpool_ref, ps_ref):
    cnt = jnp.minimum(POOL_WINDOWS[gi], pos + 1).astype(F32)
    d = s / cnt - u_cols
    cols = slice(gi * POOL_GC, (gi + 1) * POOL_GC)
    return _dot(d.astype(BF16), wpool_ref[gi]) * ps_ref[:, cols]


def _window_sums_doubling(ext_ref, lvl_ref, cols, w, n_rows):
    end = EXT0 + n_rows
    n_lvl = w.bit_length() - 1

    def read(lo, hi):
        return ext_ref[lo:hi, cols]

    for lvl in range(n_lvl):
        sh = 1 << lvl
        lo = EXT0 if lvl == n_lvl - 1 else SUBLANES
        val = read(lo, end) + read(lo - sh, end - sh)
        if lvl == n_lvl - 1:
            return val
        buf = lvl_ref.at[lvl % 2]
        buf[lo:end, :] = val

        def read(lo_, hi_, buf=buf):
            return buf[lo_:hi_, :]


def _gla_chunk(q, k, v, g, st_ref, masks):
    tri, row_id, pair_level, on_diag = masks
    c = GLA_CHUNK
    b = _cumsum_rows(tri, g)

    a = jnp.where(on_diag, jnp.sum(q * k, axis=-1, keepdims=True), 0.0)
    for lvl in range(c.bit_length() - 1):
        s = 1 << lvl
        f = jnp.exp(-jnp.abs(b - _segment_mid_rows(b, row_id, s)))
        upper = jnp.bitwise_and(row_id, s) != 0
        x = f * jnp.where(upper, q, k)
        qt = jnp.where(upper, x, 0.0).astype(BF16)
        kt = jnp.where(upper, 0.0, x).astype(BF16)
        a = jnp.where(pair_level == lvl, _dot_nt(qt, kt), a)

    vb = v.astype(BF16)
    o_intra = _dot(a.astype(BF16), vb)
    qe = (q * jnp.exp(b)).astype(BF16)
    outs = []
    for h in range(GLA_HEADS):
        sl = slice(h * c, (h + 1) * c)
        st = st_ref[h]
        outs.append(o_intra[sl] + _dot_nt(qe[sl], st.astype(BF16)))
        b_last = b[(h + 1) * c - 1:(h + 1) * c, :]
        k_dec = k[sl] * jnp.exp(b_last - b[sl])
        st_ref[h] = st * jnp.exp(b_last) + _dot_tn(vb[sl], k_dec.astype(BF16))
    return outs


def _segment_mid_rows(b, row_id, s):
    c, dk = b.shape
    seg = 2 * s
    if seg >= SUBLANES:
        b3 = b.reshape(c // seg, seg, dk)
        return jnp.broadcast_to(b3[:, s:s + 1, :], b3.shape).reshape(c, dk)
    b3 = b.reshape(c // SUBLANES, SUBLANES, dk)
    place = jnp.bitwise_and(row_id, seg - 1)
    out = b
    for p in range(seg):
        if p != s:
            below = pltpu.roll(b3, (p - s) % SUBLANES, axis=1).reshape(c, dk)
            out = jnp.where(place == p, below, out)
    return out


def _gla_masks():
    n = GLA_HEADS * GLA_CHUNK
    row = lax.broadcasted_iota(jnp.int32, (n, n), 0)
    col = lax.broadcasted_iota(jnp.int32, (n, n), 1)
    top_differing_bit = 31 - lax.clz(jnp.bitwise_xor(row, col))
    same_head = top_differing_bit < GLA_CHUNK.bit_length() - 1
    tri = jnp.logical_and(row >= col, same_head).astype(BF16)
    row_id = lax.broadcasted_iota(jnp.int32, (n, GLA_DK), 0)
    pair_level = jnp.where(col < row, top_differing_bit, -1)
    return tri, row_id, pair_level, row == col


def _gated_head_out(o, gt, gn):
    return _rmsnorm(o, gn) * (gt * jax.nn.sigmoid(gt))


def _mixer_prompt_kernel(qk_ref, vgu_ref, gk_ref, s0_ref, buf0_ref, gn_ref, wpool_ref, ps_ref, *rest,
                         pos0, n_alias, n_w):
    w32_refs = rest[:n_w]
    rest = rest[n_w + n_alias:]
    cat_ref, sout_ref, bufout_ref = rest[:3]
    w16_refs = rest[3:3 + n_w]
    st_ref, ext_ref, lvl_ref = rest[3 + n_w:]
    t = pl.program_id(1)
    nt = pl.num_programs(1)
    tt = TT_PROMPT

    for w32_ref, w16_ref in zip(w32_refs, w16_refs):
        if len(w16_ref.shape) == 2:
            w16_ref[...] = w32_ref[...].astype(BF16)
        else:
            width = w16_ref.shape[2]
            for ct in range(w16_ref.shape[0]):
                w16_ref[ct] = w32_ref[:, ct * width:(ct + 1) * width].astype(BF16)

    @pl.when(t == 0)
    def _():
        for h in range(GLA_HEADS):
            st_ref[h] = s0_ref[0, h].T
        ext_ref[0:SUBLANES, :] = jnp.zeros((SUBLANES, POOL_WIDTH), F32)
        lvl_ref[:, 0:SUBLANES, :] = jnp.zeros((2, SUBLANES, POOL_GC), F32)
        ext_ref[SUBLANES:EXT0, :] = buf0_ref[0]

    ext_ref[EXT0:EXT0 + tt, :] = vgu_ref[:, U_OFF:U_OFF + POOL_WIDTH].astype(F32)
    pos = pos0 + t * tt + lax.broadcasted_iota(jnp.int32, (tt, POOL_GC), 0)
    for gi, w in enumerate(POOL_WINDOWS):
        cols = slice(gi * POOL_GC, (gi + 1) * POOL_GC)
        s = _window_sums_doubling(ext_ref, lvl_ref, cols, w, tt)
        y = _pool_output(s, ext_ref[EXT0:EXT0 + tt, cols], pos, gi, wpool_ref, ps_ref)
        cat_ref[:, GLA_WIDTH + gi * POOL_GC:GLA_WIDTH + (gi + 1) * POOL_GC] = y.astype(BF16)
    ext_ref[SUBLANES:EXT0, :] = ext_ref[tt + SUBLANES:tt + EXT0, :]

    masks = _gla_masks()

    def chunk_body(ci, carry_):
        rows = pl.ds(pl.multiple_of(ci * GLA_CHUNK, GLA_CHUNK), GLA_CHUNK)

        def heads_on_rows(ref, off, width):
            return jnp.concatenate(
                [ref[rows, off + h * width:off + (h + 1) * width] for h in range(GLA_HEADS)], axis=0)

        outs = _gla_chunk(heads_on_rows(qk_ref, 0, GLA_DK) * Q_SCALE,
                          heads_on_rows(qk_ref, K_OFF, GLA_DK),
                          heads_on_rows(vgu_ref, V_OFF, GLA_DV),
                          heads_on_rows(gk_ref, 0, GLA_DK), st_ref, masks)
        for h in range(GLA_HEADS):
            vcols = slice(h * GLA_DV, (h + 1) * GLA_DV)
            gt = vgu_ref[rows, G_OFF + h * GLA_DV:G_OFF + (h + 1) * GLA_DV].astype(F32)
            cat_ref[rows, vcols] = _gated_head_out(outs[h], gt, gn_ref[...]).astype(BF16)
        return carry_

    lax.fori_loop(0, tt // GLA_CHUNK, chunk_body, 0, unroll=True)

    @pl.when(t == nt - 1)
    def _():
        for h in range(GLA_HEADS):
            sout_ref[0, h] = st_ref[h].T
        bufout_ref[0] = ext_ref[EXT0 - POOL_BUF:EXT0, :]


def _mixer_prompt(qk, vgu, gk, s0, buf0, layer, gla_norm, w_pool, pool_scale, ffn_weights, col_tiles,
                  stacked, *, n_seq, seq_len, pos0):
    tt = TT_PROMPT
    nt = seq_len // tt
    n_steps = n_seq * nt
    row = lambda b, t: b * nt + t
    n_alias = len(stacked)
    n_in = 8 + len(ffn_weights)
    w_blocks = [(w.shape[1] // n_steps, w.shape[2]) for w in ffn_weights]
    w_out_specs, w_out_shapes = [], []
    for w, (rows, cols), ct in zip(ffn_weights, w_blocks, col_tiles):
        if ct is None:
            w_out_specs.append(pl.BlockSpec((rows, cols), lambda b, t: (row(b, t), 0)))
            w_out_shapes.append(jax.ShapeDtypeStruct(w.shape[1:], BF16))
        else:
            w_out_specs.append(pl.BlockSpec((cols // ct, rows, ct), lambda b, t: (0, row(b, t), 0)))
            w_out_shapes.append(jax.ShapeDtypeStruct((cols // ct, w.shape[1], ct), BF16))
    return pl.pallas_call(
        functools.partial(_mixer_prompt_kernel, pos0=pos0, n_alias=n_alias, n_w=len(ffn_weights)),
        grid=(n_seq, nt),
        in_specs=[
            pl.BlockSpec((tt, 2 * QK_WIDTH), lambda b, t: (row(b, t), 0)),
            pl.BlockSpec((tt, VGU_WIDTH), lambda b, t: (row(b, t), 0)),
            pl.BlockSpec((tt, QK_WIDTH), lambda b, t: (row(b, t), 0)),
            pl.BlockSpec((1, GLA_HEADS, GLA_DK, GLA_DV), lambda b, t: (b, 0, 0, 0)),
            pl.BlockSpec((1, BUF_ROWS, POOL_WIDTH), lambda b, t: (b, 0, 0)),
            _layer_spec((1, GLA_DV), lambda b, t: (layer, 0, 0)),
            _layer_spec((len(POOL_WINDOWS), POOL_GC, POOL_GC), lambda b, t: (layer, 0, 0, 0)),
            _layer_spec((1, POOL_WIDTH), lambda b, t: (layer, 0, 0)),
        ] + [_layer_spec(blk, lambda b, t: (layer, row(b, t), 0)) for blk in w_blocks]
        + [pl.BlockSpec(memory_space=pl.ANY)] * n_alias,
        out_specs=[
            pl.BlockSpec((tt, D_MODEL), lambda b, t: (row(b, t), 0)),
            _layer_spec((1, GLA_HEADS, GLA_DK, GLA_DV), lambda b, t: (layer, b, 0, 0, 0)),
            _layer_spec((1, POOL_BUF, POOL_WIDTH), lambda b, t: (layer, b, 0, 0)),
        ] + w_out_specs,
        out_shape=[
            jax.ShapeDtypeStruct((qk.shape[0], D_MODEL), BF16),
            jax.ShapeDtypeStruct((DEPTH, n_seq, GLA_HEADS, GLA_DK, GLA_DV), F32),
            jax.ShapeDtypeStruct((DEPTH, n_seq, POOL_BUF, POOL_WIDTH), F32),
        ] + w_out_shapes,
        scratch_shapes=[
            pltpu.VMEM((GLA_HEADS, GLA_DV, GLA_DK), F32),
            pltpu.VMEM((EXT0 + tt, POOL_WIDTH), F32),
            pltpu.VMEM((2, EXT0 + tt, POOL_GC), F32),
        ],
        input_output_aliases={n_in + k: 1 + k for k in range(n_alias)},
        compiler_params=pltpu.CompilerParams(
            dimension_semantics=("arbitrary", "arbitrary"), vmem_limit_bytes=VMEM_LIMIT_BYTES),
        name="mixer_prompt",
    )(qk, vgu, gk, s0, buf0, gla_norm, w_pool, pool_scale, *ffn_weights, *stacked)


def _mixer_sample_kernel(qk_ref, vgu_ref, gk_ref, s0_ref, buf0_ref, gn_ref, wpool_ref, ps_ref, *rest,
                         pos0, t_len, n_alias):
    cat_ref, sout_ref, bufout_ref, ext_ref = rest[n_alias:]
    ns = SEQ_PER_STEP
    rows = ns * t_len

    ext_ref[:, 1:BUF_ROWS, :] = buf0_ref[...]
    ext_ref[:, BUF_ROWS:BUF_ROWS + t_len, :] = (
        vgu_ref[:, U_OFF:U_OFF + POOL_WIDTH].astype(F32).reshape(ns, t_len, POOL_WIDTH))
    pos = pos0 + lax.broadcasted_iota(jnp.int32, (ns, t_len, POOL_GC), 1)
    for gi in range(len(POOL_WINDOWS)):
        cols = slice(gi * POOL_GC, (gi + 1) * POOL_GC)
        u_cols = ext_ref[:, BUF_ROWS:BUF_ROWS + t_len, cols]
        w = POOL_WINDOWS[gi]
        s = u_cols
        for sft in range(1, w):
            s = s + ext_ref[:, BUF_ROWS - sft:BUF_ROWS - sft + t_len, cols]
        cnt = jnp.minimum(w, pos + 1).astype(F32)
        d = (s / cnt - u_cols).reshape(rows, POOL_GC)
        y = _dot(d.astype(BF16), wpool_ref[gi]) * ps_ref[:, cols]
        cat_ref[:, GLA_WIDTH + gi * POOL_GC:GLA_WIDTH + (gi + 1) * POOL_GC] = y.astype(BF16)
    bufout_ref[...] = ext_ref[:, t_len + 1:t_len + BUF_ROWS, :]

    r_i = lax.broadcasted_iota(jnp.int32, (rows, rows), 0)
    c_i = lax.broadcasted_iota(jnp.int32, (rows, rows), 1)
    same_seq = (r_i - jnp.bitwise_and(r_i, t_len - 1)) == (c_i - jnp.bitwise_and(c_i, t_len - 1))
    tri = jnp.logical_and(r_i >= c_i, same_seq).astype(BF16)
    row_in_seq = lax.broadcasted_iota(jnp.int32, (ns, t_len, 1), 1)
    seq_of_row = lax.broadcasted_iota(jnp.int32, (rows, 1), 0) // t_len
    zero_pad = jnp.zeros((LANES - rows, GLA_DK), F32)

    for h in range(GLA_HEADS):
        kcols = slice(h * GLA_DK, (h + 1) * GLA_DK)
        vcols = slice(h * GLA_DV, (h + 1) * GLA_DV)
        qh = qk_ref[:, h * GLA_DK:(h + 1) * GLA_DK] * Q_SCALE
        kh = qk_ref[:, K_OFF + h * GLA_DK:K_OFF + (h + 1) * GLA_DK]
        vb = vgu_ref[:, V_OFF + h * GLA_DV:V_OFF + (h + 1) * GLA_DV]
        vh = vb.astype(F32)
        gt = vgu_ref[:, G_OFF + h * GLA_DV:G_OFF + (h + 1) * GLA_DV].astype(F32)
        b = _cumsum_rows(tri, gk_ref[:, kcols])
        q3 = qh.reshape(ns, t_len, GLA_DK)
        k3 = kh.reshape(ns, t_len, GLA_DK)
        b3 = b.reshape(ns, t_len, GLA_DK)
        v3 = vh.reshape(ns, t_len, GLA_DV)
        o3 = jnp.zeros((ns, t_len, GLA_DV), F32)
        for jj in range(t_len):
            p = q3 * (k3[:, jj:jj + 1, :] * jnp.exp(jnp.minimum(b3 - b3[:, jj:jj + 1, :], 0.0)))
            col = jnp.sum(p, axis=-1, keepdims=True)
            col = jnp.where(row_in_seq >= jj, col, 0.0)
            o3 = o3 + col * v3[:, jj:jj + 1, :]
        o = o3.reshape(rows, GLA_DV)

        qe = qh * jnp.exp(b)
        k_dec = (k3 * jnp.exp(b3[:, t_len - 1:t_len, :] - b3)).reshape(rows, GLA_DK)
        b_t = jnp.concatenate([b, zero_pad], axis=0).T
        for s in range(ns):
            mine = seq_of_row == s
            s0 = s0_ref[s, h]
            o = o + _dot(jnp.where(mine, qe, 0.0).astype(BF16), s0.astype(BF16))
            last = s * t_len + t_len - 1
            a_col = jnp.exp(b_t[:, last:last + 1])
            upd = _dot_tn(jnp.where(mine, k_dec, 0.0).astype(BF16), vb)
            sout_ref[s, h] = a_col * s0 + upd
        cat_ref[:, vcols] = _gated_head_out(o, gt, gn_ref[...]).astype(BF16)


def _mixer_sample(qk, vgu, gk, s_in, buf_in, layer, gla_norm, w_pool, pool_scale, cat, stacked, *,
                  n_seq, t_len, pos0, row0):
    ns = SEQ_PER_STEP
    rows = ns * t_len
    rb0 = row0 // rows
    n_alias = 1 + len(stacked)
    n_in = 8
    return pl.pallas_call(
        functools.partial(_mixer_sample_kernel, pos0=pos0, t_len=t_len, n_alias=n_alias),
        grid=(n_seq // ns,),
        in_specs=[
            pl.BlockSpec((rows, 2 * QK_WIDTH), lambda i: (rb0 + i, 0)),
            pl.BlockSpec((rows, VGU_WIDTH), lambda i: (rb0 + i, 0)),
            pl.BlockSpec((rows, QK_WIDTH), lambda i: (rb0 + i, 0)),
            _layer_spec((ns, GLA_HEADS, GLA_DK, GLA_DV), lambda i: (layer, i, 0, 0, 0)),
            _layer_spec((ns, POOL_BUF, POOL_WIDTH), lambda i: (layer, i, 0, 0)),
            _layer_spec((1, GLA_DV), lambda i: (layer, 0, 0)),
            _layer_spec((len(POOL_WINDOWS), POOL_GC, POOL_GC), lambda i: (layer, 0, 0, 0)),
            _layer_spec((1, POOL_WIDTH), lambda i: (layer, 0, 0)),
        ] + [pl.BlockSpec(memory_space=pl.ANY)] * n_alias,
        out_specs=[
            pl.BlockSpec((rows, D_MODEL), lambda i: (rb0 + i, 0)),
            _layer_spec((ns, GLA_HEADS, GLA_DK, GLA_DV), lambda i: (layer, i, 0, 0, 0)),
            _layer_spec((ns, POOL_BUF, POOL_WIDTH), lambda i: (layer, i, 0, 0)),
        ],
        out_shape=[
            jax.ShapeDtypeStruct(cat.shape, BF16),
            jax.ShapeDtypeStruct((DEPTH, n_seq, GLA_HEADS, GLA_DK, GLA_DV), F32),
            jax.ShapeDtypeStruct((DEPTH, n_seq, POOL_BUF, POOL_WIDTH), F32),
        ],
        scratch_shapes=[pltpu.VMEM((ns, BUF_ROWS + t_len, POOL_WIDTH), F32)],
        input_output_aliases={n_in + k: k for k in range(n_alias)},
        compiler_params=pltpu.CompilerParams(
            dimension_semantics=("arbitrary",), vmem_limit_bytes=VMEM_LIMIT_BYTES),
        name="mixer_sample",
    )(qk, vgu, gk, s_in, buf_in, gla_norm, w_pool, pool_scale, cat, *stacked)


def _ffn_kernel(*refs, n_x, n_o, n_first_tiles, final_norm):
    x_refs = refs[:n_x]
    cat_ref, wout_ref, nrm_ref, wg_hbm, wu_hbm, wd_hbm, nf_ref = refs[n_x:n_x + 7]
    o_refs = refs[n_x + 7:n_x + 7 + n_o]
    h_ref, wg_buf, wu_buf, wd_buf, sem = refs[n_x + 7 + n_o:n_x + 12 + n_o]
    acc_ref = o_refs[0] if n_o == 1 else refs[n_x + 12 + n_o]
    i = pl.program_id(0)
    n_i = pl.num_programs(0)
    nj = wg_hbm.shape[0]

    def tile_copies(jt, slot):
        pairs = ((wg_hbm, wg_buf), (wu_hbm, wu_buf), (wd_hbm, wd_buf))
        return [pltpu.make_async_copy(src.at[jt], buf.at[slot], sem.at[k, slot])
                for k, (src, buf) in enumerate(pairs)]

    @pl.when(i == 0)
    def _():
        for s0 in range(FFN_RING - 1):
            for c in tile_copies(s0, s0):
                c.start()

    def start(x_ref):
        x2 = x_ref[...] + _dot(cat_ref[...], wout_ref[...])
        h_ref[...] = _rmsnorm(x2, nrm_ref[...]).astype(BF16)
        acc_ref[...] = x2

    _for_row_source(x_refs, n_first_tiles, start)

    def step(jt, carry):
        s = i * nj + jt
        slot = lax.rem(s, FFN_RING)
        ahead = FFN_RING - 1

        @pl.when(s + ahead < n_i * nj)
        def _():
            for c in tile_copies(lax.rem(jt + ahead, nj), lax.rem(s + ahead, FFN_RING)):
                c.start()

        for c in tile_copies(jt, slot):
            c.wait()

        h = h_ref[...]
        gate = _dot(h, wg_buf[slot])
        up = _dot(h, wu_buf[slot])
        act = (gate * jax.nn.sigmoid(gate)) * up
        acc_ref[...] += _dot(act.astype(BF16), wd_buf[slot])
        return carry

    lax.fori_loop(0, nj, step, 0)

    def result():
        return _rmsnorm(acc_ref[...], nf_ref[...]) if final_norm else acc_ref[...]

    if n_o == 1:
        if final_norm:
            acc_ref[...] = result()
    else:
        @pl.when(i < n_first_tiles)
        def _():
            o_refs[0][...] = result()

        @pl.when(i >= n_first_tiles)
        def _():
            acc_ref[...] = result()
            rows = pl.ds(pl.multiple_of((i - n_first_tiles) * TM_FFN, TM_FFN), TM_FFN)
            pltpu.sync_copy(acc_ref, o_refs[1].at[rows, :])


def _ffn(xs, cat, layer, w_out, nrm, w_gate_t, w_up_t, w_down_t, norm_final, *, final_norm, out_rows):
    m = sum(x.shape[0] for x in xs)
    n_first_tiles = (xs[0].shape[0] if len(xs) == 2 else out_rows[0]) // TM_FFN
    last = n_first_tiles - 1
    first_rows = lambda i: (jnp.minimum(i, last), 0)
    second_rows = lambda i: (jnp.maximum(i - n_first_tiles, 0), 0)
    tile = (TM_FFN, D_MODEL)
    if len(xs) == 1:
        x_specs = [pl.BlockSpec(tile, lambda i: (i, 0))]
    else:
        x_specs = [pl.BlockSpec(tile, first_rows),
                   pl.BlockSpec(tile, second_rows, pipeline_mode=pl.Buffered(1))]
    if len(out_rows) == 1:
        out_specs = [pl.BlockSpec(tile, lambda i: (i, 0))]
        acc_scratch = []
    else:
        out_specs = [pl.BlockSpec(tile, first_rows), pl.BlockSpec(memory_space=pl.ANY)]
        acc_scratch = [pltpu.VMEM(tile, F32)]
    return pl.pallas_call(
        functools.partial(_ffn_kernel, n_x=len(xs), n_o=len(out_rows),
                          n_first_tiles=n_first_tiles, final_norm=final_norm),
        grid=(m // TM_FFN,),
        in_specs=x_specs + [
            pl.BlockSpec(tile, lambda i: (i, 0)),
            pl.BlockSpec((D_MODEL, D_MODEL), lambda i: (0, 0), pipeline_mode=pl.Buffered(1)),
            _layer_spec((1, D_MODEL), lambda i: (layer, 0, 0)),
            pl.BlockSpec(memory_space=pl.ANY),
            pl.BlockSpec(memory_space=pl.ANY),
            pl.BlockSpec(memory_space=pl.ANY),
            pl.BlockSpec((1, D_MODEL), lambda i: (0, 0)),
        ],
        out_specs=out_specs,
        out_shape=[jax.ShapeDtypeStruct((r, D_MODEL), F32) for r in out_rows],
        scratch_shapes=[
            pltpu.VMEM(tile, BF16),
            pltpu.VMEM((FFN_RING, D_MODEL, TF_FFN), BF16),
            pltpu.VMEM((FFN_RING, D_MODEL, TF_FFN), BF16),
            pltpu.VMEM((FFN_RING, TF_FFN, D_MODEL), BF16),
            pltpu.SemaphoreType.DMA((3, FFN_RING)),
        ] + acc_scratch,
        compiler_params=pltpu.CompilerParams(
            dimension_semantics=("arbitrary",), vmem_limit_bytes=VMEM_LIMIT_BYTES),
        name="ffn",
    )(*xs, cat, w_out, nrm, w_gate_t, w_up_t, w_down_t, norm_final)


def kernel(x_prompt, x_sample, state_gla, state_pool, norm_mix, w_in, w_gk_up, b_gk, gla_norm,
           w_pool, pool_scale, w_out, norm_ffn, w_gate, w_up, w_down, norm_final):
    n_p, t_p, _ = x_prompt.shape
    n_s, t_s, _ = x_sample.shape
    m_p = n_p * t_p
    m_s = n_s * t_s

    w_in_t = jnp.swapaxes(w_in, 1, 2).astype(BF16)
    gate0 = W_IN_GATE0
    w_lr_t = jnp.pad(w_in_t[:, gate0:gate0 + GLA_LOWRANK],
                     ((0, 0), (0, LANES - GLA_LOWRANK), (0, 0)))
    w_upp = jnp.pad(w_gk_up, ((0, 0), (0, LANES - GLA_LOWRANK), (0, 0))).astype(BF16)
    w_pool_b = w_pool.astype(BF16)
    ffn_weights = (w_out, w_gate, w_up, w_down)
    row3 = lambda a: a[:, None, :]
    norm_mix3, norm_ffn3, b_gk3 = row3(norm_mix), row3(norm_ffn), row3(b_gk)
    gla_norm3, pool_scale3 = row3(gla_norm), row3(pool_scale)

    s0_p = jnp.zeros((n_p, GLA_HEADS, GLA_DK, GLA_DV), F32)
    buf0_p = jnp.zeros((n_p, BUF_ROWS, POOL_WIDTH), F32)

    xs = [x_prompt.reshape(m_p, D_MODEL), x_sample.reshape(m_s, D_MODEL)]
    stacked_p, stacked_s = (), ()
    for l in range(DEPTH):
        last = l == DEPTH - 1
        qk, vgu, gk = _inproj(xs, l, norm_mix3, w_in_t, w_lr_t, w_upp, b_gk3)
        cat, *rest = _mixer_prompt(qk, vgu, gk, s0_p, buf0_p, l, gla_norm3, w_pool_b, pool_scale3,
                                   ffn_weights, (None, TF_FFN, TF_FFN, None), stacked_p,
                                   n_seq=n_p, seq_len=t_p, pos0=0)
        stacked_p, (w_out_b, w_gate_b, w_up_b, w_down_b) = rest[:2], rest[2:]
        cat, *stacked_s = _mixer_sample(qk, vgu, gk, state_gla, state_pool, l, gla_norm3, w_pool_b,
                                        pool_scale3, cat, stacked_s, n_seq=n_s, t_len=t_s,
                                        pos0=PAST_LEN, row0=m_p)
        w_down_t = w_down_b.reshape(D_FF // TF_FFN, TF_FFN, D_MODEL)
        xs = _ffn(xs, cat, l, w_out_b, norm_ffn3, w_gate_b, w_up_b, w_down_t, norm_final[None],
                  final_norm=last, out_rows=(m_p, m_s) if last else (m_p + m_s,))

    y_prompt = xs[0].reshape(n_p, t_p, D_MODEL)
    y_sample = xs[1].reshape(n_s, t_s, D_MODEL)
    return (y_prompt, y_sample, stacked_p[0], stacked_p[1], stacked_s[0], stacked_s[1])
```

```python
import functools

import jax
import jax.numpy as jnp
from jax import lax
from jax.experimental import pallas as pl
from jax.experimental.pallas import tpu as pltpu

F32 = jnp.float32
BF16 = jnp.bfloat16

D_MODEL = 2048
DEPTH = 2
EPS = 1e-6
GLA_WIDTH = D_MODEL // 2
GLA_HEADS = 4
GLA_DV = GLA_WIDTH // GLA_HEADS
GLA_DK = GLA_DV // 2
GLA_LOWRANK = 16
GK_NORM = 16.0
GLA_CHUNK = 64
POOL_WIDTH = D_MODEL - GLA_WIDTH
POOL_WINDOWS = (2, 4, 8, 16)
POOL_GC = POOL_WIDTH // len(POOL_WINDOWS)
POOL_BUF = 15
D_FF = ((8 * D_MODEL // 3 + 255) // 256) * 256
QK_WIDTH = GLA_HEADS * GLA_DK
W_IN_GATE0 = 2 * QK_WIDTH + 2 * GLA_WIDTH
MAIN_WIDTH = W_IN_GATE0 + POOL_WIDTH
VGU_WIDTH = MAIN_WIDTH - 2 * QK_WIDTH
K_OFF = QK_WIDTH
V_OFF = 0
G_OFF = GLA_WIDTH
U_OFF = 2 * GLA_WIDTH
Q_SCALE = GLA_DK ** -0.5
PAST_LEN = 16384

LANES = 128
SUBLANES = 8
VMEM_LIMIT_BYTES = 58 * 1024 * 1024

TM_PROJ = 1024
TN_PROJ = 2 * QK_WIDTH
N_ALIGNED_TILES = W_IN_GATE0 // TN_PROJ
PROJ_K_CHUNK = 512
TM_FFN = 512
TF_FFN = 512
FFN_RING = 2
TT_PROMPT = 512
SUB = 16
ROW_CHUNK = 128
SEQ_PER_STEP = 8
BUF_ROWS = POOL_BUF + 1
EXT0 = BUF_ROWS + SUBLANES


def _rmsnorm(x, g):
    r = lax.rsqrt(jnp.mean(x * x, axis=-1, keepdims=True) + EPS)
    return (x * r) * g


def _dot(a, b):
    return jnp.dot(a, b, preferred_element_type=F32)


def _dot_nt(a, b):
    return lax.dot_general(a, b, (((1,), (1,)), ((), ())), preferred_element_type=F32)


def _dot_tn(a, b):
    return lax.dot_general(a, b, (((0,), (0,)), ((), ())), preferred_element_type=F32)


def _split3(x):
    hi = x.astype(BF16)
    r1 = x - hi.astype(F32)
    mid = r1.astype(BF16)
    lo = (r1 - mid.astype(F32)).astype(BF16)
    return hi, mid, lo


def _cumsum_rows(tri, g):
    hi, mid, lo = _split3(g)
    b3 = _dot(tri, jnp.concatenate([hi, mid, lo], axis=1))
    n = g.shape[1]
    return b3[:, :n] + b3[:, n:2 * n] + b3[:, 2 * n:]


def _layer_spec(block, index_map, **kw):
    return pl.BlockSpec((None,) + tuple(block), index_map, **kw)


def _split_rows_specs(tm, n_first_tiles, two_inputs):
    if not two_inputs:
        return [pl.BlockSpec((tm, D_MODEL), lambda i, j: (i, 0))]
    last = n_first_tiles - 1
    return [
        pl.BlockSpec((tm, D_MODEL), lambda i, j: (jnp.minimum(i, last), 0)),
        pl.BlockSpec((tm, D_MODEL), lambda i, j: (jnp.maximum(i - n_first_tiles, 0), 0),
                     pipeline_mode=pl.Buffered(1)),
    ]


def _for_row_source(x_refs, n_first_tiles, fn):
    if len(x_refs) == 1:
        fn(x_refs[0])
        return
    i = pl.program_id(0)
    pl.when(i < n_first_tiles)(lambda: fn(x_refs[0]))
    pl.when(i >= n_first_tiles)(lambda: fn(x_refs[1]))


def _inproj_kernel(*refs, n_x, n_first_tiles):
    x_refs = refs[:n_x]
    nrm_ref, w_ref, wlr_ref, wup_ref, bgk_ref, qk_ref, vgu_ref, gk_ref, h_ref = refs[n_x:]
    j = pl.program_id(1)

    @pl.when(j == 0)
    def _():
        def normalize(x_ref):
            def body(r, carry):
                rows = pl.ds(pl.multiple_of(r * ROW_CHUNK, ROW_CHUNK), ROW_CHUNK)
                h_ref[rows, :] = _rmsnorm(x_ref[rows, :], nrm_ref[...]).astype(BF16)
                return carry

            lax.fori_loop(0, TM_PROJ // ROW_CHUNK, body, 0)

        _for_row_source(x_refs, n_first_tiles, normalize)

        lr = _dot_nt(h_ref[...], wlr_ref[...])
        z = _dot(lr.astype(BF16), wup_ref[...]) + bgk_ref[...]
        gk_ref[...] = jax.nn.log_sigmoid(z) / GK_NORM

    kc = PROJ_K_CHUNK
    t = _dot_nt(h_ref[:, 0:kc], w_ref[0, :, 0:kc])
    for k0 in range(kc, D_MODEL, kc):
        t = t + _dot_nt(h_ref[:, k0:k0 + kc], w_ref[0, :, k0:k0 + kc])

    @pl.when(j == 0)
    def _():
        qk_ref[...] = t

    @pl.when(j > 0)
    def _():
        vgu_ref[...] = t.astype(BF16)


def _inproj(xs, layer, nrm, w_in_t, w_lr_t, w_up, b_gk):
    m = sum(x.shape[0] for x in xs)
    n_first_tiles = xs[0].shape[0] // TM_PROJ
    grid = (m // TM_PROJ, MAIN_WIDTH // TN_PROJ)

    def w_rows(j):
        start = jnp.where(j < N_ALIGNED_TILES, j * TN_PROJ, j * TN_PROJ + GLA_LOWRANK)
        return pl.multiple_of(start, GLA_LOWRANK)

    return pl.pallas_call(
        functools.partial(_inproj_kernel, n_x=len(xs), n_first_tiles=n_first_tiles),
        grid=grid,
        in_specs=_split_rows_specs(TM_PROJ, n_first_tiles, len(xs) == 2) + [
            _layer_spec((1, D_MODEL), lambda i, j: (layer, 0, 0)),
            pl.BlockSpec((pl.Element(1), pl.Element(TN_PROJ), pl.Element(D_MODEL)),
                         lambda i, j: (layer, w_rows(j), 0)),
            _layer_spec((LANES, D_MODEL), lambda i, j: (layer, 0, 0)),
            _layer_spec((LANES, QK_WIDTH), lambda i, j: (layer, 0, 0)),
            _layer_spec((1, QK_WIDTH), lambda i, j: (layer, 0, 0)),
        ],
        out_specs=[
            pl.BlockSpec((TM_PROJ, TN_PROJ), lambda i, j: (i, 0)),
            pl.BlockSpec((TM_PROJ, TN_PROJ), lambda i, j: (i, jnp.maximum(j - 1, 0))),
            pl.BlockSpec((TM_PROJ, QK_WIDTH), lambda i, j: (i, 0)),
        ],
        out_shape=[
            jax.ShapeDtypeStruct((m, TN_PROJ), F32),
            jax.ShapeDtypeStruct((m, VGU_WIDTH), BF16),
            jax.ShapeDtypeStruct((m, QK_WIDTH), F32),
        ],
        scratch_shapes=[pltpu.VMEM((TM_PROJ, D_MODEL), BF16)],
        compiler_params=pltpu.CompilerParams(
            dimension_semantics=("arbitrary", "arbitrary"), vmem_limit_bytes=VMEM_LIMIT_BYTES),
        name="inproj",
    )(*xs, nrm, w_in_t, w_lr_t, w_up, b_gk)


def _pool_output(s, u_cols, pos, gi, wpool_ref, ps_ref):
    cnt = jnp.minimum(POOL_WINDOWS[gi], pos + 1).astype(F32)
    d = s / cnt - u_cols
    cols = slice(gi * POOL_GC, (gi + 1) * POOL_GC)
    return _dot(d.astype(BF16), wpool_ref[gi]) * ps_ref[:, cols]


def _window_sums_doubling(ext_ref, lvl_ref, cols, w, n_rows):
    end = EXT0 + n_rows
    n_lvl = w.bit_length() - 1

    def read(lo, hi):
        return ext_ref[lo:hi, cols]

    for lvl in range(n_lvl):
        sh = 1 << lvl
        lo = EXT0 if lvl == n_lvl - 1 else SUBLANES
        val = read(lo, end) + read(lo - sh, end - sh)
        if lvl == n_lvl - 1:
            return val
        buf = lvl_ref.at[lvl % 2]
        buf[lo:end, :] = val

        def read(lo_, hi_, buf=buf):
            return buf[lo_:hi_, :]


def _gla_chunk(q, k, v, g, st_ref, masks):
    tri, row_id, pair_level, on_diag = masks
    c = GLA_CHUNK
    b = _cumsum_rows(tri, g)

    a = jnp.where(on_diag, jnp.sum(q * k, axis=-1, keepdims=True), 0.0)
    for lvl in range(c.bit_length() - 1):
        f = jnp.exp(-jnp.abs(b - _segment_mid_rows(b, row_id, 1 << lvl)))
        scores = _dot_nt((f * q).astype(BF16), (f * k).astype(BF16))
        a = jnp.where(pair_level == lvl, scores, a)

    vb = v.astype(BF16)
    o_intra = _dot(a.astype(BF16), vb)
    qe = (q * jnp.exp(b)).astype(BF16)
    outs = []
    for h in range(GLA_HEADS):
        sl = slice(h * c, (h + 1) * c)
        st = st_ref[h]
        outs.append(o_intra[sl] + _dot_nt(qe[sl], st.astype(BF16)))
        b_last = b[(h + 1) * c - 1:(h + 1) * c, :]
        k_dec = k[sl] * jnp.exp(b_last - b[sl])
        st_ref[h] = st * jnp.exp(b_last) + _dot_tn(vb[sl], k_dec.astype(BF16))
    return outs


def _segment_mid_rows(b, row_id, s):
    c, dk = b.shape
    seg = 2 * s
    if seg >= SUBLANES:
        b3 = b.reshape(c // seg, seg, dk)
        return jnp.broadcast_to(b3[:, s:s + 1, :], b3.shape).reshape(c, dk)
    b3 = b.reshape(c // SUBLANES, SUBLANES, dk)
    place = jnp.bitwise_and(row_id, seg - 1)
    out = b
    for p in range(seg):
        if p != s:
            below = pltpu.roll(b3, (p - s) % SUBLANES, axis=1).reshape(c, dk)
            out = jnp.where(place == p, below, out)
    return out


def _gla_masks():
    n = GLA_HEADS * GLA_CHUNK
    row = lax.broadcasted_iota(jnp.int32, (n, n), 0)
    col = lax.broadcasted_iota(jnp.int32, (n, n), 1)
    top_differing_bit = 31 - lax.clz(jnp.bitwise_xor(row, col))
    same_head = top_differing_bit < GLA_CHUNK.bit_length() - 1
    tri = jnp.logical_and(row >= col, same_head).astype(BF16)
    row_id = lax.broadcasted_iota(jnp.int32, (n, GLA_DK), 0)
    pair_level = jnp.where(col < row, top_differing_bit, -1)
    return tri, row_id, pair_level, row == col


def _gated_head_out(o, gt, gn):
    return _rmsnorm(o, gn) * (gt * jax.nn.sigmoid(gt))


def _mixer_prompt_kernel(qk_ref, vgu_ref, gk_ref, s0_ref, buf0_ref, gn_ref, wpool_ref, ps_ref, *rest,
                         pos0, n_alias, n_w):
    w32_refs = rest[:n_w]
    rest = rest[n_w + n_alias:]
    cat_ref, sout_ref, bufout_ref = rest[:3]
    w16_refs = rest[3:3 + n_w]
    st_ref, ext_ref, lvl_ref = rest[3 + n_w:]
    t = pl.program_id(1)
    nt = pl.num_programs(1)
    tt = TT_PROMPT

    for w32_ref, w16_ref in zip(w32_refs, w16_refs):
        if len(w16_ref.shape) == 2:
            w16_ref[...] = w32_ref[...].astype(BF16)
        else:
            width = w16_ref.shape[2]
            for ct in range(w16_ref.shape[0]):
                w16_ref[ct] = w32_ref[:, ct * width:(ct + 1) * width].astype(BF16)

    @pl.when(t == 0)
    def _():
        for h in range(GLA_HEADS):
            st_ref[h] = s0_ref[0, h].T
        ext_ref[0:SUBLANES, :] = jnp.zeros((SUBLANES, POOL_WIDTH), F32)
        lvl_ref[:, 0:SUBLANES, :] = jnp.zeros((2, SUBLANES, POOL_GC), F32)
        ext_ref[SUBLANES:EXT0, :] = buf0_ref[0]

    ext_ref[EXT0:EXT0 + tt, :] = vgu_ref[:, U_OFF:U_OFF + POOL_WIDTH].astype(F32)
    pos = pos0 + t * tt + lax.broadcasted_iota(jnp.int32, (tt, POOL_GC), 0)
    for gi, w in enumerate(POOL_WINDOWS):
        cols = slice(gi * POOL_GC, (gi + 1) * POOL_GC)
        s = _window_sums_doubling(ext_ref, lvl_ref, cols, w, tt)
        y = _pool_output(s, ext_ref[EXT0:EXT0 + tt, cols], pos, gi, wpool_ref, ps_ref)
        cat_ref[:, GLA_WIDTH + gi * POOL_GC:GLA_WIDTH + (gi + 1) * POOL_GC] = y.astype(BF16)
    ext_ref[SUBLANES:EXT0, :] = ext_ref[tt + SUBLANES:tt + EXT0, :]

    masks = _gla_masks()

    def chunk_body(ci, carry_):
        rows = pl.ds(pl.multiple_of(ci * GLA_CHUNK, GLA_CHUNK), GLA_CHUNK)

        def heads_on_rows(ref, off, width):
            return jnp.concatenate(
                [ref[rows, off + h * width:off + (h + 1) * width] for h in range(GLA_HEADS)], axis=0)

        outs = _gla_chunk(heads_on_rows(qk_ref, 0, GLA_DK) * Q_SCALE,
                          heads_on_rows(qk_ref, K_OFF, GLA_DK),
                          heads_on_rows(vgu_ref, V_OFF, GLA_DV),
                          heads_on_rows(gk_ref, 0, GLA_DK), st_ref, masks)
        for h in range(GLA_HEADS):
            vcols = slice(h * GLA_DV, (h + 1) * GLA_DV)
            gt = vgu_ref[rows, G_OFF + h * GLA_DV:G_OFF + (h + 1) * GLA_DV].astype(F32)
            cat_ref[rows, vcols] = _gated_head_out(outs[h], gt, gn_ref[...]).astype(BF16)
        return carry_

    lax.fori_loop(0, tt // GLA_CHUNK, chunk_body, 0, unroll=True)

    @pl.when(t == nt - 1)
    def _():
        for h in range(GLA_HEADS):
            sout_ref[0, h] = st_ref[h].T
        bufout_ref[0] = ext_ref[EXT0 - POOL_BUF:EXT0, :]


def _mixer_prompt(qk, vgu, gk, s0, buf0, layer, gla_norm, w_pool, pool_scale, ffn_weights, col_tiles,
                  stacked, *, n_seq, seq_len, pos0):
    tt = TT_PROMPT
    nt = seq_len // tt
    n_steps = n_seq * nt
    row = lambda b, t: b * nt + t
    n_alias = len(stacked)
    n_in = 8 + len(ffn_weights)
    w_blocks = [(w.shape[1] // n_steps, w.shape[2]) for w in ffn_weights]
    w_out_specs, w_out_shapes = [], []
    for w, (rows, cols), ct in zip(ffn_weights, w_blocks, col_tiles):
        if ct is None:
            w_out_specs.append(pl.BlockSpec((rows, cols), lambda b, t: (row(b, t), 0)))
            w_out_shapes.append(jax.ShapeDtypeStruct(w.shape[1:], BF16))
        else:
            w_out_specs.append(pl.BlockSpec((cols // ct, rows, ct), lambda b, t: (0, row(b, t), 0)))
            w_out_shapes.append(jax.ShapeDtypeStruct((cols // ct, w.shape[1], ct), BF16))
    return pl.pallas_call(
        functools.partial(_mixer_prompt_kernel, pos0=pos0, n_alias=n_alias, n_w=len(ffn_weights)),
        grid=(n_seq, nt),
        in_specs=[
            pl.BlockSpec((tt, 2 * QK_WIDTH), lambda b, t: (row(b, t), 0)),
            pl.BlockSpec((tt, VGU_WIDTH), lambda b, t: (row(b, t), 0)),
            pl.BlockSpec((tt, QK_WIDTH), lambda b, t: (row(b, t), 0)),
            pl.BlockSpec((1, GLA_HEADS, GLA_DK, GLA_DV), lambda b, t: (b, 0, 0, 0)),
            pl.BlockSpec((1, BUF_ROWS, POOL_WIDTH), lambda b, t: (b, 0, 0)),
            _layer_spec((1, GLA_DV), lambda b, t: (layer, 0, 0)),
            _layer_spec((len(POOL_WINDOWS), POOL_GC, POOL_GC), lambda b, t: (layer, 0, 0, 0)),
            _layer_spec((1, POOL_WIDTH), lambda b, t: (layer, 0, 0)),
        ] + [_layer_spec(blk, lambda b, t: (layer, row(b, t), 0)) for blk in w_blocks]
        + [pl.BlockSpec(memory_space=pl.ANY)] * n_alias,
        out_specs=[
            pl.BlockSpec((tt, D_MODEL), lambda b, t: (row(b, t), 0)),
            _layer_spec((1, GLA_HEADS, GLA_DK, GLA_DV), lambda b, t: (layer, b, 0, 0, 0)),
            _layer_spec((1, POOL_BUF, POOL_WIDTH), lambda b, t: (layer, b, 0, 0)),
        ] + w_out_specs,
        out_shape=[
            jax.ShapeDtypeStruct((qk.shape[0], D_MODEL), BF16),
            jax.ShapeDtypeStruct((DEPTH, n_seq, GLA_HEADS, GLA_DK, GLA_DV), F32),
            jax.ShapeDtypeStruct((DEPTH, n_seq, POOL_BUF, POOL_WIDTH), F32),
        ] + w_out_shapes,
        scratch_shapes=[
            pltpu.VMEM((GLA_HEADS, GLA_DV, GLA_DK), F32),
            pltpu.VMEM((EXT0 + tt, POOL_WIDTH), F32),
            pltpu.VMEM((2, EXT0 + tt, POOL_GC), F32),
        ],
        input_output_aliases={n_in + k: 1 + k for k in range(n_alias)},
        compiler_params=pltpu.CompilerParams(
            dimension_semantics=("arbitrary", "arbitrary"), vmem_limit_bytes=VMEM_LIMIT_BYTES),
        name="mixer_prompt",
    )(qk, vgu, gk, s0, buf0, gla_norm, w_pool, pool_scale, *ffn_weights, *stacked)


def _mixer_sample_kernel(qk_ref, vgu_ref, gk_ref, s0_ref, buf0_ref, gn_ref, wpool_ref, ps_ref, *rest,
                         pos0, t_len, n_alias):
    cat_ref, sout_ref, bufout_ref, ext_ref = rest[n_alias:]
    ns = SEQ_PER_STEP
    rows = ns * t_len

    ext_ref[:, 1:BUF_ROWS, :] = buf0_ref[...]
    ext_ref[:, BUF_ROWS:BUF_ROWS + t_len, :] = (
        vgu_ref[:, U_OFF:U_OFF + POOL_WIDTH].astype(F32).reshape(ns, t_len, POOL_WIDTH))
    pos = pos0 + lax.broadcasted_iota(jnp.int32, (ns, t_len, POOL_GC), 1)
    for gi in range(len(POOL_WINDOWS)):
        cols = slice(gi * POOL_GC, (gi + 1) * POOL_GC)
        u_cols = ext_ref[:, BUF_ROWS:BUF_ROWS + t_len, cols]
        w = POOL_WINDOWS[gi]
        s = u_cols
        for sft in range(1, w):
            s = s + ext_ref[:, BUF_ROWS - sft:BUF_ROWS - sft + t_len, cols]
        cnt = jnp.minimum(w, pos + 1).astype(F32)
        d = (s / cnt - u_cols).reshape(rows, POOL_GC)
        y = _dot(d.astype(BF16), wpool_ref[gi]) * ps_ref[:, cols]
        cat_ref[:, GLA_WIDTH + gi * POOL_GC:GLA_WIDTH + (gi + 1) * POOL_GC] = y.astype(BF16)
    bufout_ref[...] = ext_ref[:, t_len + 1:t_len + BUF_ROWS, :]

    r_i = lax.broadcasted_iota(jnp.int32, (rows, rows), 0)
    c_i = lax.broadcasted_iota(jnp.int32, (rows, rows), 1)
    same_seq = (r_i - jnp.bitwise_and(r_i, t_len - 1)) == (c_i - jnp.bitwise_and(c_i, t_len - 1))
    tri = jnp.logical_and(r_i >= c_i, same_seq).astype(BF16)
    row_in_seq = lax.broadcasted_iota(jnp.int32, (ns, t_len, 1), 1)
    seq_of_row = lax.broadcasted_iota(jnp.int32, (rows, 1), 0) // t_len
    zero_pad = jnp.zeros((LANES - rows, GLA_DK), F32)

    for h in range(GLA_HEADS):
        kcols = slice(h * GLA_DK, (h + 1) * GLA_DK)
        vcols = slice(h * GLA_DV, (h + 1) * GLA_DV)
        qh = qk_ref[:, h * GLA_DK:(h + 1) * GLA_DK] * Q_SCALE
        kh = qk_ref[:, K_OFF + h * GLA_DK:K_OFF + (h + 1) * GLA_DK]
        vb = vgu_ref[:, V_OFF + h * GLA_DV:V_OFF + (h + 1) * GLA_DV]
        vh = vb.astype(F32)
        gt = vgu_ref[:, G_OFF + h * GLA_DV:G_OFF + (h + 1) * GLA_DV].astype(F32)
        b = _cumsum_rows(tri, gk_ref[:, kcols])
        q3 = qh.reshape(ns, t_len, GLA_DK)
        k3 = kh.reshape(ns, t_len, GLA_DK)
        b3 = b.reshape(ns, t_len, GLA_DK)
        v3 = vh.reshape(ns, t_len, GLA_DV)
        o3 = jnp.zeros((ns, t_len, GLA_DV), F32)
        for jj in range(t_len):
            p = q3 * (k3[:, jj:jj + 1, :] * jnp.exp(jnp.minimum(b3 - b3[:, jj:jj + 1, :], 0.0)))
            col = jnp.sum(p, axis=-1, keepdims=True)
            col = jnp.where(row_in_seq >= jj, col, 0.0)
            o3 = o3 + col * v3[:, jj:jj + 1, :]
        o = o3.reshape(rows, GLA_DV)

        qe = qh * jnp.exp(b)
        k_dec = (k3 * jnp.exp(b3[:, t_len - 1:t_len, :] - b3)).reshape(rows, GLA_DK)
        b_t = jnp.concatenate([b, zero_pad], axis=0).T
        for s in range(ns):
            mine = seq_of_row == s
            s0 = s0_ref[s, h]
            o = o + _dot(jnp.where(mine, qe, 0.0).astype(BF16), s0.astype(BF16))
            last = s * t_len + t_len - 1
            a_col = jnp.exp(b_t[:, last:last + 1])
            upd = _dot_tn(jnp.where(mine, k_dec, 0.0).astype(BF16), vb)
            sout_ref[s, h] = a_col * s0 + upd
        cat_ref[:, vcols] = _gated_head_out(o, gt, gn_ref[...]).astype(BF16)


def _mixer_sample(qk, vgu, gk, s_in, buf_in, layer, gla_norm, w_pool, pool_scale, cat, stacked, *,
                  n_seq, t_len, pos0, row0):
    ns = SEQ_PER_STEP
    rows = ns * t_len
    rb0 = row0 // rows
    n_alias = 1 + len(stacked)
    n_in = 8
    return pl.pallas_call(
        functools.partial(_mixer_sample_kernel, pos0=pos0, t_len=t_len, n_alias=n_alias),
        grid=(n_seq // ns,),
        in_specs=[
            pl.BlockSpec((rows, 2 * QK_WIDTH), lambda i: (rb0 + i, 0)),
            pl.BlockSpec((rows, VGU_WIDTH), lambda i: (rb0 + i, 0)),
            pl.BlockSpec((rows, QK_WIDTH), lambda i: (rb0 + i, 0)),
            _layer_spec((ns, GLA_HEADS, GLA_DK, GLA_DV), lambda i: (layer, i, 0, 0, 0)),
            _layer_spec((ns, POOL_BUF, POOL_WIDTH), lambda i: (layer, i, 0, 0)),
            _layer_spec((1, GLA_DV), lambda i: (layer, 0, 0)),
            _layer_spec((len(POOL_WINDOWS), POOL_GC, POOL_GC), lambda i: (layer, 0, 0, 0)),
            _layer_spec((1, POOL_WIDTH), lambda i: (layer, 0, 0)),
        ] + [pl.BlockSpec(memory_space=pl.ANY)] * n_alias,
        out_specs=[
            pl.BlockSpec((rows, D_MODEL), lambda i: (rb0 + i, 0)),
            _layer_spec((ns, GLA_HEADS, GLA_DK, GLA_DV), lambda i: (layer, i, 0, 0, 0)),
            _layer_spec((ns, POOL_BUF, POOL_WIDTH), lambda i: (layer, i, 0, 0)),
        ],
        out_shape=[
            jax.ShapeDtypeStruct(cat.shape, BF16),
            jax.ShapeDtypeStruct((DEPTH, n_seq, GLA_HEADS, GLA_DK, GLA_DV), F32),
            jax.ShapeDtypeStruct((DEPTH, n_seq, POOL_BUF, POOL_WIDTH), F32),
        ],
        scratch_shapes=[pltpu.VMEM((ns, BUF_ROWS + t_len, POOL_WIDTH), F32)],
        input_output_aliases={n_in + k: k for k in range(n_alias)},
        compiler_params=pltpu.CompilerParams(
            dimension_semantics=("arbitrary",), vmem_limit_bytes=VMEM_LIMIT_BYTES),
        name="mixer_sample",
    )(qk, vgu, gk, s_in, buf_in, gla_norm, w_pool, pool_scale, cat, *stacked)


def _ffn_kernel(*refs, n_x, n_o, n_first_tiles, final_norm):
    x_refs = refs[:n_x]
    cat_ref, wout_ref, nrm_ref, wg_hbm, wu_hbm, wd_hbm, nf_ref = refs[n_x:n_x + 7]
    o_refs = refs[n_x + 7:n_x + 7 + n_o]
    h_ref, wg_buf, wu_buf, wd_buf, sem = refs[n_x + 7 + n_o:n_x + 12 + n_o]
    acc_ref = o_refs[0] if n_o == 1 else refs[n_x + 12 + n_o]
    i = pl.program_id(0)
    n_i = pl.num_programs(0)
    nj = wg_hbm.shape[0]

    def tile_copies(jt, slot):
        pairs = ((wg_hbm, wg_buf), (wu_hbm, wu_buf), (wd_hbm, wd_buf))
        return [pltpu.make_async_copy(src.at[jt], buf.at[slot], sem.at[k, slot])
                for k, (src, buf) in enumerate(pairs)]

    @pl.when(i == 0)
    def _():
        for s0 in range(FFN_RING - 1):
            for c in tile_copies(s0, s0):
                c.start()

    def start(x_ref):
        x2 = x_ref[...] + _dot(cat_ref[...], wout_ref[...])
        h_ref[...] = _rmsnorm(x2, nrm_ref[...]).astype(BF16)
        acc_ref[...] = x2

    _for_row_source(x_refs, n_first_tiles, start)

    def step(jt, carry):
        s = i * nj + jt
        slot = lax.rem(s, FFN_RING)
        ahead = FFN_RING - 1

        @pl.when(s + ahead < n_i * nj)
        def _():
            for c in tile_copies(lax.rem(jt + ahead, nj), lax.rem(s + ahead, FFN_RING)):
                c.start()

        for c in tile_copies(jt, slot):
            c.wait()

        h = h_ref[...]
        gate = _dot(h, wg_buf[slot])
        up = _dot(h, wu_buf[slot])
        act = (gate * jax.nn.sigmoid(gate)) * up
        acc_ref[...] += _dot(act.astype(BF16), wd_buf[slot])
        return carry

    lax.fori_loop(0, nj, step, 0)

    def result():
        return _rmsnorm(acc_ref[...], nf_ref[...]) if final_norm else acc_ref[...]

    if n_o == 1:
        if final_norm:
            acc_ref[...] = result()
    else:
        @pl.when(i < n_first_tiles)
        def _():
            o_refs[0][...] = result()

        @pl.when(i >= n_first_tiles)
        def _():
            acc_ref[...] = result()
            rows = pl.ds(pl.multiple_of((i - n_first_tiles) * TM_FFN, TM_FFN), TM_FFN)
            pltpu.sync_copy(acc_ref, o_refs[1].at[rows, :])


def _ffn(xs, cat, layer, w_out, nrm, w_gate_t, w_up_t, w_down_t, norm_final, *, final_norm, out_rows):
    m = sum(x.shape[0] for x in xs)
    n_first_tiles = (xs[0].shape[0] if len(xs) == 2 else out_rows[0]) // TM_FFN
    last = n_first_tiles - 1
    first_rows = lambda i: (jnp.minimum(i, last), 0)
    second_rows = lambda i: (jnp.maximum(i - n_first_tiles, 0), 0)
    tile = (TM_FFN, D_MODEL)
    if len(xs) == 1:
        x_specs = [pl.BlockSpec(tile, lambda i: (i, 0))]
    else:
        x_specs = [pl.BlockSpec(tile, first_rows),
                   pl.BlockSpec(tile, second_rows, pipeline_mode=pl.Buffered(1))]
    if len(out_rows) == 1:
        out_specs = [pl.BlockSpec(tile, lambda i: (i, 0))]
        acc_scratch = []
    else:
        out_specs = [pl.BlockSpec(tile, first_rows), pl.BlockSpec(memory_space=pl.ANY)]
        acc_scratch = [pltpu.VMEM(tile, F32)]
    return pl.pallas_call(
        functools.partial(_ffn_kernel, n_x=len(xs), n_o=len(out_rows),
                          n_first_tiles=n_first_tiles, final_norm=final_norm),
        grid=(m // TM_FFN,),
        in_specs=x_specs + [
            pl.BlockSpec(tile, lambda i: (i, 0)),
            pl.BlockSpec((D_MODEL, D_MODEL), lambda i: (0, 0), pipeline_mode=pl.Buffered(1)),
            _layer_spec((1, D_MODEL), lambda i: (layer, 0, 0)),
            pl.BlockSpec(memory_space=pl.ANY),
            pl.BlockSpec(memory_space=pl.ANY),
            pl.BlockSpec(memory_space=pl.ANY),
            pl.BlockSpec((1, D_MODEL), lambda i: (0, 0)),
        ],
        out_specs=out_specs,
        out_shape=[jax.ShapeDtypeStruct((r, D_MODEL), F32) for r in out_rows],
        scratch_shapes=[
            pltpu.VMEM(tile, BF16),
            pltpu.VMEM((FFN_RING, D_MODEL, TF_FFN), BF16),
            pltpu.VMEM((FFN_RING, D_MODEL, TF_FFN), BF16),
            pltpu.VMEM((FFN_RING, TF_FFN, D_MODEL), BF16),
            pltpu.SemaphoreType.DMA((3, FFN_RING)),
        ] + acc_scratch,
        compiler_params=pltpu.CompilerParams(
            dimension_semantics=("arbitrary",), vmem_limit_bytes=VMEM_LIMIT_BYTES),
        name="ffn",
    )(*xs, cat, w_out, nrm, w_gate_t, w_up_t, w_down_t, norm_final)


def kernel(x_prompt, x_sample, state_gla, state_pool, norm_mix, w_in, w_gk_up, b_gk, gla_norm,
           w_pool, pool_scale, w_out, norm_ffn, w_gate, w_up, w_down, norm_final):
    n_p, t_p, _ = x_prompt.shape
    n_s, t_s, _ = x_sample.shape
    m_p = n_p * t_p
    m_s = n_s * t_s

    w_in_t = jnp.swapaxes(w_in, 1, 2).astype(BF16)
    gate0 = W_IN_GATE0
    w_lr_t = jnp.pad(w_in_t[:, gate0:gate0 + GLA_LOWRANK],
                     ((0, 0), (0, LANES - GLA_LOWRANK), (0, 0)))
    w_upp = jnp.pad(w_gk_up, ((0, 0), (0, LANES - GLA_LOWRANK), (0, 0))).astype(BF16)
    w_pool_b = w_pool.astype(BF16)
    ffn_weights = (w_out, w_gate, w_up, w_down)
    row3 = lambda a: a[:, None, :]
    norm_mix3, norm_ffn3, b_gk3 = row3(norm_mix), row3(norm_ffn), row3(b_gk)
    gla_norm3, pool_scale3 = row3(gla_norm), row3(pool_scale)

    s0_p = jnp.zeros((n_p, GLA_HEADS, GLA_DK, GLA_DV), F32)
    buf0_p = jnp.zeros((n_p, BUF_ROWS, POOL_WIDTH), F32)

    xs = [x_prompt.reshape(m_p, D_MODEL), x_sample.reshape(m_s, D_MODEL)]
    stacked_p, stacked_s = (), ()
    for l in range(DEPTH):
        last = l == DEPTH - 1
        qk, vgu, gk = _inproj(xs, l, norm_mix3, w_in_t, w_lr_t, w_upp, b_gk3)
        cat, *rest = _mixer_prompt(qk, vgu, gk, s0_p, buf0_p, l, gla_norm3, w_pool_b, pool_scale3,
                                   ffn_weights, (None, TF_FFN, TF_FFN, None), stacked_p,
                                   n_seq=n_p, seq_len=t_p, pos0=0)
        stacked_p, (w_out_b, w_gate_b, w_up_b, w_down_b) = rest[:2], rest[2:]
        cat, *stacked_s = _mixer_sample(qk, vgu, gk, state_gla, state_pool, l, gla_norm3, w_pool_b,
                                        pool_scale3, cat, stacked_s, n_seq=n_s, t_len=t_s,
                                        pos0=PAST_LEN, row0=m_p)
        w_down_t = w_down_b.reshape(D_FF // TF_FFN, TF_FFN, D_MODEL)
        xs = _ffn(xs, cat, l, w_out_b, norm_ffn3, w_gate_b, w_up_b, w_down_t, norm_final[None],
                  final_norm=last, out_rows=(m_p, m_s) if last else (m_p + m_s,))

    y_prompt = xs[0].reshape(n_p, t_p, D_MODEL)
    y_sample = xs[1].reshape(n_s, t_s, D_MODEL)
    return (y_prompt, y_sample, stacked_p[0], stacked_p[1], stacked_s[0], stacked_s[1])
```

```python
import functools

import jax
import jax.numpy as jnp
from jax import lax
from jax.experimental import pallas as pl
from jax.experimental.pallas import tpu as pltpu

F32 = jnp.float32
BF16 = jnp.bfloat16

D_MODEL = 2048
DEPTH = 2
EPS = 1e-6
GLA_WIDTH = D_MODEL // 2
GLA_HEADS = 4
GLA_DV = GLA_WIDTH // GLA_HEADS
GLA_DK = GLA_DV // 2
GLA_LOWRANK = 16
GK_NORM = 16.0
GLA_CHUNK = 64
POOL_WIDTH = D_MODEL - GLA_WIDTH
POOL_WINDOWS = (2, 4, 8, 16)
POOL_GC = POOL_WIDTH // len(POOL_WINDOWS)
POOL_BUF = 15
D_FF = ((8 * D_MODEL // 3 + 255) // 256) * 256
QK_WIDTH = GLA_HEADS * GLA_DK
W_IN_GATE0 = 2 * QK_WIDTH + 2 * GLA_WIDTH
MAIN_WIDTH = W_IN_GATE0 + POOL_WIDTH
VGU_WIDTH = MAIN_WIDTH - 2 * QK_WIDTH
K_OFF = QK_WIDTH
V_OFF = 0
G_OFF = GLA_WIDTH
U_OFF = 2 * GLA_WIDTH
Q_SCALE = GLA_DK ** -0.5
PAST_LEN = 16384

LANES = 128
SUBLANES = 8
VMEM_LIMIT_BYTES = 58 * 1024 * 1024

TM_PROJ = 1024
TN_PROJ = 2 * QK_WIDTH
N_ALIGNED_TILES = W_IN_GATE0 // TN_PROJ
PROJ_K_CHUNK = 512
TM_FFN = 512
TF_FFN = 512
FFN_RING = 2
TT_PROMPT = 512
ROW_CHUNK = 128
SEQ_PER_STEP = 8
BUF_ROWS = POOL_BUF + 1
EXT0 = BUF_ROWS + SUBLANES


def _rmsnorm(x, g):
    r = lax.rsqrt(jnp.mean(x * x, axis=-1, keepdims=True) + EPS)
    return (x * r) * g


def _dot(a, b):
    return jnp.dot(a, b, preferred_element_type=F32)


def _dot_nt(a, b):
    return lax.dot_general(a, b, (((1,), (1,)), ((), ())), preferred_element_type=F32)


def _dot_tn(a, b):
    return lax.dot_general(a, b, (((0,), (0,)), ((), ())), preferred_element_type=F32)


def _split3(x):
    hi = x.astype(BF16)
    r1 = x - hi.astype(F32)
    mid = r1.astype(BF16)
    lo = (r1 - mid.astype(F32)).astype(BF16)
    return hi, mid, lo


def _cumsum_rows(tri, g):
    hi, mid, lo = _split3(g)
    b3 = _dot(tri, jnp.concatenate([hi, mid, lo], axis=1))
    n = g.shape[1]
    return b3[:, :n] + b3[:, n:2 * n] + b3[:, 2 * n:]


def _layer_spec(block, index_map, **kw):
    return pl.BlockSpec((None,) + tuple(block), index_map, **kw)


def _split_rows_specs(tm, n_first_tiles, two_inputs):
    if not two_inputs:
        return [pl.BlockSpec((tm, D_MODEL), lambda i, j: (i, 0))]
    last = n_first_tiles - 1
    return [
        pl.BlockSpec((tm, D_MODEL), lambda i, j: (jnp.minimum(i, last), 0)),
        pl.BlockSpec((tm, D_MODEL), lambda i, j: (jnp.maximum(i - n_first_tiles, 0), 0),
                     pipeline_mode=pl.Buffered(1)),
    ]


def _for_row_source(x_refs, n_first_tiles, fn):
    if len(x_refs) == 1:
        fn(x_refs[0])
        return
    i = pl.program_id(0)
    pl.when(i < n_first_tiles)(lambda: fn(x_refs[0]))
    pl.when(i >= n_first_tiles)(lambda: fn(x_refs[1]))


def _inproj_kernel(*refs, n_x, n_first_tiles):
    x_refs = refs[:n_x]
    nrm_ref, w_ref, wlr_ref, wup_ref, bgk_ref, qk_ref, vgu_ref, gk_ref, h_ref = refs[n_x:]
    j = pl.program_id(1)

    @pl.when(j == 0)
    def _():
        def normalize(x_ref):
            def body(r, carry):
                rows = pl.ds(pl.multiple_of(r * ROW_CHUNK, ROW_CHUNK), ROW_CHUNK)
                h_ref[rows, :] = _rmsnorm(x_ref[rows, :], nrm_ref[...]).astype(BF16)
                return carry

            lax.fori_loop(0, TM_PROJ // ROW_CHUNK, body, 0)

        _for_row_source(x_refs, n_first_tiles, normalize)

        lr = _dot_nt(h_ref[...], wlr_ref[...])
        z = _dot(lr.astype(BF16), wup_ref[...]) + bgk_ref[...]
        gk_ref[...] = jax.nn.log_sigmoid(z) / GK_NORM

    kc = PROJ_K_CHUNK
    t = _dot_nt(h_ref[:, 0:kc], w_ref[0, :, 0:kc])
    for k0 in range(kc, D_MODEL, kc):
        t = t + _dot_nt(h_ref[:, k0:k0 + kc], w_ref[0, :, k0:k0 + kc])

    @pl.when(j == 0)
    def _():
        qk_ref[...] = t

    @pl.when(j > 0)
    def _():
        vgu_ref[...] = t.astype(BF16)


def _inproj(xs, layer, nrm, w_in_t, w_lr_t, w_up, b_gk):
    m = sum(x.shape[0] for x in xs)
    n_first_tiles = xs[0].shape[0] // TM_PROJ
    grid = (m // TM_PROJ, MAIN_WIDTH // TN_PROJ)

    def w_rows(j):
        start = jnp.where(j < N_ALIGNED_TILES, j * TN_PROJ, j * TN_PROJ + GLA_LOWRANK)
        return pl.multiple_of(start, GLA_LOWRANK)

    return pl.pallas_call(
        functools.partial(_inproj_kernel, n_x=len(xs), n_first_tiles=n_first_tiles),
        grid=grid,
        in_specs=_split_rows_specs(TM_PROJ, n_first_tiles, len(xs) == 2) + [
            _layer_spec((1, D_MODEL), lambda i, j: (layer, 0, 0)),
            pl.BlockSpec((pl.Element(1), pl.Element(TN_PROJ), pl.Element(D_MODEL)),
                         lambda i, j: (layer, w_rows(j), 0)),
            _layer_spec((LANES, D_MODEL), lambda i, j: (layer, 0, 0)),
            _layer_spec((LANES, QK_WIDTH), lambda i, j: (layer, 0, 0)),
            _layer_spec((1, QK_WIDTH), lambda i, j: (layer, 0, 0)),
        ],
        out_specs=[
            pl.BlockSpec((TM_PROJ, TN_PROJ), lambda i, j: (i, 0)),
            pl.BlockSpec((TM_PROJ, TN_PROJ), lambda i, j: (i, jnp.maximum(j - 1, 0))),
            pl.BlockSpec((TM_PROJ, QK_WIDTH), lambda i, j: (i, 0)),
        ],
        out_shape=[
            jax.ShapeDtypeStruct((m, TN_PROJ), F32),
            jax.ShapeDtypeStruct((m, VGU_WIDTH), BF16),
            jax.ShapeDtypeStruct((m, QK_WIDTH), F32),
        ],
        scratch_shapes=[pltpu.VMEM((TM_PROJ, D_MODEL), BF16)],
        compiler_params=pltpu.CompilerParams(
            dimension_semantics=("arbitrary", "arbitrary"), vmem_limit_bytes=VMEM_LIMIT_BYTES),
        name="inproj",
    )(*xs, nrm, w_in_t, w_lr_t, w_up, b_gk)


def _pool_output(s, u_cols, pos, gi, wpool_ref, ps_ref):
    cnt = jnp.minimum(POOL_WINDOWS[gi], pos + 1).astype(F32)
    d = s / cnt - u_cols
    cols = slice(gi * POOL_GC, (gi + 1) * POOL_GC)
    return _dot(d.astype(BF16), wpool_ref[gi]) * ps_ref[:, cols]


def _window_sums_doubling(ext_ref, lvl_ref, cols, w, n_rows):
    end = EXT0 + n_rows
    n_lvl = w.bit_length() - 1

    def read(lo, hi):
        return ext_ref[lo:hi, cols]

    for lvl in range(n_lvl):
        sh = 1 << lvl
        lo = EXT0 if lvl == n_lvl - 1 else SUBLANES
        val = read(lo, end) + read(lo - sh, end - sh)
        if lvl == n_lvl - 1:
            return val
        buf = lvl_ref.at[lvl % 2]
        buf[lo:end, :] = val

        def read(lo_, hi_, buf=buf):
            return buf[lo_:hi_, :]


def _gla_chunk(q, k, v, g, st_ref, masks):
    tri, row_id, pair_level, on_diag = masks
    c = GLA_CHUNK
    b = _cumsum_rows(tri, g)

    a = jnp.where(on_diag, jnp.sum(q * k, axis=-1, keepdims=True), 0.0)
    for lvl in range(c.bit_length() - 1):
        f = jnp.exp(-jnp.abs(b - _segment_mid_rows(b, row_id, 1 << lvl)))
        scores = _dot_nt((f * q).astype(BF16), (f * k).astype(BF16))
        a = jnp.where(pair_level == lvl, scores, a)

    vb = v.astype(BF16)
    o_intra = _dot(a.astype(BF16), vb)
    qe = (q * jnp.exp(b)).astype(BF16)
    outs = []
    for h in range(GLA_HEADS):
        sl = slice(h * c, (h + 1) * c)
        st = st_ref[h]
        outs.append(o_intra[sl] + _dot_nt(qe[sl], st.astype(BF16)))
        b_last = b[(h + 1) * c - 1:(h + 1) * c, :]
        k_dec = k[sl] * jnp.exp(b_last - b[sl])
        st_ref[h] = st * jnp.exp(b_last) + _dot_tn(vb[sl], k_dec.astype(BF16))
    return outs


def _segment_mid_rows(b, row_id, s):
    c, dk = b.shape
    seg = 2 * s
    if seg >= SUBLANES:
        b3 = b.reshape(c // seg, seg, dk)
        return jnp.broadcast_to(b3[:, s:s + 1, :], b3.shape).reshape(c, dk)
    b3 = b.reshape(c // SUBLANES, SUBLANES, dk)
    place = jnp.bitwise_and(row_id, seg - 1)
    out = b
    for p in range(seg):
        if p != s:
            below = pltpu.roll(b3, (p - s) % SUBLANES, axis=1).reshape(c, dk)
            out = jnp.where(place == p, below, out)
    return out


def _gla_masks():
    n = GLA_HEADS * GLA_CHUNK
    row = lax.broadcasted_iota(jnp.int32, (n, n), 0)
    col = lax.broadcasted_iota(jnp.int32, (n, n), 1)
    top_differing_bit = 31 - lax.clz(jnp.bitwise_xor(row, col))
    same_head = top_differing_bit < GLA_CHUNK.bit_length() - 1
    tri = jnp.logical_and(row >= col, same_head).astype(BF16)
    row_id = lax.broadcasted_iota(jnp.int32, (n, GLA_DK), 0)
    pair_level = jnp.where(col < row, top_differing_bit, -1)
    return tri, row_id, pair_level, row == col


def _gated_head_out(o, gt, gn):
    return _rmsnorm(o, gn) * (gt * jax.nn.sigmoid(gt))


def _mixer_prompt_kernel(qk_ref, vgu_ref, gk_ref, s0_ref, buf0_ref, gn_ref, wpool_ref, ps_ref, *rest,
                         pos0, n_alias):
    wo32_ref, wg32_ref, wu32_ref, wd32_ref = rest[:4]
    rest = rest[4 + n_alias:]
    cat_ref, sout_ref, bufout_ref, wo16_ref, wgu16_ref, wd16_ref = rest[:6]
    st_ref, ext_ref, lvl_ref = rest[6:]
    t = pl.program_id(1)
    nt = pl.num_programs(1)
    tt = TT_PROMPT

    wo16_ref[...] = wo32_ref[...].astype(BF16)
    wd16_ref[...] = wd32_ref[...].astype(BF16)
    for ct in range(wgu16_ref.shape[0]):
        cols = slice(ct * TF_FFN, (ct + 1) * TF_FFN)
        wgu16_ref[ct, :, 0:TF_FFN] = wg32_ref[:, cols].astype(BF16)
        wgu16_ref[ct, :, TF_FFN:2 * TF_FFN] = wu32_ref[:, cols].astype(BF16)

    @pl.when(t == 0)
    def _():
        for h in range(GLA_HEADS):
            st_ref[h] = s0_ref[0, h].T
        ext_ref[0:SUBLANES, :] = jnp.zeros((SUBLANES, POOL_WIDTH), F32)
        lvl_ref[:, 0:SUBLANES, :] = jnp.zeros((2, SUBLANES, POOL_GC), F32)
        ext_ref[SUBLANES:EXT0, :] = buf0_ref[0]

    ext_ref[EXT0:EXT0 + tt, :] = vgu_ref[:, U_OFF:U_OFF + POOL_WIDTH].astype(F32)
    pos = pos0 + t * tt + lax.broadcasted_iota(jnp.int32, (tt, POOL_GC), 0)
    for gi, w in enumerate(POOL_WINDOWS):
        cols = slice(gi * POOL_GC, (gi + 1) * POOL_GC)
        s = _window_sums_doubling(ext_ref, lvl_ref, cols, w, tt)
        y = _pool_output(s, ext_ref[EXT0:EXT0 + tt, cols], pos, gi, wpool_ref, ps_ref)
        cat_ref[:, GLA_WIDTH + gi * POOL_GC:GLA_WIDTH + (gi + 1) * POOL_GC] = y.astype(BF16)
    ext_ref[SUBLANES:EXT0, :] = ext_ref[tt + SUBLANES:tt + EXT0, :]

    masks = _gla_masks()

    def chunk_body(ci, carry_):
        rows = pl.ds(pl.multiple_of(ci * GLA_CHUNK, GLA_CHUNK), GLA_CHUNK)

        def heads_on_rows(ref, off, width):
            return jnp.concatenate(
                [ref[rows, off + h * width:off + (h + 1) * width] for h in range(GLA_HEADS)], axis=0)

        outs = _gla_chunk(heads_on_rows(qk_ref, 0, GLA_DK) * Q_SCALE,
                          heads_on_rows(qk_ref, K_OFF, GLA_DK),
                          heads_on_rows(vgu_ref, V_OFF, GLA_DV),
                          heads_on_rows(gk_ref, 0, GLA_DK), st_ref, masks)
        for h in range(GLA_HEADS):
            vcols = slice(h * GLA_DV, (h + 1) * GLA_DV)
            gt = vgu_ref[rows, G_OFF + h * GLA_DV:G_OFF + (h + 1) * GLA_DV].astype(F32)
            cat_ref[rows, vcols] = _gated_head_out(outs[h], gt, gn_ref[...]).astype(BF16)
        return carry_

    lax.fori_loop(0, tt // GLA_CHUNK, chunk_body, 0, unroll=True)

    @pl.when(t == nt - 1)
    def _():
        for h in range(GLA_HEADS):
            sout_ref[0, h] = st_ref[h].T
        bufout_ref[0] = ext_ref[EXT0 - POOL_BUF:EXT0, :]


def _mixer_prompt(qk, vgu, gk, s0, buf0, layer, gla_norm, w_pool, pool_scale, ffn_weights, stacked,
                  *, n_seq, seq_len, pos0):
    tt = TT_PROMPT
    nt = seq_len // tt
    n_steps = n_seq * nt
    row = lambda b, t: b * nt + t
    n_alias = len(stacked)
    n_in = 8 + len(ffn_weights)
    w_blocks = [(w.shape[1] // n_steps, w.shape[2]) for w in ffn_weights]
    n_tiles = D_FF // TF_FFN
    w_out_specs = [
        pl.BlockSpec(w_blocks[0], lambda b, t: (row(b, t), 0)),
        pl.BlockSpec((n_tiles, w_blocks[1][0], 2 * TF_FFN), lambda b, t: (0, row(b, t), 0)),
        pl.BlockSpec(w_blocks[3], lambda b, t: (row(b, t), 0)),
    ]
    w_out_shapes = [
        jax.ShapeDtypeStruct((D_MODEL, D_MODEL), BF16),
        jax.ShapeDtypeStruct((n_tiles, D_MODEL, 2 * TF_FFN), BF16),
        jax.ShapeDtypeStruct((D_FF, D_MODEL), BF16),
    ]
    return pl.pallas_call(
        functools.partial(_mixer_prompt_kernel, pos0=pos0, n_alias=n_alias),
        grid=(n_seq, nt),
        in_specs=[
            pl.BlockSpec((tt, 2 * QK_WIDTH), lambda b, t: (row(b, t), 0)),
            pl.BlockSpec((tt, VGU_WIDTH), lambda b, t: (row(b, t), 0)),
            pl.BlockSpec((tt, QK_WIDTH), lambda b, t: (row(b, t), 0)),
            pl.BlockSpec((1, GLA_HEADS, GLA_DK, GLA_DV), lambda b, t: (b, 0, 0, 0)),
            pl.BlockSpec((1, BUF_ROWS, POOL_WIDTH), lambda b, t: (b, 0, 0)),
            _layer_spec((1, GLA_DV), lambda b, t: (layer, 0, 0)),
            _layer_spec((len(POOL_WINDOWS), POOL_GC, POOL_GC), lambda b, t: (layer, 0, 0, 0)),
            _layer_spec((1, POOL_WIDTH), lambda b, t: (layer, 0, 0)),
        ] + [_layer_spec(blk, lambda b, t: (layer, row(b, t), 0)) for blk in w_blocks]
        + [pl.BlockSpec(memory_space=pl.ANY)] * n_alias,
        out_specs=[
            pl.BlockSpec((tt, D_MODEL), lambda b, t: (row(b, t), 0)),
            _layer_spec((1, GLA_HEADS, GLA_DK, GLA_DV), lambda b, t: (layer, b, 0, 0, 0)),
            _layer_spec((1, POOL_BUF, POOL_WIDTH), lambda b, t: (layer, b, 0, 0)),
        ] + w_out_specs,
        out_shape=[
            jax.ShapeDtypeStruct((qk.shape[0], D_MODEL), BF16),
            jax.ShapeDtypeStruct((DEPTH, n_seq, GLA_HEADS, GLA_DK, GLA_DV), F32),
            jax.ShapeDtypeStruct((DEPTH, n_seq, POOL_BUF, POOL_WIDTH), F32),
        ] + w_out_shapes,
        scratch_shapes=[
            pltpu.VMEM((GLA_HEADS, GLA_DV, GLA_DK), F32),
            pltpu.VMEM((EXT0 + tt, POOL_WIDTH), F32),
            pltpu.VMEM((2, EXT0 + tt, POOL_GC), F32),
        ],
        input_output_aliases={n_in + k: 1 + k for k in range(n_alias)},
        compiler_params=pltpu.CompilerParams(
            dimension_semantics=("arbitrary", "arbitrary"), vmem_limit_bytes=VMEM_LIMIT_BYTES),
        name="mixer_prompt",
    )(qk, vgu, gk, s0, buf0, gla_norm, w_pool, pool_scale, *ffn_weights, *stacked)


def _mixer_sample_kernel(qk_ref, vgu_ref, gk_ref, s0_ref, buf0_ref, gn_ref, wpool_ref, ps_ref, *rest,
                         pos0, t_len, n_alias):
    cat_ref, sout_ref, bufout_ref, ext_ref = rest[n_alias:]
    ns = SEQ_PER_STEP
    rows = ns * t_len

    ext_ref[:, 1:BUF_ROWS, :] = buf0_ref[...]
    ext_ref[:, BUF_ROWS:BUF_ROWS + t_len, :] = (
        vgu_ref[:, U_OFF:U_OFF + POOL_WIDTH].astype(F32).reshape(ns, t_len, POOL_WIDTH))
    pos = pos0 + lax.broadcasted_iota(jnp.int32, (ns, t_len, POOL_GC), 1)
    for gi in range(len(POOL_WINDOWS)):
        cols = slice(gi * POOL_GC, (gi + 1) * POOL_GC)
        u_cols = ext_ref[:, BUF_ROWS:BUF_ROWS + t_len, cols]
        w = POOL_WINDOWS[gi]
        s = u_cols
        for sft in range(1, w):
            s = s + ext_ref[:, BUF_ROWS - sft:BUF_ROWS - sft + t_len, cols]
        cnt = jnp.minimum(w, pos + 1).astype(F32)
        d = (s / cnt - u_cols).reshape(rows, POOL_GC)
        y = _dot(d.astype(BF16), wpool_ref[gi]) * ps_ref[:, cols]
        cat_ref[:, GLA_WIDTH + gi * POOL_GC:GLA_WIDTH + (gi + 1) * POOL_GC] = y.astype(BF16)
    bufout_ref[...] = ext_ref[:, t_len + 1:t_len + BUF_ROWS, :]

    r_i = lax.broadcasted_iota(jnp.int32, (rows, rows), 0)
    c_i = lax.broadcasted_iota(jnp.int32, (rows, rows), 1)
    same_seq = (r_i - jnp.bitwise_and(r_i, t_len - 1)) == (c_i - jnp.bitwise_and(c_i, t_len - 1))
    tri = jnp.logical_and(r_i >= c_i, same_seq).astype(BF16)
    row_in_seq = lax.broadcasted_iota(jnp.int32, (ns, t_len, 1), 1)
    seq_of_row = lax.broadcasted_iota(jnp.int32, (rows, 1), 0) // t_len
    zero_pad = jnp.zeros((LANES - rows, GLA_DK), F32)

    for h in range(GLA_HEADS):
        kcols = slice(h * GLA_DK, (h + 1) * GLA_DK)
        vcols = slice(h * GLA_DV, (h + 1) * GLA_DV)
        qh = qk_ref[:, h * GLA_DK:(h + 1) * GLA_DK] * Q_SCALE
        kh = qk_ref[:, K_OFF + h * GLA_DK:K_OFF + (h + 1) * GLA_DK]
        vb = vgu_ref[:, V_OFF + h * GLA_DV:V_OFF + (h + 1) * GLA_DV]
        vh = vb.astype(F32)
        gt = vgu_ref[:, G_OFF + h * GLA_DV:G_OFF + (h + 1) * GLA_DV].astype(F32)
        b = _cumsum_rows(tri, gk_ref[:, kcols])
        q3 = qh.reshape(ns, t_len, GLA_DK)
        k3 = kh.reshape(ns, t_len, GLA_DK)
        b3 = b.reshape(ns, t_len, GLA_DK)
        v3 = vh.reshape(ns, t_len, GLA_DV)
        o3 = jnp.zeros((ns, t_len, GLA_DV), F32)
        for jj in range(t_len):
            p = q3 * (k3[:, jj:jj + 1, :] * jnp.exp(jnp.minimum(b3 - b3[:, jj:jj + 1, :], 0.0)))
            col = jnp.sum(p, axis=-1, keepdims=True)
            col = jnp.where(row_in_seq >= jj, col, 0.0)
            o3 = o3 + col * v3[:, jj:jj + 1, :]
        o = o3.reshape(rows, GLA_DV)

        qe = qh * jnp.exp(b)
        k_dec = (k3 * jnp.exp(b3[:, t_len - 1:t_len, :] - b3)).reshape(rows, GLA_DK)
        b_t = jnp.concatenate([b, zero_pad], axis=0).T
        for s in range(ns):
            mine = seq_of_row == s
            s0 = s0_ref[s, h]
            o = o + _dot(jnp.where(mine, qe, 0.0).astype(BF16), s0.astype(BF16))
            last = s * t_len + t_len - 1
            a_col = jnp.exp(b_t[:, last:last + 1])
            upd = _dot_tn(jnp.where(mine, k_dec, 0.0).astype(BF16), vb)
            sout_ref[s, h] = a_col * s0 + upd
        cat_ref[:, vcols] = _gated_head_out(o, gt, gn_ref[...]).astype(BF16)


def _mixer_sample(qk, vgu, gk, s_in, buf_in, layer, gla_norm, w_pool, pool_scale, cat, stacked, *,
                  n_seq, t_len, pos0, row0):
    ns = SEQ_PER_STEP
    rows = ns * t_len
    rb0 = row0 // rows
    n_alias = 1 + len(stacked)
    n_in = 8
    return pl.pallas_call(
        functools.partial(_mixer_sample_kernel, pos0=pos0, t_len=t_len, n_alias=n_alias),
        grid=(n_seq // ns,),
        in_specs=[
            pl.BlockSpec((rows, 2 * QK_WIDTH), lambda i: (rb0 + i, 0)),
            pl.BlockSpec((rows, VGU_WIDTH), lambda i: (rb0 + i, 0)),
            pl.BlockSpec((rows, QK_WIDTH), lambda i: (rb0 + i, 0)),
            _layer_spec((ns, GLA_HEADS, GLA_DK, GLA_DV), lambda i: (layer, i, 0, 0, 0)),
            _layer_spec((ns, POOL_BUF, POOL_WIDTH), lambda i: (layer, i, 0, 0)),
            _layer_spec((1, GLA_DV), lambda i: (layer, 0, 0)),
            _layer_spec((len(POOL_WINDOWS), POOL_GC, POOL_GC), lambda i: (layer, 0, 0, 0)),
            _layer_spec((1, POOL_WIDTH), lambda i: (layer, 0, 0)),
        ] + [pl.BlockSpec(memory_space=pl.ANY)] * n_alias,
        out_specs=[
            pl.BlockSpec((rows, D_MODEL), lambda i: (rb0 + i, 0)),
            _layer_spec((ns, GLA_HEADS, GLA_DK, GLA_DV), lambda i: (layer, i, 0, 0, 0)),
            _layer_spec((ns, POOL_BUF, POOL_WIDTH), lambda i: (layer, i, 0, 0)),
        ],
        out_shape=[
            jax.ShapeDtypeStruct(cat.shape, BF16),
            jax.ShapeDtypeStruct((DEPTH, n_seq, GLA_HEADS, GLA_DK, GLA_DV), F32),
            jax.ShapeDtypeStruct((DEPTH, n_seq, POOL_BUF, POOL_WIDTH), F32),
        ],
        scratch_shapes=[pltpu.VMEM((ns, BUF_ROWS + t_len, POOL_WIDTH), F32)],
        input_output_aliases={n_in + k: k for k in range(n_alias)},
        compiler_params=pltpu.CompilerParams(
            dimension_semantics=("arbitrary",), vmem_limit_bytes=VMEM_LIMIT_BYTES),
        name="mixer_sample",
    )(qk, vgu, gk, s_in, buf_in, gla_norm, w_pool, pool_scale, cat, *stacked)


def _ffn_kernel(*refs, n_x, n_o, n_first_tiles, final_norm):
    x_refs = refs[:n_x]
    cat_ref, wout_ref, nrm_ref, wgu_hbm, wd_hbm, nf_ref = refs[n_x:n_x + 6]
    o_refs = refs[n_x + 6:n_x + 6 + n_o]
    h_ref, wgu_buf, wd_buf, sem = refs[n_x + 6 + n_o:n_x + 10 + n_o]
    acc_ref = o_refs[0] if n_o == 1 else refs[n_x + 10 + n_o]
    i = pl.program_id(0)
    n_i = pl.num_programs(0)
    nj = wgu_hbm.shape[0]

    def tile_copies(jt, slot):
        pairs = ((wgu_hbm, wgu_buf), (wd_hbm, wd_buf))
        return [pltpu.make_async_copy(src.at[jt], buf.at[slot], sem.at[k, slot])
                for k, (src, buf) in enumerate(pairs)]

    @pl.when(i == 0)
    def _():
        for s0 in range(FFN_RING - 1):
            for c in tile_copies(s0, s0):
                c.start()

    def start(x_ref):
        x2 = x_ref[...] + _dot(cat_ref[...], wout_ref[...])
        h_ref[...] = _rmsnorm(x2, nrm_ref[...]).astype(BF16)
        acc_ref[...] = x2

    _for_row_source(x_refs, n_first_tiles, start)

    def step(jt, carry):
        s = i * nj + jt
        slot = lax.rem(s, FFN_RING)
        ahead = FFN_RING - 1

        @pl.when(s + ahead < n_i * nj)
        def _():
            for c in tile_copies(lax.rem(jt + ahead, nj), lax.rem(s + ahead, FFN_RING)):
                c.start()

        for c in tile_copies(jt, slot):
            c.wait()

        h = h_ref[...]
        gate_up = _dot(h, wgu_buf[slot])
        gate = gate_up[:, :TF_FFN]
        up = gate_up[:, TF_FFN:]
        act = (gate * jax.nn.sigmoid(gate)) * up
        acc_ref[...] += _dot(act.astype(BF16), wd_buf[slot])
        return carry

    lax.fori_loop(0, nj, step, 0)

    def result():
        return _rmsnorm(acc_ref[...], nf_ref[...]) if final_norm else acc_ref[...]

    if n_o == 1:
        if final_norm:
            acc_ref[...] = result()
    else:
        @pl.when(i < n_first_tiles)
        def _():
            o_refs[0][...] = result()

        @pl.when(i >= n_first_tiles)
        def _():
            acc_ref[...] = result()
            rows = pl.ds(pl.multiple_of((i - n_first_tiles) * TM_FFN, TM_FFN), TM_FFN)
            pltpu.sync_copy(acc_ref, o_refs[1].at[rows, :])


def _ffn(xs, cat, layer, w_out, nrm, w_gu_t, w_down_t, norm_final, *, final_norm, out_rows):
    m = sum(x.shape[0] for x in xs)
    n_first_tiles = (xs[0].shape[0] if len(xs) == 2 else out_rows[0]) // TM_FFN
    last = n_first_tiles - 1
    first_rows = lambda i: (jnp.minimum(i, last), 0)
    second_rows = lambda i: (jnp.maximum(i - n_first_tiles, 0), 0)
    tile = (TM_FFN, D_MODEL)
    if len(xs) == 1:
        x_specs = [pl.BlockSpec(tile, lambda i: (i, 0))]
    else:
        x_specs = [pl.BlockSpec(tile, first_rows),
                   pl.BlockSpec(tile, second_rows, pipeline_mode=pl.Buffered(1))]
    if len(out_rows) == 1:
        out_specs = [pl.BlockSpec(tile, lambda i: (i, 0))]
        acc_scratch = []
    else:
        out_specs = [pl.BlockSpec(tile, first_rows), pl.BlockSpec(memory_space=pl.ANY)]
        acc_scratch = [pltpu.VMEM(tile, F32)]
    return pl.pallas_call(
        functools.partial(_ffn_kernel, n_x=len(xs), n_o=len(out_rows),
                          n_first_tiles=n_first_tiles, final_norm=final_norm),
        grid=(m // TM_FFN,),
        in_specs=x_specs + [
            pl.BlockSpec(tile, lambda i: (i, 0)),
            pl.BlockSpec((D_MODEL, D_MODEL), lambda i: (0, 0), pipeline_mode=pl.Buffered(1)),
            _layer_spec((1, D_MODEL), lambda i: (layer, 0, 0)),
            pl.BlockSpec(memory_space=pl.ANY),
            pl.BlockSpec(memory_space=pl.ANY),
            pl.BlockSpec((1, D_MODEL), lambda i: (0, 0)),
        ],
        out_specs=out_specs,
        out_shape=[jax.ShapeDtypeStruct((r, D_MODEL), F32) for r in out_rows],
        scratch_shapes=[
            pltpu.VMEM(tile, BF16),
            pltpu.VMEM((FFN_RING, D_MODEL, 2 * TF_FFN), BF16),
            pltpu.VMEM((FFN_RING, TF_FFN, D_MODEL), BF16),
            pltpu.SemaphoreType.DMA((2, FFN_RING)),
        ] + acc_scratch,
        compiler_params=pltpu.CompilerParams(
            dimension_semantics=("arbitrary",), vmem_limit_bytes=VMEM_LIMIT_BYTES),
        name="ffn",
    )(*xs, cat, w_out, nrm, w_gu_t, w_down_t, norm_final)


def kernel(x_prompt, x_sample, state_gla, state_pool, norm_mix, w_in, w_gk_up, b_gk, gla_norm,
           w_pool, pool_scale, w_out, norm_ffn, w_gate, w_up, w_down, norm_final):
    n_p, t_p, _ = x_prompt.shape
    n_s, t_s, _ = x_sample.shape
    m_p = n_p * t_p
    m_s = n_s * t_s

    w_in_t = jnp.swapaxes(w_in, 1, 2).astype(BF16)
    gate0 = W_IN_GATE0
    w_lr_t = jnp.pad(w_in_t[:, gate0:gate0 + GLA_LOWRANK],
                     ((0, 0), (0, LANES - GLA_LOWRANK), (0, 0)))
    w_upp = jnp.pad(w_gk_up, ((0, 0), (0, LANES - GLA_LOWRANK), (0, 0))).astype(BF16)
    w_pool_b = w_pool.astype(BF16)
    ffn_weights = (w_out, w_gate, w_up, w_down)
    row3 = lambda a: a[:, None, :]
    norm_mix3, norm_ffn3, b_gk3 = row3(norm_mix), row3(norm_ffn), row3(b_gk)
    gla_norm3, pool_scale3 = row3(gla_norm), row3(pool_scale)

    s0_p = jnp.zeros((n_p, GLA_HEADS, GLA_DK, GLA_DV), F32)
    buf0_p = jnp.zeros((n_p, BUF_ROWS, POOL_WIDTH), F32)

    xs = [x_prompt.reshape(m_p, D_MODEL), x_sample.reshape(m_s, D_MODEL)]
    stacked_p, stacked_s = (), ()
    for l in range(DEPTH):
        last = l == DEPTH - 1
        qk, vgu, gk = _inproj(xs, l, norm_mix3, w_in_t, w_lr_t, w_upp, b_gk3)
        cat, *rest = _mixer_prompt(qk, vgu, gk, s0_p, buf0_p, l, gla_norm3, w_pool_b, pool_scale3,
                                   ffn_weights, stacked_p, n_seq=n_p, seq_len=t_p, pos0=0)
        stacked_p, (w_out_b, w_gu_t, w_down_b) = rest[:2], rest[2:]
        cat, *stacked_s = _mixer_sample(qk, vgu, gk, state_gla, state_pool, l, gla_norm3, w_pool_b,
                                        pool_scale3, cat, stacked_s, n_seq=n_s, t_len=t_s,
                                        pos0=PAST_LEN, row0=m_p)
        w_down_t = w_down_b.reshape(D_FF // TF_FFN, TF_FFN, D_MODEL)
        xs = _ffn(xs, cat, l, w_out_b, norm_ffn3, w_gu_t, w_down_t, norm_final[None],
                  final_norm=last, out_rows=(m_p, m_s) if last else (m_p + m_s,))

    y_prompt = xs[0].reshape(n_p, t_p, D_MODEL)
    y_sample = xs[1].reshape(n_s, t_s, D_MODEL)
    return (y_prompt, y_sample, stacked_p[0], stacked_p[1], stacked_s[0], stacked_s[1])
```

```python
import functools

import jax
import jax.numpy as jnp
from jax import lax
from jax.experimental import pallas as pl
from jax.experimental.pallas import tpu as pltpu

F32 = jnp.float32
BF16 = jnp.bfloat16

D_MODEL = 2048
DEPTH = 2
EPS = 1e-6
GLA_WIDTH = D_MODEL // 2
GLA_HEADS = 4
GLA_DV = GLA_WIDTH // GLA_HEADS
GLA_DK = GLA_DV // 2
GLA_LOWRANK = 16
GK_NORM = 16.0
GLA_CHUNK = 64
POOL_WIDTH = D_MODEL - GLA_WIDTH
POOL_WINDOWS = (2, 4, 8, 16)
POOL_GC = POOL_WIDTH // len(POOL_WINDOWS)
POOL_BUF = 15
D_FF = ((8 * D_MODEL // 3 + 255) // 256) * 256
QK_WIDTH = GLA_HEADS * GLA_DK
W_IN_GATE0 = 2 * QK_WIDTH + 2 * GLA_WIDTH
MAIN_WIDTH = W_IN_GATE0 + POOL_WIDTH
VGU_WIDTH = MAIN_WIDTH - 2 * QK_WIDTH
K_OFF = QK_WIDTH
V_OFF = 0
G_OFF = GLA_WIDTH
U_OFF = 2 * GLA_WIDTH
Q_SCALE = GLA_DK ** -0.5
PAST_LEN = 16384

LANES = 128
SUBLANES = 8
VMEM_LIMIT_BYTES = 58 * 1024 * 1024

TM_PROJ = 1024
TN_PROJ = 2 * QK_WIDTH
N_ALIGNED_TILES = W_IN_GATE0 // TN_PROJ
PROJ_K_CHUNK = 512
TM_FFN = 512
TF_FFN = 512
TT_PROMPT = 256
SUB = 16
ROW_CHUNK = 128
SEQ_PER_STEP = 8
BUF_ROWS = POOL_BUF + 1
EXT0 = BUF_ROWS + SUBLANES


def _rmsnorm(x, g):
    r = lax.rsqrt(jnp.mean(x * x, axis=-1, keepdims=True) + EPS)
    return (x * r) * g


def _dot(a, b):
    return jnp.dot(a, b, preferred_element_type=F32)


def _dot_nt(a, b):
    return lax.dot_general(a, b, (((1,), (1,)), ((), ())), preferred_element_type=F32)


def _dot_tn(a, b):
    return lax.dot_general(a, b, (((0,), (0,)), ((), ())), preferred_element_type=F32)


def _split3(x):
    hi = x.astype(BF16)
    r1 = x - hi.astype(F32)
    mid = r1.astype(BF16)
    lo = (r1 - mid.astype(F32)).astype(BF16)
    return hi, mid, lo


def _cumsum_rows(tri, g):
    hi, mid, lo = _split3(g)
    b3 = _dot(tri, jnp.concatenate([hi, mid, lo], axis=1))
    n = g.shape[1]
    return b3[:, :n] + b3[:, n:2 * n] + b3[:, 2 * n:]


def _layer_spec(block, index_map, **kw):
    return pl.BlockSpec((None,) + tuple(block), index_map, **kw)


def _split_rows_specs(tm, n_first_tiles, two_inputs):
    if not two_inputs:
        return [pl.BlockSpec((tm, D_MODEL), lambda i, j: (i, 0))]
    last = n_first_tiles - 1
    return [
        pl.BlockSpec((tm, D_MODEL), lambda i, j: (jnp.minimum(i, last), 0)),
        pl.BlockSpec((tm, D_MODEL), lambda i, j: (jnp.maximum(i - n_first_tiles, 0), 0),
                     pipeline_mode=pl.Buffered(1)),
    ]


def _for_row_source(x_refs, n_first_tiles, fn):
    if len(x_refs) == 1:
        fn(x_refs[0])
        return
    i = pl.program_id(0)
    pl.when(i < n_first_tiles)(lambda: fn(x_refs[0]))
    pl.when(i >= n_first_tiles)(lambda: fn(x_refs[1]))


def _inproj_kernel(*refs, n_x, n_first_tiles):
    x_refs = refs[:n_x]
    nrm_ref, w_ref, wlr_ref, wup_ref, bgk_ref, qk_ref, vgu_ref, gk_ref, h_ref = refs[n_x:]
    j = pl.program_id(1)

    @pl.when(j == 0)
    def _():
        def normalize(x_ref):
            def body(r, carry):
                rows = pl.ds(pl.multiple_of(r * ROW_CHUNK, ROW_CHUNK), ROW_CHUNK)
                h_ref[rows, :] = _rmsnorm(x_ref[rows, :], nrm_ref[...]).astype(BF16)
                return carry

            lax.fori_loop(0, TM_PROJ // ROW_CHUNK, body, 0)

        _for_row_source(x_refs, n_first_tiles, normalize)

        lr = _dot_nt(h_ref[...], wlr_ref[...])
        z = _dot(lr.astype(BF16), wup_ref[...]) + bgk_ref[...]
        gk_ref[...] = jax.nn.log_sigmoid(z) / GK_NORM

    kc = PROJ_K_CHUNK
    t = _dot_nt(h_ref[:, 0:kc], w_ref[0, :, 0:kc])
    for k0 in range(kc, D_MODEL, kc):
        t = t + _dot_nt(h_ref[:, k0:k0 + kc], w_ref[0, :, k0:k0 + kc])

    @pl.when(j == 0)
    def _():
        qk_ref[...] = t

    @pl.when(j > 0)
    def _():
        vgu_ref[...] = t.astype(BF16)


def _inproj(xs, layer, nrm, w_in_t, w_lr_t, w_up, b_gk):
    m = sum(x.shape[0] for x in xs)
    n_first_tiles = xs[0].shape[0] // TM_PROJ
    grid = (m // TM_PROJ, MAIN_WIDTH // TN_PROJ)

    def w_rows(j):
        start = jnp.where(j < N_ALIGNED_TILES, j * TN_PROJ, j * TN_PROJ + GLA_LOWRANK)
        return pl.multiple_of(start, GLA_LOWRANK)

    return pl.pallas_call(
        functools.partial(_inproj_kernel, n_x=len(xs), n_first_tiles=n_first_tiles),
        grid=grid,
        in_specs=_split_rows_specs(TM_PROJ, n_first_tiles, len(xs) == 2) + [
            _layer_spec((1, D_MODEL), lambda i, j: (layer, 0, 0)),
            pl.BlockSpec((pl.Element(1), pl.Element(TN_PROJ), pl.Element(D_MODEL)),
                         lambda i, j: (layer, w_rows(j), 0)),
            _layer_spec((LANES, D_MODEL), lambda i, j: (layer, 0, 0)),
            _layer_spec((LANES, QK_WIDTH), lambda i, j: (layer, 0, 0)),
            _layer_spec((1, QK_WIDTH), lambda i, j: (layer, 0, 0)),
        ],
        out_specs=[
            pl.BlockSpec((TM_PROJ, TN_PROJ), lambda i, j: (i, 0)),
            pl.BlockSpec((TM_PROJ, TN_PROJ), lambda i, j: (i, jnp.maximum(j - 1, 0))),
            pl.BlockSpec((TM_PROJ, QK_WIDTH), lambda i, j: (i, 0)),
        ],
        out_shape=[
            jax.ShapeDtypeStruct((m, TN_PROJ), F32),
            jax.ShapeDtypeStruct((m, VGU_WIDTH), BF16),
            jax.ShapeDtypeStruct((m, QK_WIDTH), F32),
        ],
        scratch_shapes=[pltpu.VMEM((TM_PROJ, D_MODEL), BF16)],
        compiler_params=pltpu.CompilerParams(
            dimension_semantics=("arbitrary", "arbitrary"), vmem_limit_bytes=VMEM_LIMIT_BYTES),
        name="inproj",
    )(*xs, nrm, w_in_t, w_lr_t, w_up, b_gk)


def _pool_output(s, u_cols, pos, gi, wpool_ref, ps_ref):
    cnt = jnp.minimum(POOL_WINDOWS[gi], pos + 1).astype(F32)
    d = s / cnt - u_cols
    cols = slice(gi * POOL_GC, (gi + 1) * POOL_GC)
    return _dot(d.astype(BF16), wpool_ref[gi]) * ps_ref[:, cols]


def _window_sums_doubling(ext_ref, lvl_ref, cols, w, n_rows):
    end = EXT0 + n_rows
    n_lvl = w.bit_length() - 1

    def read(lo, hi):
        return ext_ref[lo:hi, cols]

    for lvl in range(n_lvl):
        sh = 1 << lvl
        lo = EXT0 if lvl == n_lvl - 1 else SUBLANES
        val = read(lo, end) + read(lo - sh, end - sh)
        if lvl == n_lvl - 1:
            return val
        buf = lvl_ref.at[lvl % 2]
        buf[lo:end, :] = val

        def read(lo_, hi_, buf=buf):
            return buf[lo_:hi_, :]


def _gla_chunk(q, k, v, g, st_ref, masks):
    tri, row_id, pair_level, on_diag = masks
    c = GLA_CHUNK
    b = _cumsum_rows(tri, g)

    a = jnp.where(on_diag, jnp.sum(q * k, axis=-1, keepdims=True), 0.0)
    for lvl in range(c.bit_length() - 1):
        s = 1 << lvl
        f = jnp.exp(-jnp.abs(b - _segment_mid_rows(b, row_id, s)))
        upper = jnp.bitwise_and(row_id, s) != 0
        x = f * jnp.where(upper, q, k)
        qt = jnp.where(upper, x, 0.0).astype(BF16)
        kt = jnp.where(upper, 0.0, x).astype(BF16)
        a = jnp.where(pair_level == lvl, _dot_nt(qt, kt), a)

    vb = v.astype(BF16)
    o_intra = _dot(a.astype(BF16), vb)
    qe = (q * jnp.exp(b)).astype(BF16)
    outs = []
    for h in range(GLA_HEADS):
        sl = slice(h * c, (h + 1) * c)
        st = st_ref[h]
        outs.append(o_intra[sl] + _dot_nt(qe[sl], st.astype(BF16)))
        b_last = b[(h + 1) * c - 1:(h + 1) * c, :]
        k_dec = k[sl] * jnp.exp(b_last - b[sl])
        st_ref[h] = st * jnp.exp(b_last) + _dot_tn(vb[sl], k_dec.astype(BF16))
    return outs


def _segment_mid_rows(b, row_id, s):
    c, dk = b.shape
    seg = 2 * s
    if seg >= SUBLANES:
        b3 = b.reshape(c // seg, seg, dk)
        return jnp.broadcast_to(b3[:, s:s + 1, :], b3.shape).reshape(c, dk)
    b3 = b.reshape(c // SUBLANES, SUBLANES, dk)
    place = jnp.bitwise_and(row_id, seg - 1)
    out = b
    for p in range(seg):
        if p != s:
            below = pltpu.roll(b3, (p - s) % SUBLANES, axis=1).reshape(c, dk)
            out = jnp.where(place == p, below, out)
    return out


def _gla_masks():
    n = GLA_HEADS * GLA_CHUNK
    row = lax.broadcasted_iota(jnp.int32, (n, n), 0)
    col = lax.broadcasted_iota(jnp.int32, (n, n), 1)
    top_differing_bit = 31 - lax.clz(jnp.bitwise_xor(row, col))
    same_head = top_differing_bit < GLA_CHUNK.bit_length() - 1
    tri = jnp.logical_and(row >= col, same_head).astype(BF16)
    row_id = lax.broadcasted_iota(jnp.int32, (n, GLA_DK), 0)
    pair_level = jnp.where(col < row, top_differing_bit, -1)
    return tri, row_id, pair_level, row == col


def _gated_head_out(o, gt, gn):
    return _rmsnorm(o, gn) * (gt * jax.nn.sigmoid(gt))


def _mixer_prompt_kernel(qk_ref, vgu_ref, gk_ref, s0_ref, buf0_ref, gn_ref, wpool_ref, ps_ref, *rest,
                         pos0, n_alias, n_w):
    w32_refs = rest[:n_w]
    rest = rest[n_w + n_alias:]
    cat_ref, sout_ref, bufout_ref = rest[:3]
    w16_refs = rest[3:3 + n_w]
    st_ref, ext_ref, lvl_ref = rest[3 + n_w:]
    t = pl.program_id(1)
    nt = pl.num_programs(1)
    tt = TT_PROMPT

    for w32_ref, w16_ref in zip(w32_refs, w16_refs):
        if len(w16_ref.shape) == 2:
            w16_ref[...] = w32_ref[...].astype(BF16)
        else:
            width = w16_ref.shape[2]
            for ct in range(w16_ref.shape[0]):
                w16_ref[ct] = w32_ref[:, ct * width:(ct + 1) * width].astype(BF16)

    @pl.when(t == 0)
    def _():
        for h in range(GLA_HEADS):
            st_ref[h] = s0_ref[0, h].T
        ext_ref[0:SUBLANES, :] = jnp.zeros((SUBLANES, POOL_WIDTH), F32)
        lvl_ref[:, 0:SUBLANES, :] = jnp.zeros((2, SUBLANES, POOL_GC), F32)
        ext_ref[SUBLANES:EXT0, :] = buf0_ref[0]

    ext_ref[EXT0:EXT0 + tt, :] = vgu_ref[:, U_OFF:U_OFF + POOL_WIDTH].astype(F32)
    pos = pos0 + t * tt + lax.broadcasted_iota(jnp.int32, (tt, POOL_GC), 0)
    for gi, w in enumerate(POOL_WINDOWS):
        cols = slice(gi * POOL_GC, (gi + 1) * POOL_GC)
        s = _window_sums_doubling(ext_ref, lvl_ref, cols, w, tt)
        y = _pool_output(s, ext_ref[EXT0:EXT0 + tt, cols], pos, gi, wpool_ref, ps_ref)
        cat_ref[:, GLA_WIDTH + gi * POOL_GC:GLA_WIDTH + (gi + 1) * POOL_GC] = y.astype(BF16)
    ext_ref[SUBLANES:EXT0, :] = ext_ref[tt + SUBLANES:tt + EXT0, :]

    masks = _gla_masks()

    def chunk_body(ci, carry_):
        rows = pl.ds(pl.multiple_of(ci * GLA_CHUNK, GLA_CHUNK), GLA_CHUNK)

        def heads_on_rows(ref, off, width):
            return jnp.concatenate(
                [ref[rows, off + h * width:off + (h + 1) * width] for h in range(GLA_HEADS)], axis=0)

        outs = _gla_chunk(heads_on_rows(qk_ref, 0, GLA_DK) * Q_SCALE,
                          heads_on_rows(qk_ref, K_OFF, GLA_DK),
                          heads_on_rows(vgu_ref, V_OFF, GLA_DV),
                          heads_on_rows(gk_ref, 0, GLA_DK), st_ref, masks)
        for h in range(GLA_HEADS):
            vcols = slice(h * GLA_DV, (h + 1) * GLA_DV)
            gt = vgu_ref[rows, G_OFF + h * GLA_DV:G_OFF + (h + 1) * GLA_DV].astype(F32)
            cat_ref[rows, vcols] = _gated_head_out(outs[h], gt, gn_ref[...]).astype(BF16)
        return carry_

    lax.fori_loop(0, tt // GLA_CHUNK, chunk_body, 0, unroll=True)

    @pl.when(t == nt - 1)
    def _():
        for h in range(GLA_HEADS):
            sout_ref[0, h] = st_ref[h].T
        bufout_ref[0] = ext_ref[EXT0 - POOL_BUF:EXT0, :]


def _mixer_prompt(qk, vgu, gk, s0, buf0, layer, gla_norm, w_pool, pool_scale, ffn_weights, col_tiles,
                  stacked, *, n_seq, seq_len, pos0):
    tt = TT_PROMPT
    nt = seq_len // tt
    n_steps = n_seq * nt
    row = lambda b, t: b * nt + t
    n_alias = len(stacked)
    n_in = 8 + len(ffn_weights)
    w_blocks = [(w.shape[1] // n_steps, w.shape[2]) for w in ffn_weights]
    w_out_specs, w_out_shapes = [], []
    for w, (rows, cols), ct in zip(ffn_weights, w_blocks, col_tiles):
        if ct is None:
            w_out_specs.append(pl.BlockSpec((rows, cols), lambda b, t: (row(b, t), 0)))
            w_out_shapes.append(jax.ShapeDtypeStruct(w.shape[1:], BF16))
        else:
            w_out_specs.append(pl.BlockSpec((cols // ct, rows, ct), lambda b, t: (0, row(b, t), 0)))
            w_out_shapes.append(jax.ShapeDtypeStruct((cols // ct, w.shape[1], ct), BF16))
    return pl.pallas_call(
        functools.partial(_mixer_prompt_kernel, pos0=pos0, n_alias=n_alias, n_w=len(ffn_weights)),
        grid=(n_seq, nt),
        in_specs=[
            pl.BlockSpec((tt, 2 * QK_WIDTH), lambda b, t: (row(b, t), 0)),
            pl.BlockSpec((tt, VGU_WIDTH), lambda b, t: (row(b, t), 0)),
            pl.BlockSpec((tt, QK_WIDTH), lambda b, t: (row(b, t), 0)),
            pl.BlockSpec((1, GLA_HEADS, GLA_DK, GLA_DV), lambda b, t: (b, 0, 0, 0)),
            pl.BlockSpec((1, BUF_ROWS, POOL_WIDTH), lambda b, t: (b, 0, 0)),
            _layer_spec((1, GLA_DV), lambda b, t: (layer, 0, 0)),
            _layer_spec((len(POOL_WINDOWS), POOL_GC, POOL_GC), lambda b, t: (layer, 0, 0, 0)),
            _layer_spec((1, POOL_WIDTH), lambda b, t: (layer, 0, 0)),
        ] + [_layer_spec(blk, lambda b, t: (layer, row(b, t), 0)) for blk in w_blocks]
        + [pl.BlockSpec(memory_space=pl.ANY)] * n_alias,
        out_specs=[
            pl.BlockSpec((tt, D_MODEL), lambda b, t: (row(b, t), 0)),
            _layer_spec((1, GLA_HEADS, GLA_DK, GLA_DV), lambda b, t: (layer, b, 0, 0, 0)),
            _layer_spec((1, POOL_BUF, POOL_WIDTH), lambda b, t: (layer, b, 0, 0)),
        ] + w_out_specs,
        out_shape=[
            jax.ShapeDtypeStruct((qk.shape[0], D_MODEL), BF16),
            jax.ShapeDtypeStruct((DEPTH, n_seq, GLA_HEADS, GLA_DK, GLA_DV), F32),
            jax.ShapeDtypeStruct((DEPTH, n_seq, POOL_BUF, POOL_WIDTH), F32),
        ] + w_out_shapes,
        scratch_shapes=[
            pltpu.VMEM((GLA_HEADS, GLA_DV, GLA_DK), F32),
            pltpu.VMEM((EXT0 + tt, POOL_WIDTH), F32),
            pltpu.VMEM((2, EXT0 + tt, POOL_GC), F32),
        ],
        input_output_aliases={n_in + k: 1 + k for k in range(n_alias)},
        compiler_params=pltpu.CompilerParams(
            dimension_semantics=("arbitrary", "arbitrary"), vmem_limit_bytes=VMEM_LIMIT_BYTES),
        name="mixer_prompt",
    )(qk, vgu, gk, s0, buf0, gla_norm, w_pool, pool_scale, *ffn_weights, *stacked)


def _mixer_sample_kernel(qk_ref, vgu_ref, gk_ref, s0_ref, buf0_ref, gn_ref, wpool_ref, ps_ref, *rest,
                         pos0, t_len, n_alias):
    cat_ref, sout_ref, bufout_ref, ext_ref = rest[n_alias:]
    ns = SEQ_PER_STEP
    rows = ns * t_len

    for r in range(POOL_BUF):
        ext_ref[:, 1 + r, :] = buf0_ref[r]
    ext_ref[:, BUF_ROWS:BUF_ROWS + t_len, :] = (
        vgu_ref[:, U_OFF:U_OFF + POOL_WIDTH].astype(F32).reshape(ns, t_len, POOL_WIDTH))
    pos = pos0 + lax.broadcasted_iota(jnp.int32, (ns, t_len, POOL_GC), 1)
    for gi in range(len(POOL_WINDOWS)):
        cols = slice(gi * POOL_GC, (gi + 1) * POOL_GC)
        u_cols = ext_ref[:, BUF_ROWS:BUF_ROWS + t_len, cols]
        w = POOL_WINDOWS[gi]
        s = u_cols
        for sft in range(1, w):
            s = s + ext_ref[:, BUF_ROWS - sft:BUF_ROWS - sft + t_len, cols]
        cnt = jnp.minimum(w, pos + 1).astype(F32)
        d = (s / cnt - u_cols).reshape(rows, POOL_GC)
        y = _dot(d.astype(BF16), wpool_ref[gi]) * ps_ref[:, cols]
        cat_ref[:, GLA_WIDTH + gi * POOL_GC:GLA_WIDTH + (gi + 1) * POOL_GC] = y.astype(BF16)
    bufout_ref[...] = ext_ref[:, t_len + 1:t_len + BUF_ROWS, :]

    r_i = lax.broadcasted_iota(jnp.int32, (rows, rows), 0)
    c_i = lax.broadcasted_iota(jnp.int32, (rows, rows), 1)
    same_seq = (r_i - jnp.bitwise_and(r_i, t_len - 1)) == (c_i - jnp.bitwise_and(c_i, t_len - 1))
    tri = jnp.logical_and(r_i >= c_i, same_seq).astype(BF16)
    row_in_seq = lax.broadcasted_iota(jnp.int32, (ns, t_len, 1), 1)
    seq_of_row = lax.broadcasted_iota(jnp.int32, (rows, 1), 0) // t_len
    zero_pad = jnp.zeros((LANES - rows, GLA_DK), F32)

    for h in range(GLA_HEADS):
        kcols = slice(h * GLA_DK, (h + 1) * GLA_DK)
        vcols = slice(h * GLA_DV, (h + 1) * GLA_DV)
        qh = qk_ref[:, h * GLA_DK:(h + 1) * GLA_DK] * Q_SCALE
        kh = qk_ref[:, K_OFF + h * GLA_DK:K_OFF + (h + 1) * GLA_DK]
        vb = vgu_ref[:, V_OFF + h * GLA_DV:V_OFF + (h + 1) * GLA_DV]
        vh = vb.astype(F32)
        gt = vgu_ref[:, G_OFF + h * GLA_DV:G_OFF + (h + 1) * GLA_DV].astype(F32)
        b = _cumsum_rows(tri, gk_ref[:, kcols])
        q3 = qh.reshape(ns, t_len, GLA_DK)
        k3 = kh.reshape(ns, t_len, GLA_DK)
        b3 = b.reshape(ns, t_len, GLA_DK)
        v3 = vh.reshape(ns, t_len, GLA_DV)
        o3 = jnp.zeros((ns, t_len, GLA_DV), F32)
        for jj in range(t_len):
            p = q3 * (k3[:, jj:jj + 1, :] * jnp.exp(jnp.minimum(b3 - b3[:, jj:jj + 1, :], 0.0)))
            col = jnp.sum(p, axis=-1, keepdims=True)
            col = jnp.where(row_in_seq >= jj, col, 0.0)
            o3 = o3 + col * v3[:, jj:jj + 1, :]
        o = o3.reshape(rows, GLA_DV)

        qe = qh * jnp.exp(b)
        k_dec = (k3 * jnp.exp(b3[:, t_len - 1:t_len, :] - b3)).reshape(rows, GLA_DK)
        b_t = jnp.concatenate([b, zero_pad], axis=0).T
        for s in range(ns):
            mine = seq_of_row == s
            s0 = s0_ref[s, h]
            o = o + _dot(jnp.where(mine, qe, 0.0).astype(BF16), s0.astype(BF16))
            last = s * t_len + t_len - 1
            a_col = jnp.exp(b_t[:, last:last + 1])
            upd = _dot_tn(jnp.where(mine, k_dec, 0.0).astype(BF16), vb)
            sout_ref[s, h] = a_col * s0 + upd
        cat_ref[:, vcols] = _gated_head_out(o, gt, gn_ref[...]).astype(BF16)


def _mixer_sample(qk, vgu, gk, s_in, buf_in, layer, gla_norm, w_pool, pool_scale, cat, stacked, *,
                  n_seq, t_len, pos0, row0):
    ns = SEQ_PER_STEP
    rows = ns * t_len
    rb0 = row0 // rows
    n_alias = 1 + len(stacked)
    n_in = 8
    return pl.pallas_call(
        functools.partial(_mixer_sample_kernel, pos0=pos0, t_len=t_len, n_alias=n_alias),
        grid=(n_seq // ns,),
        in_specs=[
            pl.BlockSpec((rows, 2 * QK_WIDTH), lambda i: (rb0 + i, 0)),
            pl.BlockSpec((rows, VGU_WIDTH), lambda i: (rb0 + i, 0)),
            pl.BlockSpec((rows, QK_WIDTH), lambda i: (rb0 + i, 0)),
            _layer_spec((ns, GLA_HEADS, GLA_DK, GLA_DV), lambda i: (layer, i, 0, 0, 0)),
            _layer_spec((POOL_BUF, ns, POOL_WIDTH), lambda i: (layer, 0, i, 0)),
            _layer_spec((1, GLA_DV), lambda i: (layer, 0, 0)),
            _layer_spec((len(POOL_WINDOWS), POOL_GC, POOL_GC), lambda i: (layer, 0, 0, 0)),
            _layer_spec((1, POOL_WIDTH), lambda i: (layer, 0, 0)),
        ] + [pl.BlockSpec(memory_space=pl.ANY)] * n_alias,
        out_specs=[
            pl.BlockSpec((rows, D_MODEL), lambda i: (rb0 + i, 0)),
            _layer_spec((ns, GLA_HEADS, GLA_DK, GLA_DV), lambda i: (layer, i, 0, 0, 0)),
            _layer_spec((ns, POOL_BUF, POOL_WIDTH), lambda i: (layer, i, 0, 0)),
        ],
        out_shape=[
            jax.ShapeDtypeStruct(cat.shape, BF16),
            jax.ShapeDtypeStruct((DEPTH, n_seq, GLA_HEADS, GLA_DK, GLA_DV), F32),
            jax.ShapeDtypeStruct((DEPTH, n_seq, POOL_BUF, POOL_WIDTH), F32),
        ],
        scratch_shapes=[pltpu.VMEM((ns, BUF_ROWS + t_len, POOL_WIDTH), F32)],
        input_output_aliases={n_in + k: k for k in range(n_alias)},
        compiler_params=pltpu.CompilerParams(
            dimension_semantics=("arbitrary",), vmem_limit_bytes=VMEM_LIMIT_BYTES),
        name="mixer_sample",
    )(qk, vgu, gk, s_in, buf_in, gla_norm, w_pool, pool_scale, cat, *stacked)


def _ffn_kernel(*refs, n_x, n_o, n_first_tiles, final_norm):
    x_refs = refs[:n_x]
    cat_ref, wout_ref, nrm_ref, wg_hbm, wu_hbm, wd_hbm, nf_ref = refs[n_x:n_x + 7]
    o_refs = refs[n_x + 7:n_x + 7 + n_o]
    h_ref, wg_buf, wu_buf, wd_buf, sem = refs[n_x + 7 + n_o:n_x + 12 + n_o]
    acc_ref = o_refs[0] if n_o == 1 else refs[n_x + 12 + n_o]
    i = pl.program_id(0)
    n_i = pl.num_programs(0)
    nj = wg_hbm.shape[0]

    def tile_copies(jt, slot):
        pairs = ((wg_hbm, wg_buf), (wu_hbm, wu_buf), (wd_hbm, wd_buf))
        return [pltpu.make_async_copy(src.at[jt], buf.at[slot], sem.at[k, slot])
                for k, (src, buf) in enumerate(pairs)]

    @pl.when(i == 0)
    def _():
        for c in tile_copies(0, 0):
            c.start()

    def start(x_ref):
        x2 = x_ref[...] + _dot(cat_ref[...], wout_ref[...])
        h_ref[...] = _rmsnorm(x2, nrm_ref[...]).astype(BF16)
        acc_ref[...] = x2

    _for_row_source(x_refs, n_first_tiles, start)

    def step(jt, carry):
        s = i * nj + jt
        slot = jnp.bitwise_and(s, 1)

        @pl.when(s + 1 < n_i * nj)
        def _():
            for c in tile_copies(jnp.where(jt + 1 < nj, jt + 1, 0), 1 - slot):
                c.start()

        for c in tile_copies(jt, slot):
            c.wait()

        h = h_ref[...]
        gate = _dot(h, wg_buf[slot])
        up = _dot(h, wu_buf[slot])
        act = (gate * jax.nn.sigmoid(gate)) * up
        acc_ref[...] += _dot(act.astype(BF16), wd_buf[slot])
        return carry

    lax.fori_loop(0, nj, step, 0)

    def result():
        return _rmsnorm(acc_ref[...], nf_ref[...]) if final_norm else acc_ref[...]

    if n_o == 1:
        if final_norm:
            acc_ref[...] = result()
    else:
        @pl.when(i < n_first_tiles)
        def _():
            o_refs[0][...] = result()

        @pl.when(i >= n_first_tiles)
        def _():
            acc_ref[...] = result()
            rows = pl.ds(pl.multiple_of((i - n_first_tiles) * TM_FFN, TM_FFN), TM_FFN)
            pltpu.sync_copy(acc_ref, o_refs[1].at[rows, :])


def _ffn(xs, cat, layer, w_out, nrm, w_gate_t, w_up_t, w_down_t, norm_final, *, final_norm, out_rows):
    m = sum(x.shape[0] for x in xs)
    n_first_tiles = (xs[0].shape[0] if len(xs) == 2 else out_rows[0]) // TM_FFN
    last = n_first_tiles - 1
    first_rows = lambda i: (jnp.minimum(i, last), 0)
    second_rows = lambda i: (jnp.maximum(i - n_first_tiles, 0), 0)
    tile = (TM_FFN, D_MODEL)
    if len(xs) == 1:
        x_specs = [pl.BlockSpec(tile, lambda i: (i, 0))]
    else:
        x_specs = [pl.BlockSpec(tile, first_rows),
                   pl.BlockSpec(tile, second_rows, pipeline_mode=pl.Buffered(1))]
    if len(out_rows) == 1:
        out_specs = [pl.BlockSpec(tile, lambda i: (i, 0))]
        acc_scratch = []
    else:
        out_specs = [pl.BlockSpec(tile, first_rows), pl.BlockSpec(memory_space=pl.ANY)]
        acc_scratch = [pltpu.VMEM(tile, F32)]
    return pl.pallas_call(
        functools.partial(_ffn_kernel, n_x=len(xs), n_o=len(out_rows),
                          n_first_tiles=n_first_tiles, final_norm=final_norm),
        grid=(m // TM_FFN,),
        in_specs=x_specs + [
            pl.BlockSpec(tile, lambda i: (i, 0)),
            pl.BlockSpec((D_MODEL, D_MODEL), lambda i: (0, 0), pipeline_mode=pl.Buffered(1)),
            _layer_spec((1, D_MODEL), lambda i: (layer, 0, 0)),
            pl.BlockSpec(memory_space=pl.ANY),
            pl.BlockSpec(memory_space=pl.ANY),
            pl.BlockSpec(memory_space=pl.ANY),
            pl.BlockSpec((1, D_MODEL), lambda i: (0, 0)),
        ],
        out_specs=out_specs,
        out_shape=[jax.ShapeDtypeStruct((r, D_MODEL), F32) for r in out_rows],
        scratch_shapes=[
            pltpu.VMEM(tile, BF16),
            pltpu.VMEM((2, D_MODEL, TF_FFN), BF16),
            pltpu.VMEM((2, D_MODEL, TF_FFN), BF16),
            pltpu.VMEM((2, TF_FFN, D_MODEL), BF16),
            pltpu.SemaphoreType.DMA((3, 2)),
        ] + acc_scratch,
        compiler_params=pltpu.CompilerParams(
            dimension_semantics=("arbitrary",), vmem_limit_bytes=VMEM_LIMIT_BYTES),
        name="ffn",
    )(*xs, cat, w_out, nrm, w_gate_t, w_up_t, w_down_t, norm_final)


def kernel(x_prompt, x_sample, state_gla, state_pool, norm_mix, w_in, w_gk_up, b_gk, gla_norm,
           w_pool, pool_scale, w_out, norm_ffn, w_gate, w_up, w_down, norm_final):
    n_p, t_p, _ = x_prompt.shape
    n_s, t_s, _ = x_sample.shape
    m_p = n_p * t_p
    m_s = n_s * t_s

    w_in_t = jnp.swapaxes(w_in, 1, 2).astype(BF16)
    gate0 = W_IN_GATE0
    w_lr_t = jnp.pad(w_in_t[:, gate0:gate0 + GLA_LOWRANK],
                     ((0, 0), (0, LANES - GLA_LOWRANK), (0, 0)))
    w_upp = jnp.pad(w_gk_up, ((0, 0), (0, LANES - GLA_LOWRANK), (0, 0))).astype(BF16)
    w_pool_b = w_pool.astype(BF16)
    ffn_weights = (w_out, w_gate, w_up, w_down)
    row3 = lambda a: a[:, None, :]
    norm_mix3, norm_ffn3, b_gk3 = row3(norm_mix), row3(norm_ffn), row3(b_gk)
    gla_norm3, pool_scale3 = row3(gla_norm), row3(pool_scale)

    state_pool_t = jnp.swapaxes(state_pool, 1, 2)
    s0_p = jnp.zeros((n_p, GLA_HEADS, GLA_DK, GLA_DV), F32)
    buf0_p = jnp.zeros((n_p, BUF_ROWS, POOL_WIDTH), F32)

    xs = [x_prompt.reshape(m_p, D_MODEL), x_sample.reshape(m_s, D_MODEL)]
    stacked_p, stacked_s = (), ()
    for l in range(DEPTH):
        last = l == DEPTH - 1
        qk, vgu, gk = _inproj(xs, l, norm_mix3, w_in_t, w_lr_t, w_upp, b_gk3)
        cat, *rest = _mixer_prompt(qk, vgu, gk, s0_p, buf0_p, l, gla_norm3, w_pool_b, pool_scale3,
                                   ffn_weights, (None, TF_FFN, TF_FFN, None), stacked_p,
                                   n_seq=n_p, seq_len=t_p, pos0=0)
        stacked_p, (w_out_b, w_gate_b, w_up_b, w_down_b) = rest[:2], rest[2:]
        cat, *stacked_s = _mixer_sample(qk, vgu, gk, state_gla, state_pool_t, l, gla_norm3, w_pool_b,
                                        pool_scale3, cat, stacked_s, n_seq=n_s, t_len=t_s,
                                        pos0=PAST_LEN, row0=m_p)
        w_down_t = w_down_b.reshape(D_FF // TF_FFN, TF_FFN, D_MODEL)
        xs = _ffn(xs, cat, l, w_out_b, norm_ffn3, w_gate_b, w_up_b, w_down_t, norm_final[None],
                  final_norm=last, out_rows=(m_p, m_s) if last else (m_p + m_s,))

    y_prompt = xs[0].reshape(n_p, t_p, D_MODEL)
    y_sample = xs[1].reshape(n_s, t_s, D_MODEL)
    return (y_prompt, y_sample, stacked_p[0], stacked_p[1], stacked_s[0], stacked_s[1])
```

```python
import functools

import jax
import jax.numpy as jnp
from jax import lax
from jax.experimental import pallas as pl
from jax.experimental.pallas import tpu as pltpu

F32 = jnp.float32
BF16 = jnp.bfloat16

D_MODEL = 2048
DEPTH = 2
EPS = 1e-6
GLA_WIDTH = D_MODEL // 2
GLA_HEADS = 4
GLA_DV = GLA_WIDTH // GLA_HEADS
GLA_DK = GLA_DV // 2
GLA_LOWRANK = 16
GK_NORM = 16.0
GLA_CHUNK = 64
POOL_WIDTH = D_MODEL - GLA_WIDTH
POOL_WINDOWS = (2, 4, 8, 16)
POOL_GC = POOL_WIDTH // len(POOL_WINDOWS)
POOL_BUF = 15
D_FF = ((8 * D_MODEL // 3 + 255) // 256) * 256
QK_WIDTH = GLA_HEADS * GLA_DK
W_IN_GATE0 = 2 * QK_WIDTH + 2 * GLA_WIDTH
MAIN_WIDTH = W_IN_GATE0 + POOL_WIDTH
VGU_WIDTH = MAIN_WIDTH - 2 * QK_WIDTH
K_OFF = QK_WIDTH
V_OFF = 0
G_OFF = GLA_WIDTH
U_OFF = 2 * GLA_WIDTH
Q_SCALE = GLA_DK ** -0.5
PAST_LEN = 16384

LANES = 128
SUBLANES = 8
VMEM_LIMIT_BYTES = 58 * 1024 * 1024

TM_PROJ = 1024
TN_PROJ = 2 * QK_WIDTH
N_ALIGNED_TILES = W_IN_GATE0 // TN_PROJ
PROJ_K_CHUNK = 512
TM_FFN = 512
TF_FFN = 512
TT_PROMPT = 256
ROW_CHUNK = 128
SEQ_PER_STEP = 8
BUF_ROWS = POOL_BUF + 1
EXT0 = BUF_ROWS + SUBLANES


def _rmsnorm(x, g):
    r = lax.rsqrt(jnp.mean(x * x, axis=-1, keepdims=True) + EPS)
    return (x * r) * g


def _dot(a, b):
    return jnp.dot(a, b, preferred_element_type=F32)


def _dot_nt(a, b):
    return lax.dot_general(a, b, (((1,), (1,)), ((), ())), preferred_element_type=F32)


def _dot_tn(a, b):
    return lax.dot_general(a, b, (((0,), (0,)), ((), ())), preferred_element_type=F32)


def _split3(x):
    hi = x.astype(BF16)
    r1 = x - hi.astype(F32)
    mid = r1.astype(BF16)
    lo = (r1 - mid.astype(F32)).astype(BF16)
    return hi, mid, lo


def _cumsum_rows(tri, g):
    hi, mid, lo = _split3(g)
    b3 = _dot(tri, jnp.concatenate([hi, mid, lo], axis=1))
    n = g.shape[1]
    return b3[:, :n] + b3[:, n:2 * n] + b3[:, 2 * n:]


def _layer_spec(block, index_map, **kw):
    return pl.BlockSpec((None,) + tuple(block), index_map, **kw)


def _split_rows_specs(tm, n_first_tiles, two_inputs):
    if not two_inputs:
        return [pl.BlockSpec((tm, D_MODEL), lambda i, j: (i, 0))]
    last = n_first_tiles - 1
    return [
        pl.BlockSpec((tm, D_MODEL), lambda i, j: (jnp.minimum(i, last), 0)),
        pl.BlockSpec((tm, D_MODEL), lambda i, j: (jnp.maximum(i - n_first_tiles, 0), 0),
                     pipeline_mode=pl.Buffered(1)),
    ]


def _for_row_source(x_refs, n_first_tiles, fn):
    if len(x_refs) == 1:
        fn(x_refs[0])
        return
    i = pl.program_id(0)
    pl.when(i < n_first_tiles)(lambda: fn(x_refs[0]))
    pl.when(i >= n_first_tiles)(lambda: fn(x_refs[1]))


def _inproj_kernel(*refs, n_x, n_first_tiles):
    x_refs = refs[:n_x]
    nrm_ref, w_ref, wlr_ref, wup_ref, bgk_ref, qk_ref, vgu_ref, gk_ref, h_ref = refs[n_x:]
    j = pl.program_id(1)

    @pl.when(j == 0)
    def _():
        def normalize(x_ref):
            def body(r, carry):
                rows = pl.ds(pl.multiple_of(r * ROW_CHUNK, ROW_CHUNK), ROW_CHUNK)
                h_ref[rows, :] = _rmsnorm(x_ref[rows, :], nrm_ref[...]).astype(BF16)
                return carry

            lax.fori_loop(0, TM_PROJ // ROW_CHUNK, body, 0)

        _for_row_source(x_refs, n_first_tiles, normalize)

        lr = _dot_nt(h_ref[...], wlr_ref[...])
        z = _dot(lr.astype(BF16), wup_ref[...]) + bgk_ref[...]
        gk_ref[...] = jax.nn.log_sigmoid(z) / GK_NORM

    kc = PROJ_K_CHUNK
    t = _dot_nt(h_ref[:, 0:kc], w_ref[0, :, 0:kc])
    for k0 in range(kc, D_MODEL, kc):
        t = t + _dot_nt(h_ref[:, k0:k0 + kc], w_ref[0, :, k0:k0 + kc])

    @pl.when(j == 0)
    def _():
        qk_ref[...] = t

    @pl.when(j > 0)
    def _():
        vgu_ref[...] = t.astype(BF16)


def _inproj(xs, layer, nrm, w_in_t, w_lr_t, w_up, b_gk):
    m = sum(x.shape[0] for x in xs)
    n_first_tiles = xs[0].shape[0] // TM_PROJ
    grid = (m // TM_PROJ, MAIN_WIDTH // TN_PROJ)

    def w_rows(j):
        start = jnp.where(j < N_ALIGNED_TILES, j * TN_PROJ, j * TN_PROJ + GLA_LOWRANK)
        return pl.multiple_of(start, GLA_LOWRANK)

    return pl.pallas_call(
        functools.partial(_inproj_kernel, n_x=len(xs), n_first_tiles=n_first_tiles),
        grid=grid,
        in_specs=_split_rows_specs(TM_PROJ, n_first_tiles, len(xs) == 2) + [
            _layer_spec((1, D_MODEL), lambda i, j: (layer, 0, 0)),
            pl.BlockSpec((pl.Element(1), pl.Element(TN_PROJ), pl.Element(D_MODEL)),
                         lambda i, j: (layer, w_rows(j), 0)),
            _layer_spec((LANES, D_MODEL), lambda i, j: (layer, 0, 0)),
            _layer_spec((LANES, QK_WIDTH), lambda i, j: (layer, 0, 0)),
            _layer_spec((1, QK_WIDTH), lambda i, j: (layer, 0, 0)),
        ],
        out_specs=[
            pl.BlockSpec((TM_PROJ, TN_PROJ), lambda i, j: (i, 0)),
            pl.BlockSpec((TM_PROJ, TN_PROJ), lambda i, j: (i, jnp.maximum(j - 1, 0))),
            pl.BlockSpec((TM_PROJ, QK_WIDTH), lambda i, j: (i, 0)),
        ],
        out_shape=[
            jax.ShapeDtypeStruct((m, TN_PROJ), F32),
            jax.ShapeDtypeStruct((m, VGU_WIDTH), BF16),
            jax.ShapeDtypeStruct((m, QK_WIDTH), F32),
        ],
        scratch_shapes=[pltpu.VMEM((TM_PROJ, D_MODEL), BF16)],
        compiler_params=pltpu.CompilerParams(
            dimension_semantics=("arbitrary", "arbitrary"), vmem_limit_bytes=VMEM_LIMIT_BYTES),
        name="inproj",
    )(*xs, nrm, w_in_t, w_lr_t, w_up, b_gk)


def _pool_output(s, u_cols, pos, gi, wpool_ref, ps_ref):
    cnt = jnp.minimum(POOL_WINDOWS[gi], pos + 1).astype(F32)
    d = s / cnt - u_cols
    cols = slice(gi * POOL_GC, (gi + 1) * POOL_GC)
    return _dot(d.astype(BF16), wpool_ref[gi]) * ps_ref[:, cols]


def _window_sums_doubling(ext_ref, lvl_ref, cols, w, n_rows):
    end = EXT0 + n_rows
    n_lvl = w.bit_length() - 1

    def read(lo, hi):
        return ext_ref[lo:hi, cols]

    for lvl in range(n_lvl):
        sh = 1 << lvl
        lo = EXT0 if lvl == n_lvl - 1 else SUBLANES
        val = read(lo, end) + read(lo - sh, end - sh)
        if lvl == n_lvl - 1:
            return val
        buf = lvl_ref.at[lvl % 2]
        buf[lo:end, :] = val

        def read(lo_, hi_, buf=buf):
            return buf[lo_:hi_, :]


def _gla_chunk(q, k, v, g, st_ref, masks):
    tri, row_id, pair_level, on_diag = masks
    c = GLA_CHUNK
    b = _cumsum_rows(tri, g)

    a = jnp.where(on_diag, jnp.sum(q * k, axis=-1, keepdims=True), 0.0)
    for lvl in range(c.bit_length() - 1):
        f = jnp.exp(-jnp.abs(b - _segment_mid_rows(b, row_id, 1 << lvl)))
        scores = _dot_nt((f * q).astype(BF16), (f * k).astype(BF16))
        a = jnp.where(pair_level == lvl, scores, a)

    vb = v.astype(BF16)
    o_intra = _dot(a.astype(BF16), vb)
    qe = (q * jnp.exp(b)).astype(BF16)
    outs = []
    for h in range(GLA_HEADS):
        sl = slice(h * c, (h + 1) * c)
        st = st_ref[h]
        outs.append(o_intra[sl] + _dot_nt(qe[sl], st.astype(BF16)))
        b_last = b[(h + 1) * c - 1:(h + 1) * c, :]
        k_dec = k[sl] * jnp.exp(b_last - b[sl])
        st_ref[h] = st * jnp.exp(b_last) + _dot_tn(vb[sl], k_dec.astype(BF16))
    return outs


def _segment_mid_rows(b, row_id, s):
    c, dk = b.shape
    seg = 2 * s
    if seg >= SUBLANES:
        b3 = b.reshape(c // seg, seg, dk)
        return jnp.broadcast_to(b3[:, s:s + 1, :], b3.shape).reshape(c, dk)
    b3 = b.reshape(c // SUBLANES, SUBLANES, dk)
    place = jnp.bitwise_and(row_id, seg - 1)
    out = b
    for p in range(seg):
        if p != s:
            below = pltpu.roll(b3, (p - s) % SUBLANES, axis=1).reshape(c, dk)
            out = jnp.where(place == p, below, out)
    return out


def _gla_masks():
    n = GLA_HEADS * GLA_CHUNK
    row = lax.broadcasted_iota(jnp.int32, (n, n), 0)
    col = lax.broadcasted_iota(jnp.int32, (n, n), 1)
    top_differing_bit = 31 - lax.clz(jnp.bitwise_xor(row, col))
    same_head = top_differing_bit < GLA_CHUNK.bit_length() - 1
    tri = jnp.logical_and(row >= col, same_head).astype(BF16)
    row_id = lax.broadcasted_iota(jnp.int32, (n, GLA_DK), 0)
    pair_level = jnp.where(col < row, top_differing_bit, -1)
    return tri, row_id, pair_level, row == col


def _gated_head_out(o, gt, gn):
    return _rmsnorm(o, gn) * (gt * jax.nn.sigmoid(gt))


def _mixer_prompt_kernel(qk_ref, vgu_ref, gk_ref, s0_ref, buf0_ref, gn_ref, wpool_ref, ps_ref, *rest,
                         pos0, n_alias, n_w):
    w32_refs = rest[:n_w]
    rest = rest[n_w + n_alias:]
    cat_ref, sout_ref, bufout_ref = rest[:3]
    w16_refs = rest[3:3 + n_w]
    st_ref, ext_ref, lvl_ref = rest[3 + n_w:]
    t = pl.program_id(1)
    nt = pl.num_programs(1)
    tt = TT_PROMPT

    for w32_ref, w16_ref in zip(w32_refs, w16_refs):
        if len(w16_ref.shape) == 2:
            w16_ref[...] = w32_ref[...].astype(BF16)
        else:
            width = w16_ref.shape[2]
            for ct in range(w16_ref.shape[0]):
                w16_ref[ct] = w32_ref[:, ct * width:(ct + 1) * width].astype(BF16)

    @pl.when(t == 0)
    def _():
        for h in range(GLA_HEADS):
            st_ref[h] = s0_ref[0, h].T
        ext_ref[0:SUBLANES, :] = jnp.zeros((SUBLANES, POOL_WIDTH), F32)
        lvl_ref[:, 0:SUBLANES, :] = jnp.zeros((2, SUBLANES, POOL_GC), F32)
        ext_ref[SUBLANES:EXT0, :] = buf0_ref[0]

    ext_ref[EXT0:EXT0 + tt, :] = vgu_ref[:, U_OFF:U_OFF + POOL_WIDTH].astype(F32)
    pos = pos0 + t * tt + lax.broadcasted_iota(jnp.int32, (tt, POOL_GC), 0)
    for gi, w in enumerate(POOL_WINDOWS):
        cols = slice(gi * POOL_GC, (gi + 1) * POOL_GC)
        s = _window_sums_doubling(ext_ref, lvl_ref, cols, w, tt)
        y = _pool_output(s, ext_ref[EXT0:EXT0 + tt, cols], pos, gi, wpool_ref, ps_ref)
        cat_ref[:, GLA_WIDTH + gi * POOL_GC:GLA_WIDTH + (gi + 1) * POOL_GC] = y.astype(BF16)
    ext_ref[SUBLANES:EXT0, :] = ext_ref[tt + SUBLANES:tt + EXT0, :]

    masks = _gla_masks()

    def chunk_body(ci, carry_):
        rows = pl.ds(pl.multiple_of(ci * GLA_CHUNK, GLA_CHUNK), GLA_CHUNK)

        def heads_on_rows(ref, off, width):
            return jnp.concatenate(
                [ref[rows, off + h * width:off + (h + 1) * width] for h in range(GLA_HEADS)], axis=0)

        outs = _gla_chunk(heads_on_rows(qk_ref, 0, GLA_DK) * Q_SCALE,
                          heads_on_rows(qk_ref, K_OFF, GLA_DK),
                          heads_on_rows(vgu_ref, V_OFF, GLA_DV),
                          heads_on_rows(gk_ref, 0, GLA_DK), st_ref, masks)
        for h in range(GLA_HEADS):
            vcols = slice(h * GLA_DV, (h + 1) * GLA_DV)
            gt = vgu_ref[rows, G_OFF + h * GLA_DV:G_OFF + (h + 1) * GLA_DV].astype(F32)
            cat_ref[rows, vcols] = _gated_head_out(outs[h], gt, gn_ref[...]).astype(BF16)
        return carry_

    lax.fori_loop(0, tt // GLA_CHUNK, chunk_body, 0, unroll=True)

    @pl.when(t == nt - 1)
    def _():
        for h in range(GLA_HEADS):
            sout_ref[0, h] = st_ref[h].T
        bufout_ref[0] = ext_ref[EXT0 - POOL_BUF:EXT0, :]


def _mixer_prompt(qk, vgu, gk, s0, buf0, layer, gla_norm, w_pool, pool_scale, ffn_weights, col_tiles,
                  stacked, *, n_seq, seq_len, pos0):
    tt = TT_PROMPT
    nt = seq_len // tt
    n_steps = n_seq * nt
    row = lambda b, t: b * nt + t
    n_alias = len(stacked)
    n_in = 8 + len(ffn_weights)
    w_blocks = [(w.shape[1] // n_steps, w.shape[2]) for w in ffn_weights]
    w_out_specs, w_out_shapes = [], []
    for w, (rows, cols), ct in zip(ffn_weights, w_blocks, col_tiles):
        if ct is None:
            w_out_specs.append(pl.BlockSpec((rows, cols), lambda b, t: (row(b, t), 0)))
            w_out_shapes.append(jax.ShapeDtypeStruct(w.shape[1:], BF16))
        else:
            w_out_specs.append(pl.BlockSpec((cols // ct, rows, ct), lambda b, t: (0, row(b, t), 0)))
            w_out_shapes.append(jax.ShapeDtypeStruct((cols // ct, w.shape[1], ct), BF16))
    return pl.pallas_call(
        functools.partial(_mixer_prompt_kernel, pos0=pos0, n_alias=n_alias, n_w=len(ffn_weights)),
        grid=(n_seq, nt),
        in_specs=[
            pl.BlockSpec((tt, 2 * QK_WIDTH), lambda b, t: (row(b, t), 0)),
            pl.BlockSpec((tt, VGU_WIDTH), lambda b, t: (row(b, t), 0)),
            pl.BlockSpec((tt, QK_WIDTH), lambda b, t: (row(b, t), 0)),
            pl.BlockSpec((1, GLA_HEADS, GLA_DK, GLA_DV), lambda b, t: (b, 0, 0, 0)),
            pl.BlockSpec((1, BUF_ROWS, POOL_WIDTH), lambda b, t: (b, 0, 0)),
            _layer_spec((1, GLA_DV), lambda b, t: (layer, 0, 0)),
            _layer_spec((len(POOL_WINDOWS), POOL_GC, POOL_GC), lambda b, t: (layer, 0, 0, 0)),
            _layer_spec((1, POOL_WIDTH), lambda b, t: (layer, 0, 0)),
        ] + [_layer_spec(blk, lambda b, t: (layer, row(b, t), 0)) for blk in w_blocks]
        + [pl.BlockSpec(memory_space=pl.ANY)] * n_alias,
        out_specs=[
            pl.BlockSpec((tt, D_MODEL), lambda b, t: (row(b, t), 0)),
            _layer_spec((1, GLA_HEADS, GLA_DK, GLA_DV), lambda b, t: (layer, b, 0, 0, 0)),
            _layer_spec((1, POOL_BUF, POOL_WIDTH), lambda b, t: (layer, b, 0, 0)),
        ] + w_out_specs,
        out_shape=[
            jax.ShapeDtypeStruct((qk.shape[0], D_MODEL), BF16),
            jax.ShapeDtypeStruct((DEPTH, n_seq, GLA_HEADS, GLA_DK, GLA_DV), F32),
            jax.ShapeDtypeStruct((DEPTH, n_seq, POOL_BUF, POOL_WIDTH), F32),
        ] + w_out_shapes,
        scratch_shapes=[
            pltpu.VMEM((GLA_HEADS, GLA_DV, GLA_DK), F32),
            pltpu.VMEM((EXT0 + tt, POOL_WIDTH), F32),
            pltpu.VMEM((2, EXT0 + tt, POOL_GC), F32),
        ],
        input_output_aliases={n_in + k: 1 + k for k in range(n_alias)},
        compiler_params=pltpu.CompilerParams(
            dimension_semantics=("arbitrary", "arbitrary"), vmem_limit_bytes=VMEM_LIMIT_BYTES),
        name="mixer_prompt",
    )(qk, vgu, gk, s0, buf0, gla_norm, w_pool, pool_scale, *ffn_weights, *stacked)


def _mixer_sample_kernel(qk_ref, vgu_ref, gk_ref, s0_ref, buf0_ref, gn_ref, wpool_ref, ps_ref, *rest,
                         pos0, t_len, n_alias):
    cat_ref, sout_ref, bufout_ref, ext_ref = rest[n_alias:]
    ns = SEQ_PER_STEP
    rows = ns * t_len

    for r in range(POOL_BUF):
        ext_ref[:, 1 + r, :] = buf0_ref[r]
    ext_ref[:, BUF_ROWS:BUF_ROWS + t_len, :] = (
        vgu_ref[:, U_OFF:U_OFF + POOL_WIDTH].astype(F32).reshape(ns, t_len, POOL_WIDTH))
    pos = pos0 + lax.broadcasted_iota(jnp.int32, (ns, t_len, POOL_GC), 1)
    for gi in range(len(POOL_WINDOWS)):
        cols = slice(gi * POOL_GC, (gi + 1) * POOL_GC)
        u_cols = ext_ref[:, BUF_ROWS:BUF_ROWS + t_len, cols]
        w = POOL_WINDOWS[gi]
        s = u_cols
        for sft in range(1, w):
            s = s + ext_ref[:, BUF_ROWS - sft:BUF_ROWS - sft + t_len, cols]
        cnt = jnp.minimum(w, pos + 1).astype(F32)
        d = (s / cnt - u_cols).reshape(rows, POOL_GC)
        y = _dot(d.astype(BF16), wpool_ref[gi]) * ps_ref[:, cols]
        cat_ref[:, GLA_WIDTH + gi * POOL_GC:GLA_WIDTH + (gi + 1) * POOL_GC] = y.astype(BF16)
    bufout_ref[...] = ext_ref[:, t_len + 1:t_len + BUF_ROWS, :]

    r_i = lax.broadcasted_iota(jnp.int32, (rows, rows), 0)
    c_i = lax.broadcasted_iota(jnp.int32, (rows, rows), 1)
    same_seq = (r_i - jnp.bitwise_and(r_i, t_len - 1)) == (c_i - jnp.bitwise_and(c_i, t_len - 1))
    tri = jnp.logical_and(r_i >= c_i, same_seq).astype(BF16)
    row_in_seq = lax.broadcasted_iota(jnp.int32, (ns, t_len, 1), 1)
    seq_of_row = lax.broadcasted_iota(jnp.int32, (rows, 1), 0) // t_len
    zero_pad = jnp.zeros((LANES - rows, GLA_DK), F32)

    for h in range(GLA_HEADS):
        kcols = slice(h * GLA_DK, (h + 1) * GLA_DK)
        vcols = slice(h * GLA_DV, (h + 1) * GLA_DV)
        qh = qk_ref[:, h * GLA_DK:(h + 1) * GLA_DK] * Q_SCALE
        kh = qk_ref[:, K_OFF + h * GLA_DK:K_OFF + (h + 1) * GLA_DK]
        vb = vgu_ref[:, V_OFF + h * GLA_DV:V_OFF + (h + 1) * GLA_DV]
        vh = vb.astype(F32)
        gt = vgu_ref[:, G_OFF + h * GLA_DV:G_OFF + (h + 1) * GLA_DV].astype(F32)
        b = _cumsum_rows(tri, gk_ref[:, kcols])
        q3 = qh.reshape(ns, t_len, GLA_DK)
        k3 = kh.reshape(ns, t_len, GLA_DK)
        b3 = b.reshape(ns, t_len, GLA_DK)
        v3 = vh.reshape(ns, t_len, GLA_DV)
        o3 = jnp.zeros((ns, t_len, GLA_DV), F32)
        for jj in range(t_len):
            p = q3 * (k3[:, jj:jj + 1, :] * jnp.exp(jnp.minimum(b3 - b3[:, jj:jj + 1, :], 0.0)))
            col = jnp.sum(p, axis=-1, keepdims=True)
            col = jnp.where(row_in_seq >= jj, col, 0.0)
            o3 = o3 + col * v3[:, jj:jj + 1, :]
        o = o3.reshape(rows, GLA_DV)

        qe = qh * jnp.exp(b)
        k_dec = (k3 * jnp.exp(b3[:, t_len - 1:t_len, :] - b3)).reshape(rows, GLA_DK)
        b_t = jnp.concatenate([b, zero_pad], axis=0).T
        for s in range(ns):
            mine = seq_of_row == s
            s0 = s0_ref[s, h]
            o = o + _dot(jnp.where(mine, qe, 0.0).astype(BF16), s0.astype(BF16))
            last = s * t_len + t_len - 1
            a_col = jnp.exp(b_t[:, last:last + 1])
            upd = _dot_tn(jnp.where(mine, k_dec, 0.0).astype(BF16), vb)
            sout_ref[s, h] = a_col * s0 + upd
        cat_ref[:, vcols] = _gated_head_out(o, gt, gn_ref[...]).astype(BF16)


def _mixer_sample(qk, vgu, gk, s_in, buf_in, layer, gla_norm, w_pool, pool_scale, cat, stacked, *,
                  n_seq, t_len, pos0, row0):
    ns = SEQ_PER_STEP
    rows = ns * t_len
    rb0 = row0 // rows
    n_alias = 1 + len(stacked)
    n_in = 8
    return pl.pallas_call(
        functools.partial(_mixer_sample_kernel, pos0=pos0, t_len=t_len, n_alias=n_alias),
        grid=(n_seq // ns,),
        in_specs=[
            pl.BlockSpec((rows, 2 * QK_WIDTH), lambda i: (rb0 + i, 0)),
            pl.BlockSpec((rows, VGU_WIDTH), lambda i: (rb0 + i, 0)),
            pl.BlockSpec((rows, QK_WIDTH), lambda i: (rb0 + i, 0)),
            _layer_spec((ns, GLA_HEADS, GLA_DK, GLA_DV), lambda i: (layer, i, 0, 0, 0)),
            _layer_spec((POOL_BUF, ns, POOL_WIDTH), lambda i: (layer, 0, i, 0)),
            _layer_spec((1, GLA_DV), lambda i: (layer, 0, 0)),
            _layer_spec((len(POOL_WINDOWS), POOL_GC, POOL_GC), lambda i: (layer, 0, 0, 0)),
            _layer_spec((1, POOL_WIDTH), lambda i: (layer, 0, 0)),
        ] + [pl.BlockSpec(memory_space=pl.ANY)] * n_alias,
        out_specs=[
            pl.BlockSpec((rows, D_MODEL), lambda i: (rb0 + i, 0)),
            _layer_spec((ns, GLA_HEADS, GLA_DK, GLA_DV), lambda i: (layer, i, 0, 0, 0)),
            _layer_spec((ns, POOL_BUF, POOL_WIDTH), lambda i: (layer, i, 0, 0)),
        ],
        out_shape=[
            jax.ShapeDtypeStruct(cat.shape, BF16),
            jax.ShapeDtypeStruct((DEPTH, n_seq, GLA_HEADS, GLA_DK, GLA_DV), F32),
            jax.ShapeDtypeStruct((DEPTH, n_seq, POOL_BUF, POOL_WIDTH), F32),
        ],
        scratch_shapes=[pltpu.VMEM((ns, BUF_ROWS + t_len, POOL_WIDTH), F32)],
        input_output_aliases={n_in + k: k for k in range(n_alias)},
        compiler_params=pltpu.CompilerParams(
            dimension_semantics=("arbitrary",), vmem_limit_bytes=VMEM_LIMIT_BYTES),
        name="mixer_sample",
    )(qk, vgu, gk, s_in, buf_in, gla_norm, w_pool, pool_scale, cat, *stacked)


def _ffn_kernel(*refs, n_x, n_o, n_first_tiles, final_norm):
    x_refs = refs[:n_x]
    cat_ref, wout_ref, nrm_ref, wg_hbm, wu_hbm, wd_hbm, nf_ref = refs[n_x:n_x + 7]
    o_refs = refs[n_x + 7:n_x + 7 + n_o]
    h_ref, wg_buf, wu_buf, wd_buf, sem = refs[n_x + 7 + n_o:n_x + 12 + n_o]
    acc_ref = o_refs[0] if n_o == 1 else refs[n_x + 12 + n_o]
    i = pl.program_id(0)
    n_i = pl.num_programs(0)
    nj = wg_hbm.shape[0]

    def tile_copies(jt, slot):
        pairs = ((wg_hbm, wg_buf), (wu_hbm, wu_buf), (wd_hbm, wd_buf))
        return [pltpu.make_async_copy(src.at[jt], buf.at[slot], sem.at[k, slot])
                for k, (src, buf) in enumerate(pairs)]

    @pl.when(i == 0)
    def _():
        for c in tile_copies(0, 0):
            c.start()

    def start(x_ref):
        x2 = x_ref[...] + _dot(cat_ref[...], wout_ref[...])
        h_ref[...] = _rmsnorm(x2, nrm_ref[...]).astype(BF16)
        acc_ref[...] = x2

    _for_row_source(x_refs, n_first_tiles, start)

    def step(jt, carry):
        s = i * nj + jt
        slot = jnp.bitwise_and(s, 1)

        @pl.when(s + 1 < n_i * nj)
        def _():
            for c in tile_copies(jnp.where(jt + 1 < nj, jt + 1, 0), 1 - slot):
                c.start()

        for c in tile_copies(jt, slot):
            c.wait()

        h = h_ref[...]
        gate = _dot(h, wg_buf[slot])
        up = _dot(h, wu_buf[slot])
        act = (gate * jax.nn.sigmoid(gate)) * up
        acc_ref[...] += _dot(act.astype(BF16), wd_buf[slot])
        return carry

    lax.fori_loop(0, nj, step, 0)

    def result():
        return _rmsnorm(acc_ref[...], nf_ref[...]) if final_norm else acc_ref[...]

    if n_o == 1:
        if final_norm:
            acc_ref[...] = result()
    else:
        @pl.when(i < n_first_tiles)
        def _():
            o_refs[0][...] = result()

        @pl.when(i >= n_first_tiles)
        def _():
            acc_ref[...] = result()
            rows = pl.ds(pl.multiple_of((i - n_first_tiles) * TM_FFN, TM_FFN), TM_FFN)
            pltpu.sync_copy(acc_ref, o_refs[1].at[rows, :])


def _ffn(xs, cat, layer, w_out, nrm, w_gate_t, w_up_t, w_down_t, norm_final, *, final_norm, out_rows):
    m = sum(x.shape[0] for x in xs)
    n_first_tiles = (xs[0].shape[0] if len(xs) == 2 else out_rows[0]) // TM_FFN
    last = n_first_tiles - 1
    first_rows = lambda i: (jnp.minimum(i, last), 0)
    second_rows = lambda i: (jnp.maximum(i - n_first_tiles, 0), 0)
    tile = (TM_FFN, D_MODEL)
    if len(xs) == 1:
        x_specs = [pl.BlockSpec(tile, lambda i: (i, 0))]
    else:
        x_specs = [pl.BlockSpec(tile, first_rows),
                   pl.BlockSpec(tile, second_rows, pipeline_mode=pl.Buffered(1))]
    if len(out_rows) == 1:
        out_specs = [pl.BlockSpec(tile, lambda i: (i, 0))]
        acc_scratch = []
    else:
        out_specs = [pl.BlockSpec(tile, first_rows), pl.BlockSpec(memory_space=pl.ANY)]
        acc_scratch = [pltpu.VMEM(tile, F32)]
    return pl.pallas_call(
        functools.partial(_ffn_kernel, n_x=len(xs), n_o=len(out_rows),
                          n_first_tiles=n_first_tiles, final_norm=final_norm),
        grid=(m // TM_FFN,),
        in_specs=x_specs + [
            pl.BlockSpec(tile, lambda i: (i, 0)),
            pl.BlockSpec((D_MODEL, D_MODEL), lambda i: (0, 0), pipeline_mode=pl.Buffered(1)),
            _layer_spec((1, D_MODEL), lambda i: (layer, 0, 0)),
            pl.BlockSpec(memory_space=pl.ANY),
            pl.BlockSpec(memory_space=pl.ANY),
            pl.BlockSpec(memory_space=pl.ANY),
            pl.BlockSpec((1, D_MODEL), lambda i: (0, 0)),
        ],
        out_specs=out_specs,
        out_shape=[jax.ShapeDtypeStruct((r, D_MODEL), F32) for r in out_rows],
        scratch_shapes=[
            pltpu.VMEM(tile, BF16),
            pltpu.VMEM((2, D_MODEL, TF_FFN), BF16),
            pltpu.VMEM((2, D_MODEL, TF_FFN), BF16),
            pltpu.VMEM((2, TF_FFN, D_MODEL), BF16),
            pltpu.SemaphoreType.DMA((3, 2)),
        ] + acc_scratch,
        compiler_params=pltpu.CompilerParams(
            dimension_semantics=("arbitrary",), vmem_limit_bytes=VMEM_LIMIT_BYTES),
        name="ffn",
    )(*xs, cat, w_out, nrm, w_gate_t, w_up_t, w_down_t, norm_final)


def kernel(x_prompt, x_sample, state_gla, state_pool, norm_mix, w_in, w_gk_up, b_gk, gla_norm,
           w_pool, pool_scale, w_out, norm_ffn, w_gate, w_up, w_down, norm_final):
    n_p, t_p, _ = x_prompt.shape
    n_s, t_s, _ = x_sample.shape
    m_p = n_p * t_p
    m_s = n_s * t_s

    w_in_t = jnp.swapaxes(w_in, 1, 2).astype(BF16)
    gate0 = W_IN_GATE0
    w_lr_t = jnp.pad(w_in_t[:, gate0:gate0 + GLA_LOWRANK],
                     ((0, 0), (0, LANES - GLA_LOWRANK), (0, 0)))
    w_upp = jnp.pad(w_gk_up, ((0, 0), (0, LANES - GLA_LOWRANK), (0, 0))).astype(BF16)
    w_pool_b = w_pool.astype(BF16)
    ffn_weights = (w_out, w_gate, w_up, w_down)
    row3 = lambda a: a[:, None, :]
    norm_mix3, norm_ffn3, b_gk3 = row3(norm_mix), row3(norm_ffn), row3(b_gk)
    gla_norm3, pool_scale3 = row3(gla_norm), row3(pool_scale)

    state_pool_t = jnp.swapaxes(state_pool, 1, 2)
    s0_p = jnp.zeros((n_p, GLA_HEADS, GLA_DK, GLA_DV), F32)
    buf0_p = jnp.zeros((n_p, BUF_ROWS, POOL_WIDTH), F32)

    xs = [x_prompt.reshape(m_p, D_MODEL), x_sample.reshape(m_s, D_MODEL)]
    stacked_p, stacked_s = (), ()
    for l in range(DEPTH):
        last = l == DEPTH - 1
        qk, vgu, gk = _inproj(xs, l, norm_mix3, w_in_t, w_lr_t, w_upp, b_gk3)
        cat, *rest = _mixer_prompt(qk, vgu, gk, s0_p, buf0_p, l, gla_norm3, w_pool_b, pool_scale3,
                                   ffn_weights, (None, TF_FFN, TF_FFN, None), stacked_p,
                                   n_seq=n_p, seq_len=t_p, pos0=0)
        stacked_p, (w_out_b, w_gate_b, w_up_b, w_down_b) = rest[:2], rest[2:]
        cat, *stacked_s = _mixer_sample(qk, vgu, gk, state_gla, state_pool_t, l, gla_norm3, w_pool_b,
                                        pool_scale3, cat, stacked_s, n_seq=n_s, t_len=t_s,
                                        pos0=PAST_LEN, row0=m_p)
        w_down_t = w_down_b.reshape(D_FF // TF_FFN, TF_FFN, D_MODEL)
        xs = _ffn(xs, cat, l, w_out_b, norm_ffn3, w_gate_b, w_up_b, w_down_t, norm_final[None],
                  final_norm=last, out_rows=(m_p, m_s) if last else (m_p + m_s,))

    y_prompt = xs[0].reshape(n_p, t_p, D_MODEL)
    y_sample = xs[1].reshape(n_s, t_s, D_MODEL)
    return (y_prompt, y_sample, stacked_p[0], stacked_p[1], stacked_s[0], stacked_s[1])
```

```python
import functools

import jax
import jax.numpy as jnp
from jax import lax
from jax.experimental import pallas as pl
from jax.experimental.pallas import tpu as pltpu

F32 = jnp.float32
BF16 = jnp.bfloat16

D_MODEL = 2048
DEPTH = 2
EPS = 1e-6
GLA_WIDTH = D_MODEL // 2
GLA_HEADS = 4
GLA_DV = GLA_WIDTH // GLA_HEADS
GLA_DK = GLA_DV // 2
GLA_LOWRANK = 16
GK_NORM = 16.0
GLA_CHUNK = 64
POOL_WIDTH = D_MODEL - GLA_WIDTH
POOL_WINDOWS = (2, 4, 8, 16)
POOL_GC = POOL_WIDTH // len(POOL_WINDOWS)
POOL_BUF = 15
D_FF = ((8 * D_MODEL // 3 + 255) // 256) * 256
QK_WIDTH = GLA_HEADS * GLA_DK
W_IN_GATE0 = 2 * QK_WIDTH + 2 * GLA_WIDTH
MAIN_WIDTH = W_IN_GATE0 + POOL_WIDTH
VGU_WIDTH = MAIN_WIDTH - 2 * QK_WIDTH
K_OFF = QK_WIDTH
V_OFF = 0
G_OFF = GLA_WIDTH
U_OFF = 2 * GLA_WIDTH
Q_SCALE = GLA_DK ** -0.5
PAST_LEN = 16384

LANES = 128
SUBLANES = 8
VMEM_LIMIT_BYTES = 58 * 1024 * 1024
WEIGHT_DMA_PRIORITY = 1

TM_PROJ = 1024
TN_PROJ = 2 * QK_WIDTH
N_ALIGNED_TILES = W_IN_GATE0 // TN_PROJ
PROJ_K_CHUNK = 512
TM_FFN = 512
TF_FFN = 512
TT_PROMPT = 256
SUB = 16
ROW_CHUNK = 128
SEQ_PER_STEP = 8
BUF_ROWS = POOL_BUF + 1
EXT0 = BUF_ROWS + SUBLANES


def _rmsnorm(x, g):
    r = lax.rsqrt(jnp.mean(x * x, axis=-1, keepdims=True) + EPS)
    return (x * r) * g


def _dot(a, b):
    return jnp.dot(a, b, preferred_element_type=F32)


def _dot_nt(a, b):
    return lax.dot_general(a, b, (((1,), (1,)), ((), ())), preferred_element_type=F32)


def _dot_tn(a, b):
    return lax.dot_general(a, b, (((0,), (0,)), ((), ())), preferred_element_type=F32)


def _split3(x):
    hi = x.astype(BF16)
    r1 = x - hi.astype(F32)
    mid = r1.astype(BF16)
    lo = (r1 - mid.astype(F32)).astype(BF16)
    return hi, mid, lo


def _cumsum_rows(tri, g):
    hi, mid, lo = _split3(g)
    b3 = _dot(tri, jnp.concatenate([hi, mid, lo], axis=1))
    n = g.shape[1]
    return b3[:, :n] + b3[:, n:2 * n] + b3[:, 2 * n:]


def _layer_spec(block, index_map, **kw):
    return pl.BlockSpec((None,) + tuple(block), index_map, **kw)


def _split_rows_specs(tm, n_first_tiles, two_inputs):
    if not two_inputs:
        return [pl.BlockSpec((tm, D_MODEL), lambda i, j: (i, 0))]
    last = n_first_tiles - 1
    return [
        pl.BlockSpec((tm, D_MODEL), lambda i, j: (jnp.minimum(i, last), 0)),
        pl.BlockSpec((tm, D_MODEL), lambda i, j: (jnp.maximum(i - n_first_tiles, 0), 0),
                     pipeline_mode=pl.Buffered(1)),
    ]


def _for_row_source(x_refs, n_first_tiles, fn):
    if len(x_refs) == 1:
        fn(x_refs[0])
        return
    i = pl.program_id(0)
    pl.when(i < n_first_tiles)(lambda: fn(x_refs[0]))
    pl.when(i >= n_first_tiles)(lambda: fn(x_refs[1]))


def _inproj_kernel(*refs, n_x, n_first_tiles):
    x_refs = refs[:n_x]
    nrm_ref, w_ref, wlr_ref, wup_ref, bgk_ref, qk_ref, vgu_ref, gk_ref, h_ref = refs[n_x:]
    j = pl.program_id(1)

    @pl.when(j == 0)
    def _():
        def normalize(x_ref):
            def body(r, carry):
                rows = pl.ds(pl.multiple_of(r * ROW_CHUNK, ROW_CHUNK), ROW_CHUNK)
                h_ref[rows, :] = _rmsnorm(x_ref[rows, :], nrm_ref[...]).astype(BF16)
                return carry

            lax.fori_loop(0, TM_PROJ // ROW_CHUNK, body, 0)

        _for_row_source(x_refs, n_first_tiles, normalize)

        lr = _dot_nt(h_ref[...], wlr_ref[...])
        z = _dot(lr.astype(BF16), wup_ref[...]) + bgk_ref[...]
        gk_ref[...] = jax.nn.log_sigmoid(z) / GK_NORM

    kc = PROJ_K_CHUNK
    t = _dot_nt(h_ref[:, 0:kc], w_ref[0, :, 0:kc])
    for k0 in range(kc, D_MODEL, kc):
        t = t + _dot_nt(h_ref[:, k0:k0 + kc], w_ref[0, :, k0:k0 + kc])

    @pl.when(j == 0)
    def _():
        qk_ref[...] = t

    @pl.when(j > 0)
    def _():
        vgu_ref[...] = t.astype(BF16)


def _inproj(xs, layer, nrm, w_in_t, w_lr_t, w_up, b_gk):
    m = sum(x.shape[0] for x in xs)
    n_first_tiles = xs[0].shape[0] // TM_PROJ
    grid = (m // TM_PROJ, MAIN_WIDTH // TN_PROJ)

    def w_rows(j):
        start = jnp.where(j < N_ALIGNED_TILES, j * TN_PROJ, j * TN_PROJ + GLA_LOWRANK)
        return pl.multiple_of(start, GLA_LOWRANK)

    return pl.pallas_call(
        functools.partial(_inproj_kernel, n_x=len(xs), n_first_tiles=n_first_tiles),
        grid=grid,
        in_specs=_split_rows_specs(TM_PROJ, n_first_tiles, len(xs) == 2) + [
            _layer_spec((1, D_MODEL), lambda i, j: (layer, 0, 0)),
            pl.BlockSpec((pl.Element(1), pl.Element(TN_PROJ), pl.Element(D_MODEL)),
                         lambda i, j: (layer, w_rows(j), 0)),
            _layer_spec((LANES, D_MODEL), lambda i, j: (layer, 0, 0)),
            _layer_spec((LANES, QK_WIDTH), lambda i, j: (layer, 0, 0)),
            _layer_spec((1, QK_WIDTH), lambda i, j: (layer, 0, 0)),
        ],
        out_specs=[
            pl.BlockSpec((TM_PROJ, TN_PROJ), lambda i, j: (i, 0)),
            pl.BlockSpec((TM_PROJ, TN_PROJ), lambda i, j: (i, jnp.maximum(j - 1, 0))),
            pl.BlockSpec((TM_PROJ, QK_WIDTH), lambda i, j: (i, 0)),
        ],
        out_shape=[
            jax.ShapeDtypeStruct((m, TN_PROJ), F32),
            jax.ShapeDtypeStruct((m, VGU_WIDTH), BF16),
            jax.ShapeDtypeStruct((m, QK_WIDTH), F32),
        ],
        scratch_shapes=[pltpu.VMEM((TM_PROJ, D_MODEL), BF16)],
        compiler_params=pltpu.CompilerParams(
            dimension_semantics=("arbitrary", "arbitrary"), vmem_limit_bytes=VMEM_LIMIT_BYTES),
        name="inproj",
    )(*xs, nrm, w_in_t, w_lr_t, w_up, b_gk)


def _pool_output(s, u_cols, pos, gi, wpool_ref, ps_ref):
    cnt = jnp.minimum(POOL_WINDOWS[gi], pos + 1).astype(F32)
    d = s / cnt - u_cols
    cols = slice(gi * POOL_GC, (gi + 1) * POOL_GC)
    return _dot(d.astype(BF16), wpool_ref[gi]) * ps_ref[:, cols]


def _window_sums_doubling(ext_ref, lvl_ref, cols, w, n_rows):
    end = EXT0 + n_rows
    n_lvl = w.bit_length() - 1

    def read(lo, hi):
        return ext_ref[lo:hi, cols]

    for lvl in range(n_lvl):
        sh = 1 << lvl
        lo = EXT0 if lvl == n_lvl - 1 else SUBLANES
        val = read(lo, end) + read(lo - sh, end - sh)
        if lvl == n_lvl - 1:
            return val
        buf = lvl_ref.at[lvl % 2]
        buf[lo:end, :] = val

        def read(lo_, hi_, buf=buf):
            return buf[lo_:hi_, :]


def _gla_chunk(q, k, v, g, st_ref, masks):
    tri, row_id, pair_level, on_diag = masks
    c = GLA_CHUNK
    b = _cumsum_rows(tri, g)

    a = jnp.where(on_diag, jnp.sum(q * k, axis=-1, keepdims=True), 0.0)
    for lvl in range(c.bit_length() - 1):
        s = 1 << lvl
        f = jnp.exp(-jnp.abs(b - _segment_mid_rows(b, row_id, s)))
        upper = jnp.bitwise_and(row_id, s) != 0
        x = f * jnp.where(upper, q, k)
        qt = jnp.where(upper, x, 0.0).astype(BF16)
        kt = jnp.where(upper, 0.0, x).astype(BF16)
        a = jnp.where(pair_level == lvl, _dot_nt(qt, kt), a)

    vb = v.astype(BF16)
    o_intra = _dot(a.astype(BF16), vb)
    qe = (q * jnp.exp(b)).astype(BF16)
    outs = []
    for h in range(GLA_HEADS):
        sl = slice(h * c, (h + 1) * c)
        st = st_ref[h]
        outs.append(o_intra[sl] + _dot_nt(qe[sl], st.astype(BF16)))
        b_last = b[(h + 1) * c - 1:(h + 1) * c, :]
        k_dec = k[sl] * jnp.exp(b_last - b[sl])
        st_ref[h] = st * jnp.exp(b_last) + _dot_tn(vb[sl], k_dec.astype(BF16))
    return outs


def _segment_mid_rows(b, row_id, s):
    c, dk = b.shape
    seg = 2 * s
    if seg >= SUBLANES:
        b3 = b.reshape(c // seg, seg, dk)
        return jnp.broadcast_to(b3[:, s:s + 1, :], b3.shape).reshape(c, dk)
    b3 = b.reshape(c // SUBLANES, SUBLANES, dk)
    place = jnp.bitwise_and(row_id, seg - 1)
    out = b
    for p in range(seg):
        if p != s:
            below = pltpu.roll(b3, (p - s) % SUBLANES, axis=1).reshape(c, dk)
            out = jnp.where(place == p, below, out)
    return out


def _gla_masks():
    n = GLA_HEADS * GLA_CHUNK
    row = lax.broadcasted_iota(jnp.int32, (n, n), 0)
    col = lax.broadcasted_iota(jnp.int32, (n, n), 1)
    top_differing_bit = 31 - lax.clz(jnp.bitwise_xor(row, col))
    same_head = top_differing_bit < GLA_CHUNK.bit_length() - 1
    tri = jnp.logical_and(row >= col, same_head).astype(BF16)
    row_id = lax.broadcasted_iota(jnp.int32, (n, GLA_DK), 0)
    pair_level = jnp.where(col < row, top_differing_bit, -1)
    return tri, row_id, pair_level, row == col


def _gated_head_out(o, gt, gn):
    return _rmsnorm(o, gn) * (gt * jax.nn.sigmoid(gt))


def _mixer_prompt_kernel(qk_ref, vgu_ref, gk_ref, s0_ref, buf0_ref, gn_ref, wpool_ref, ps_ref, *rest,
                         pos0, n_alias, n_w):
    w32_refs = rest[:n_w]
    rest = rest[n_w + n_alias:]
    cat_ref, sout_ref, bufout_ref = rest[:3]
    w16_refs = rest[3:3 + n_w]
    st_ref, ext_ref, lvl_ref = rest[3 + n_w:]
    t = pl.program_id(1)
    nt = pl.num_programs(1)
    tt = TT_PROMPT

    for w32_ref, w16_ref in zip(w32_refs, w16_refs):
        if len(w16_ref.shape) == 2:
            w16_ref[...] = w32_ref[...].astype(BF16)
        else:
            width = w16_ref.shape[2]
            for ct in range(w16_ref.shape[0]):
                w16_ref[ct] = w32_ref[:, ct * width:(ct + 1) * width].astype(BF16)

    @pl.when(t == 0)
    def _():
        for h in range(GLA_HEADS):
            st_ref[h] = s0_ref[0, h].T
        ext_ref[0:SUBLANES, :] = jnp.zeros((SUBLANES, POOL_WIDTH), F32)
        lvl_ref[:, 0:SUBLANES, :] = jnp.zeros((2, SUBLANES, POOL_GC), F32)
        ext_ref[SUBLANES:EXT0, :] = buf0_ref[0]

    ext_ref[EXT0:EXT0 + tt, :] = vgu_ref[:, U_OFF:U_OFF + POOL_WIDTH].astype(F32)
    pos = pos0 + t * tt + lax.broadcasted_iota(jnp.int32, (tt, POOL_GC), 0)
    for gi, w in enumerate(POOL_WINDOWS):
        cols = slice(gi * POOL_GC, (gi + 1) * POOL_GC)
        s = _window_sums_doubling(ext_ref, lvl_ref, cols, w, tt)
        y = _pool_output(s, ext_ref[EXT0:EXT0 + tt, cols], pos, gi, wpool_ref, ps_ref)
        cat_ref[:, GLA_WIDTH + gi * POOL_GC:GLA_WIDTH + (gi + 1) * POOL_GC] = y.astype(BF16)
    ext_ref[SUBLANES:EXT0, :] = ext_ref[tt + SUBLANES:tt + EXT0, :]

    masks = _gla_masks()

    def chunk_body(ci, carry_):
        rows = pl.ds(pl.multiple_of(ci * GLA_CHUNK, GLA_CHUNK), GLA_CHUNK)

        def heads_on_rows(ref, off, width):
            return jnp.concatenate(
                [ref[rows, off + h * width:off + (h + 1) * width] for h in range(GLA_HEADS)], axis=0)

        outs = _gla_chunk(heads_on_rows(qk_ref, 0, GLA_DK) * Q_SCALE,
                          heads_on_rows(qk_ref, K_OFF, GLA_DK),
                          heads_on_rows(vgu_ref, V_OFF, GLA_DV),
                          heads_on_rows(gk_ref, 0, GLA_DK), st_ref, masks)
        for h in range(GLA_HEADS):
            vcols = slice(h * GLA_DV, (h + 1) * GLA_DV)
            gt = vgu_ref[rows, G_OFF + h * GLA_DV:G_OFF + (h + 1) * GLA_DV].astype(F32)
            cat_ref[rows, vcols] = _gated_head_out(outs[h], gt, gn_ref[...]).astype(BF16)
        return carry_

    lax.fori_loop(0, tt // GLA_CHUNK, chunk_body, 0, unroll=True)

    @pl.when(t == nt - 1)
    def _():
        for h in range(GLA_HEADS):
            sout_ref[0, h] = st_ref[h].T
        bufout_ref[0] = ext_ref[EXT0 - POOL_BUF:EXT0, :]


def _mixer_prompt(qk, vgu, gk, s0, buf0, layer, gla_norm, w_pool, pool_scale, ffn_weights, col_tiles,
                  stacked, *, n_seq, seq_len, pos0):
    tt = TT_PROMPT
    nt = seq_len // tt
    n_steps = n_seq * nt
    row = lambda b, t: b * nt + t
    n_alias = len(stacked)
    n_in = 8 + len(ffn_weights)
    w_blocks = [(w.shape[1] // n_steps, w.shape[2]) for w in ffn_weights]
    w_out_specs, w_out_shapes = [], []
    for w, (rows, cols), ct in zip(ffn_weights, w_blocks, col_tiles):
        if ct is None:
            w_out_specs.append(pl.BlockSpec((rows, cols), lambda b, t: (row(b, t), 0)))
            w_out_shapes.append(jax.ShapeDtypeStruct(w.shape[1:], BF16))
        else:
            w_out_specs.append(pl.BlockSpec((cols // ct, rows, ct), lambda b, t: (0, row(b, t), 0)))
            w_out_shapes.append(jax.ShapeDtypeStruct((cols // ct, w.shape[1], ct), BF16))
    return pl.pallas_call(
        functools.partial(_mixer_prompt_kernel, pos0=pos0, n_alias=n_alias, n_w=len(ffn_weights)),
        grid=(n_seq, nt),
        in_specs=[
            pl.BlockSpec((tt, 2 * QK_WIDTH), lambda b, t: (row(b, t), 0)),
            pl.BlockSpec((tt, VGU_WIDTH), lambda b, t: (row(b, t), 0)),
            pl.BlockSpec((tt, QK_WIDTH), lambda b, t: (row(b, t), 0)),
            pl.BlockSpec((1, GLA_HEADS, GLA_DK, GLA_DV), lambda b, t: (b, 0, 0, 0)),
            pl.BlockSpec((1, BUF_ROWS, POOL_WIDTH), lambda b, t: (b, 0, 0)),
            _layer_spec((1, GLA_DV), lambda b, t: (layer, 0, 0)),
            _layer_spec((len(POOL_WINDOWS), POOL_GC, POOL_GC), lambda b, t: (layer, 0, 0, 0)),
            _layer_spec((1, POOL_WIDTH), lambda b, t: (layer, 0, 0)),
        ] + [_layer_spec(blk, lambda b, t: (layer, row(b, t), 0)) for blk in w_blocks]
        + [pl.BlockSpec(memory_space=pl.ANY)] * n_alias,
        out_specs=[
            pl.BlockSpec((tt, D_MODEL), lambda b, t: (row(b, t), 0)),
            _layer_spec((1, GLA_HEADS, GLA_DK, GLA_DV), lambda b, t: (layer, b, 0, 0, 0)),
            _layer_spec((1, POOL_BUF, POOL_WIDTH), lambda b, t: (layer, b, 0, 0)),
        ] + w_out_specs,
        out_shape=[
            jax.ShapeDtypeStruct((qk.shape[0], D_MODEL), BF16),
            jax.ShapeDtypeStruct((DEPTH, n_seq, GLA_HEADS, GLA_DK, GLA_DV), F32),
            jax.ShapeDtypeStruct((DEPTH, n_seq, POOL_BUF, POOL_WIDTH), F32),
        ] + w_out_shapes,
        scratch_shapes=[
            pltpu.VMEM((GLA_HEADS, GLA_DV, GLA_DK), F32),
            pltpu.VMEM((EXT0 + tt, POOL_WIDTH), F32),
            pltpu.VMEM((2, EXT0 + tt, POOL_GC), F32),
        ],
        input_output_aliases={n_in + k: 1 + k for k in range(n_alias)},
        compiler_params=pltpu.CompilerParams(
            dimension_semantics=("arbitrary", "arbitrary"), vmem_limit_bytes=VMEM_LIMIT_BYTES),
        name="mixer_prompt",
    )(qk, vgu, gk, s0, buf0, gla_norm, w_pool, pool_scale, *ffn_weights, *stacked)


def _mixer_sample_kernel(qk_ref, vgu_ref, gk_ref, s0_ref, buf0_ref, gn_ref, wpool_ref, ps_ref, *rest,
                         pos0, t_len, n_alias):
    cat_ref, sout_ref, bufout_ref, ext_ref = rest[n_alias:]
    ns = SEQ_PER_STEP
    rows = ns * t_len

    for r in range(POOL_BUF):
        ext_ref[:, 1 + r, :] = buf0_ref[r]
    ext_ref[:, BUF_ROWS:BUF_ROWS + t_len, :] = (
        vgu_ref[:, U_OFF:U_OFF + POOL_WIDTH].astype(F32).reshape(ns, t_len, POOL_WIDTH))
    pos = pos0 + lax.broadcasted_iota(jnp.int32, (ns, t_len, POOL_GC), 1)
    for gi in range(len(POOL_WINDOWS)):
        cols = slice(gi * POOL_GC, (gi + 1) * POOL_GC)
        u_cols = ext_ref[:, BUF_ROWS:BUF_ROWS + t_len, cols]
        w = POOL_WINDOWS[gi]
        s = u_cols
        for sft in range(1, w):
            s = s + ext_ref[:, BUF_ROWS - sft:BUF_ROWS - sft + t_len, cols]
        cnt = jnp.minimum(w, pos + 1).astype(F32)
        d = (s / cnt - u_cols).reshape(rows, POOL_GC)
        y = _dot(d.astype(BF16), wpool_ref[gi]) * ps_ref[:, cols]
        cat_ref[:, GLA_WIDTH + gi * POOL_GC:GLA_WIDTH + (gi + 1) * POOL_GC] = y.astype(BF16)
    bufout_ref[...] = ext_ref[:, t_len + 1:t_len + BUF_ROWS, :]

    r_i = lax.broadcasted_iota(jnp.int32, (rows, rows), 0)
    c_i = lax.broadcasted_iota(jnp.int32, (rows, rows), 1)
    same_seq = (r_i - jnp.bitwise_and(r_i, t_len - 1)) == (c_i - jnp.bitwise_and(c_i, t_len - 1))
    tri = jnp.logical_and(r_i >= c_i, same_seq).astype(BF16)
    row_in_seq = lax.broadcasted_iota(jnp.int32, (ns, t_len, 1), 1)
    seq_of_row = lax.broadcasted_iota(jnp.int32, (rows, 1), 0) // t_len
    zero_pad = jnp.zeros((LANES - rows, GLA_DK), F32)

    for h in range(GLA_HEADS):
        kcols = slice(h * GLA_DK, (h + 1) * GLA_DK)
        vcols = slice(h * GLA_DV, (h + 1) * GLA_DV)
        qh = qk_ref[:, h * GLA_DK:(h + 1) * GLA_DK] * Q_SCALE
        kh = qk_ref[:, K_OFF + h * GLA_DK:K_OFF + (h + 1) * GLA_DK]
        vb = vgu_ref[:, V_OFF + h * GLA_DV:V_OFF + (h + 1) * GLA_DV]
        vh = vb.astype(F32)
        gt = vgu_ref[:, G_OFF + h * GLA_DV:G_OFF + (h + 1) * GLA_DV].astype(F32)
        b = _cumsum_rows(tri, gk_ref[:, kcols])
        q3 = qh.reshape(ns, t_len, GLA_DK)
        k3 = kh.reshape(ns, t_len, GLA_DK)
        b3 = b.reshape(ns, t_len, GLA_DK)
        v3 = vh.reshape(ns, t_len, GLA_DV)
        o3 = jnp.zeros((ns, t_len, GLA_DV), F32)
        for jj in range(t_len):
            p = q3 * (k3[:, jj:jj + 1, :] * jnp.exp(jnp.minimum(b3 - b3[:, jj:jj + 1, :], 0.0)))
            col = jnp.sum(p, axis=-1, keepdims=True)
            col = jnp.where(row_in_seq >= jj, col, 0.0)
            o3 = o3 + col * v3[:, jj:jj + 1, :]
        o = o3.reshape(rows, GLA_DV)

        qe = qh * jnp.exp(b)
        k_dec = (k3 * jnp.exp(b3[:, t_len - 1:t_len, :] - b3)).reshape(rows, GLA_DK)
        b_t = jnp.concatenate([b, zero_pad], axis=0).T
        for s in range(ns):
            mine = seq_of_row == s
            s0 = s0_ref[s, h]
            o = o + _dot(jnp.where(mine, qe, 0.0).astype(BF16), s0.astype(BF16))
            last = s * t_len + t_len - 1
            a_col = jnp.exp(b_t[:, last:last + 1])
            upd = _dot_tn(jnp.where(mine, k_dec, 0.0).astype(BF16), vb)
            sout_ref[s, h] = a_col * s0 + upd
        cat_ref[:, vcols] = _gated_head_out(o, gt, gn_ref[...]).astype(BF16)


def _mixer_sample(qk, vgu, gk, s_in, buf_in, layer, gla_norm, w_pool, pool_scale, cat, stacked, *,
                  n_seq, t_len, pos0, row0):
    ns = SEQ_PER_STEP
    rows = ns * t_len
    rb0 = row0 // rows
    n_alias = 1 + len(stacked)
    n_in = 8
    return pl.pallas_call(
        functools.partial(_mixer_sample_kernel, pos0=pos0, t_len=t_len, n_alias=n_alias),
        grid=(n_seq // ns,),
        in_specs=[
            pl.BlockSpec((rows, 2 * QK_WIDTH), lambda i: (rb0 + i, 0)),
            pl.BlockSpec((rows, VGU_WIDTH), lambda i: (rb0 + i, 0)),
            pl.BlockSpec((rows, QK_WIDTH), lambda i: (rb0 + i, 0)),
            _layer_spec((ns, GLA_HEADS, GLA_DK, GLA_DV), lambda i: (layer, i, 0, 0, 0)),
            _layer_spec((POOL_BUF, ns, POOL_WIDTH), lambda i: (layer, 0, i, 0)),
            _layer_spec((1, GLA_DV), lambda i: (layer, 0, 0)),
            _layer_spec((len(POOL_WINDOWS), POOL_GC, POOL_GC), lambda i: (layer, 0, 0, 0)),
            _layer_spec((1, POOL_WIDTH), lambda i: (layer, 0, 0)),
        ] + [pl.BlockSpec(memory_space=pl.ANY)] * n_alias,
        out_specs=[
            pl.BlockSpec((rows, D_MODEL), lambda i: (rb0 + i, 0)),
            _layer_spec((ns, GLA_HEADS, GLA_DK, GLA_DV), lambda i: (layer, i, 0, 0, 0)),
            _layer_spec((ns, POOL_BUF, POOL_WIDTH), lambda i: (layer, i, 0, 0)),
        ],
        out_shape=[
            jax.ShapeDtypeStruct(cat.shape, BF16),
            jax.ShapeDtypeStruct((DEPTH, n_seq, GLA_HEADS, GLA_DK, GLA_DV), F32),
            jax.ShapeDtypeStruct((DEPTH, n_seq, POOL_BUF, POOL_WIDTH), F32),
        ],
        scratch_shapes=[pltpu.VMEM((ns, BUF_ROWS + t_len, POOL_WIDTH), F32)],
        input_output_aliases={n_in + k: k for k in range(n_alias)},
        compiler_params=pltpu.CompilerParams(
            dimension_semantics=("arbitrary",), vmem_limit_bytes=VMEM_LIMIT_BYTES),
        name="mixer_sample",
    )(qk, vgu, gk, s_in, buf_in, gla_norm, w_pool, pool_scale, cat, *stacked)


def _ffn_kernel(*refs, n_x, n_o, n_first_tiles, final_norm):
    x_refs = refs[:n_x]
    cat_ref, wout_ref, nrm_ref, wg_hbm, wu_hbm, wd_hbm, nf_ref = refs[n_x:n_x + 7]
    o_refs = refs[n_x + 7:n_x + 7 + n_o]
    h_ref, wg_buf, wu_buf, wd_buf, sem = refs[n_x + 7 + n_o:n_x + 12 + n_o]
    acc_ref = o_refs[0] if n_o == 1 else refs[n_x + 12 + n_o]
    i = pl.program_id(0)
    n_i = pl.num_programs(0)
    nj = wg_hbm.shape[0]

    def tile_copies(jt, slot):
        pairs = ((wg_hbm, wg_buf), (wu_hbm, wu_buf), (wd_hbm, wd_buf))
        return [pltpu.make_async_copy(src.at[jt], buf.at[slot], sem.at[k, slot])
                for k, (src, buf) in enumerate(pairs)]

    @pl.when(i == 0)
    def _():
        for c in tile_copies(0, 0):
            c.start(priority=WEIGHT_DMA_PRIORITY)

    def start(x_ref):
        x2 = x_ref[...] + _dot(cat_ref[...], wout_ref[...])
        h_ref[...] = _rmsnorm(x2, nrm_ref[...]).astype(BF16)
        acc_ref[...] = x2

    _for_row_source(x_refs, n_first_tiles, start)

    def step(jt, carry):
        s = i * nj + jt
        slot = jnp.bitwise_and(s, 1)

        @pl.when(s + 1 < n_i * nj)
        def _():
            for c in tile_copies(jnp.where(jt + 1 < nj, jt + 1, 0), 1 - slot):
                c.start(priority=WEIGHT_DMA_PRIORITY)

        for c in tile_copies(jt, slot):
            c.wait()

        h = h_ref[...]
        gate = _dot(h, wg_buf[slot])
        up = _dot(h, wu_buf[slot])
        act = (gate * jax.nn.sigmoid(gate)) * up
        acc_ref[...] += _dot(act.astype(BF16), wd_buf[slot])
        return carry

    lax.fori_loop(0, nj, step, 0)

    def result():
        return _rmsnorm(acc_ref[...], nf_ref[...]) if final_norm else acc_ref[...]

    if n_o == 1:
        if final_norm:
            acc_ref[...] = result()
    else:
        @pl.when(i < n_first_tiles)
        def _():
            o_refs[0][...] = result()

        @pl.when(i >= n_first_tiles)
        def _():
            acc_ref[...] = result()
            rows = pl.ds(pl.multiple_of((i - n_first_tiles) * TM_FFN, TM_FFN), TM_FFN)
            pltpu.sync_copy(acc_ref, o_refs[1].at[rows, :])


def _ffn(xs, cat, layer, w_out, nrm, w_gate_t, w_up_t, w_down_t, norm_final, *, final_norm, out_rows):
    m = sum(x.shape[0] for x in xs)
    n_first_tiles = (xs[0].shape[0] if len(xs) == 2 else out_rows[0]) // TM_FFN
    last = n_first_tiles - 1
    first_rows = lambda i: (jnp.minimum(i, last), 0)
    second_rows = lambda i: (jnp.maximum(i - n_first_tiles, 0), 0)
    tile = (TM_FFN, D_MODEL)
    if len(xs) == 1:
        x_specs = [pl.BlockSpec(tile, lambda i: (i, 0))]
    else:
        x_specs = [pl.BlockSpec(tile, first_rows),
                   pl.BlockSpec(tile, second_rows, pipeline_mode=pl.Buffered(1))]
    if len(out_rows) == 1:
        out_specs = [pl.BlockSpec(tile, lambda i: (i, 0))]
        acc_scratch = []
    else:
        out_specs = [pl.BlockSpec(tile, first_rows), pl.BlockSpec(memory_space=pl.ANY)]
        acc_scratch = [pltpu.VMEM(tile, F32)]
    return pl.pallas_call(
        functools.partial(_ffn_kernel, n_x=len(xs), n_o=len(out_rows),
                          n_first_tiles=n_first_tiles, final_norm=final_norm),
        grid=(m // TM_FFN,),
        in_specs=x_specs + [
            pl.BlockSpec(tile, lambda i: (i, 0)),
            pl.BlockSpec((D_MODEL, D_MODEL), lambda i: (0, 0), pipeline_mode=pl.Buffered(1)),
            _layer_spec((1, D_MODEL), lambda i: (layer, 0, 0)),
            pl.BlockSpec(memory_space=pl.ANY),
            pl.BlockSpec(memory_space=pl.ANY),
            pl.BlockSpec(memory_space=pl.ANY),
            pl.BlockSpec((1, D_MODEL), lambda i: (0, 0)),
        ],
        out_specs=out_specs,
        out_shape=[jax.ShapeDtypeStruct((r, D_MODEL), F32) for r in out_rows],
        scratch_shapes=[
            pltpu.VMEM(tile, BF16),
            pltpu.VMEM((2, D_MODEL, TF_FFN), BF16),
            pltpu.VMEM((2, D_MODEL, TF_FFN), BF16),
            pltpu.VMEM((2, TF_FFN, D_MODEL), BF16),
            pltpu.SemaphoreType.DMA((3, 2)),
        ] + acc_scratch,
        compiler_params=pltpu.CompilerParams(
            dimension_semantics=("arbitrary",), vmem_limit_bytes=VMEM_LIMIT_BYTES),
        name="ffn",
    )(*xs, cat, w_out, nrm, w_gate_t, w_up_t, w_down_t, norm_final)


def kernel(x_prompt, x_sample, state_gla, state_pool, norm_mix, w_in, w_gk_up, b_gk, gla_norm,
           w_pool, pool_scale, w_out, norm_ffn, w_gate, w_up, w_down, norm_final):
    n_p, t_p, _ = x_prompt.shape
    n_s, t_s, _ = x_sample.shape
    m_p = n_p * t_p
    m_s = n_s * t_s

    w_in_t = jnp.swapaxes(w_in, 1, 2).astype(BF16)
    gate0 = W_IN_GATE0
    w_lr_t = jnp.pad(w_in_t[:, gate0:gate0 + GLA_LOWRANK],
                     ((0, 0), (0, LANES - GLA_LOWRANK), (0, 0)))
    w_upp = jnp.pad(w_gk_up, ((0, 0), (0, LANES - GLA_LOWRANK), (0, 0))).astype(BF16)
    w_pool_b = w_pool.astype(BF16)
    ffn_weights = (w_out, w_gate, w_up, w_down)
    row3 = lambda a: a[:, None, :]
    norm_mix3, norm_ffn3, b_gk3 = row3(norm_mix), row3(norm_ffn), row3(b_gk)
    gla_norm3, pool_scale3 = row3(gla_norm), row3(pool_scale)

    state_pool_t = jnp.swapaxes(state_pool, 1, 2)
    s0_p = jnp.zeros((n_p, GLA_HEADS, GLA_DK, GLA_DV), F32)
    buf0_p = jnp.zeros((n_p, BUF_ROWS, POOL_WIDTH), F32)

    xs = [x_prompt.reshape(m_p, D_MODEL), x_sample.reshape(m_s, D_MODEL)]
    stacked_p, stacked_s = (), ()
    for l in range(DEPTH):
        last = l == DEPTH - 1
        qk, vgu, gk = _inproj(xs, l, norm_mix3, w_in_t, w_lr_t, w_upp, b_gk3)
        cat, *rest = _mixer_prompt(qk, vgu, gk, s0_p, buf0_p, l, gla_norm3, w_pool_b, pool_scale3,
                                   ffn_weights, (None, TF_FFN, TF_FFN, None), stacked_p,
                                   n_seq=n_p, seq_len=t_p, pos0=0)
        stacked_p, (w_out_b, w_gate_b, w_up_b, w_down_b) = rest[:2], rest[2:]
        cat, *stacked_s = _mixer_sample(qk, vgu, gk, state_gla, state_pool_t, l, gla_norm3, w_pool_b,
                                        pool_scale3, cat, stacked_s, n_seq=n_s, t_len=t_s,
                                        pos0=PAST_LEN, row0=m_p)
        w_down_t = w_down_b.reshape(D_FF // TF_FFN, TF_FFN, D_MODEL)
        xs = _ffn(xs, cat, l, w_out_b, norm_ffn3, w_gate_b, w_up_b, w_down_t, norm_final[None],
                  final_norm=last, out_rows=(m_p, m_s) if last else (m_p + m_s,))

    y_prompt = xs[0].reshape(n_p, t_p, D_MODEL)
    y_sample = xs[1].reshape(n_s, t_s, D_MODEL)
    return (y_prompt, y_sample, stacked_p[0], stacked_p[1], stacked_s[0], stacked_s[1])
```
